```python
import jax, jax.numpy as jnp
from jax import lax
import numpy as np

D_MODEL = 1024
BATCH = 32
SEQ = 2048
DEPTH = 2

GRID_W = 64
N_HEADS = 8
N_KV_HEADS = 2
HEAD_DIM = 64
Q_GROUP = N_HEADS // N_KV_HEADS
ROPE_THETA = 10000.0
Q_BLOCK = 128
SGU_GROUPS = 4
SGU_HEAD = 64
SGU_W = SGU_GROUPS * SGU_HEAD
SGU_CHUNK = 128
POOL_WINDOWS = (2, 4, 8, 16)
POOL_GROUPS = len(POOL_WINDOWS)
POOL_HEAD = 64
POOL_W = POOL_GROUPS * POOL_HEAD
Q_W = N_HEADS * HEAD_DIM
KV_W = N_KV_HEADS * HEAD_DIM
N_BRANCHES = 3
GATE_W = N_BRANCHES * D_MODEL
SPLITS = (Q_W, KV_W, KV_W, SGU_W, SGU_W, POOL_W, GATE_W)
IN_W = sum(SPLITS)
N_EXPERTS = 16
CAPACITY_FACTOR = 2
D_FF_EXPERT = 1024
EPS = 1e-6

kernel_name = "hybrid_gated_attn_sgu_pool_ecmoe"


def rmsnorm(x, g):
    xf = x.astype(jnp.float32)
    y = xf * lax.rsqrt(jnp.mean(xf * xf, axis=-1, keepdims=True) + EPS)
    return (y * g.astype(jnp.float32)).astype(x.dtype)


def axial_rope_tables(seq_len):
    rows = seq_len // GRID_W
    row = jnp.broadcast_to(jnp.arange(rows, dtype=jnp.float32)[:, None], (rows, GRID_W)).reshape(-1)
    col = jnp.broadcast_to(jnp.arange(GRID_W, dtype=jnp.float32)[None, :], (rows, GRID_W)).reshape(-1)
    part = HEAD_DIM // 2
    freqs = ROPE_THETA ** (-jnp.arange(0, part, 2, dtype=jnp.float32) / part)
    ang = jnp.stack([row[:, None] * freqs, col[:, None] * freqs], axis=1)
    return jnp.cos(ang), jnp.sin(ang)


def apply_axial_rope(x, cos, sin):
    B, S, H, _ = x.shape
    nf = HEAD_DIM // 4
    xf = x.astype(jnp.float32).reshape(B, S, H, 2, 2, nf)
    x1, x2 = xf[..., 0, :], xf[..., 1, :]
    c = cos[None, :, None]
    s = sin[None, :, None]
    out = jnp.stack([x1 * c - x2 * s, x2 * c + x1 * s], axis=-2)
    return out.reshape(B, S, H, HEAD_DIM).astype(x.dtype)


def block_attention(q, k, v):
    B, S = q.shape[:2]
    nb = S // Q_BLOCK
    scale = HEAD_DIM ** -0.5
    qb = q.reshape(B, nb, Q_BLOCK, N_KV_HEADS, Q_GROUP, HEAD_DIM).transpose(1, 0, 2, 3, 4, 5)

    def one_block(qblk):
        s = jnp.einsum('bqkgd,bskd->bkgqs', qblk, k).astype(jnp.float32) * scale
        p = jax.nn.softmax(s, axis=-1).astype(v.dtype)
        return jnp.einsum('bkgqs,bskd->bqkgd', p, v)

    o = lax.map(one_block, qb)
    return o.transpose(1, 0, 2, 3, 4, 5).reshape(B, S, Q_W)


def spatial_gating(u, v, g_sgu, w_spatial, b_spatial):
    B, S, _ = v.shape
    nc = S // SGU_CHUNK
    vn = rmsnorm(v, g_sgu).reshape(B, nc, SGU_CHUNK, SGU_GROUPS, SGU_HEAD)
    z = jnp.einsum('gpq,bcqgd->bcpgd', w_spatial, vn) + b_spatial.T[None, None, :, :, None]
    return u * z.reshape(B, S, SGU_W)


def multiscale_pool(p, w_pool, pool_scale):
    B, S, _ = p.shape
    pf = p.astype(jnp.float32)
    cs = jnp.concatenate([jnp.zeros((B, 1, POOL_W), jnp.float32), jnp.cumsum(pf, axis=1)], axis=1)
    t = jnp.arange(S)
    means = []
    for g, w in enumerate(POOL_WINDOWS):
        lo = jnp.clip(t - w // 2, 0, S - 1)
        hi = jnp.clip(t + (w - 1 - w // 2), 0, S - 1)
        seg = cs[:, :, g * POOL_HEAD:(g + 1) * POOL_HEAD]
        cnt = (hi - lo + 1).astype(jnp.float32)
        means.append((jnp.take(seg, hi + 1, axis=1) - jnp.take(seg, lo, axis=1)) / cnt[None, :, None])
    pooled = jnp.concatenate(means, axis=-1)
    d = (pooled - pf).astype(p.dtype).reshape(B, S, POOL_GROUPS, POOL_HEAD)
    y = jnp.einsum('bsgc,gcd->bsgd', d, w_pool).reshape(B, S, POOL_W)
    return y * pool_scale


def expert_choice_moe(h, w_router, w_gate_e, w_up_e, w_down_e):
    B, S, D = h.shape
    cap = CAPACITY_FACTOR * S // N_EXPERTS
    aff = jax.nn.softmax(jnp.einsum('bsd,de->bse', h, w_router).astype(jnp.float32), axis=-1)
    vals, idx = lax.top_k(aff.transpose(0, 2, 1), cap)
    xs = jax.vmap(lambda hb, ib: hb[ib])(h, idx)
    a = jnp.einsum('becd,edf->becf', xs, w_gate_e)
    b = jnp.einsum('becd,edf->becf', xs, w_up_e)
    out = jnp.einsum('becf,efd->becd', jax.nn.silu(a) * b, w_down_e)
    out = out * vals[..., None].astype(out.dtype)
    return jax.vmap(lambda ib, ob: jnp.zeros((S, D), ob.dtype).at[ib.reshape(-1)].add(ob.reshape(-1, D)))(idx, out)


def setup_inputs(seed: int = 0) -> dict:
    key = jax.random.key(seed)
    ks = jax.random.split(key, 24)
    f32 = jnp.float32

    def nrm(k, shape, fan_in):
        return jax.random.normal(k, shape, f32) * (fan_in ** -0.5)

    def gain(k, shape, s=0.05):
        return 1.0 + s * jax.random.normal(k, shape, f32)

    L = DEPTH
    return {
        "x": jax.random.normal(ks[0], (BATCH, SEQ, D_MODEL), f32),
        "g_mix": gain(ks[1], (L, D_MODEL)),
        "w_in": nrm(ks[2], (L, D_MODEL, IN_W), D_MODEL),
        "g_q": gain(ks[3], (L, HEAD_DIM)),
        "g_k": gain(ks[4], (L, HEAD_DIM)),
        "g_sgu": gain(ks[5], (L, SGU_W)),
        "w_spatial": nrm(ks[6], (L, SGU_GROUPS, SGU_CHUNK, SGU_CHUNK), SGU_CHUNK),
        "b_spatial": gain(ks[7], (L, SGU_GROUPS, SGU_CHUNK)),
        "w_pool": nrm(ks[8], (L, POOL_GROUPS, POOL_HEAD, POOL_HEAD), POOL_HEAD),
        "pool_scale": gain(ks[9], (L, POOL_W), 0.1),
        "w_attn_o": nrm(ks[10], (L, Q_W, D_MODEL), Q_W),
        "w_sgu_o": nrm(ks[11], (L, SGU_W, D_MODEL), SGU_W),
        "w_pool_o": nrm(ks[12], (L, POOL_W, D_MODEL), POOL_W),
        "w_out": nrm(ks[13], (L, D_MODEL, D_MODEL), D_MODEL),
        "g_ffn": gain(ks[14], (L, D_MODEL)),
        "w_router": nrm(ks[15], (L, D_MODEL, N_EXPERTS), D_MODEL),
        "w_gate_e": nrm(ks[16], (L, N_EXPERTS, D_MODEL, D_FF_EXPERT), D_MODEL),
        "w_up_e": nrm(ks[17], (L, N_EXPERTS, D_MODEL, D_FF_EXPERT), D_MODEL),
        "w_down_e": nrm(ks[18], (L, N_EXPERTS, D_FF_EXPERT, D_MODEL), D_FF_EXPERT),
        "g_final": gain(ks[19], (D_MODEL,)),
    }


def reference(x, g_mix, w_in, g_q, g_k, g_sgu, w_spatial, b_spatial, w_pool, pool_scale,
              w_attn_o, w_sgu_o, w_pool_o, w_out, g_ffn, w_router, w_gate_e, w_up_e, w_down_e, g_final):
    B, S, D = x.shape
    cos, sin = axial_rope_tables(S)
    cuts = [int(c) for c in np.cumsum(SPLITS)[:-1]]
    for l in range(DEPTH):
        h = rmsnorm(x, g_mix[l])
        proj = jnp.einsum('bsd,dn->bsn', h, w_in[l])
        q, k, v, u, vs, p, gl = jnp.split(proj, cuts, axis=-1)
        q = apply_axial_rope(rmsnorm(q.reshape(B, S, N_HEADS, HEAD_DIM), g_q[l]), cos, sin)
        k = apply_axial_rope(rmsnorm(k.reshape(B, S, N_KV_HEADS, HEAD_DIM), g_k[l]), cos, sin)
        v = v.reshape(B, S, N_KV_HEADS, HEAD_DIM)
        br_a = jnp.einsum('bsc,cd->bsd', block_attention(q, k, v), w_attn_o[l])
        br_b = jnp.einsum('bsc,cd->bsd', spatial_gating(u, vs, g_sgu[l], w_spatial[l], b_spatial[l]), w_sgu_o[l])
        br_c = jnp.einsum('bsc,cd->bsd', multiscale_pool(p, w_pool[l], pool_scale[l]), w_pool_o[l])
        ga, gb, gc = jnp.split(jax.nn.sigmoid(gl), N_BRANCHES, axis=-1)
        merged = ga * br_a + gb * br_b + gc * br_c
        x = x + jnp.einsum('bsd,de->bse', merged, w_out[l])
        h2 = rmsnorm(x, g_ffn[l])
        x = x + expert_choice_moe(h2, w_router[l], w_gate_e[l], w_up_e[l], w_down_e[l])
    return rmsnorm(x, g_final)
```

```python
import functools
import math

import jax
import jax.numpy as jnp
from jax import lax
from jax.experimental import pallas as pl
from jax.experimental.pallas import tpu as pltpu

F32 = jnp.float32
BF16 = jnp.bfloat16

GRID_W = 64
N_HEADS = 8
N_KV_HEADS = 2
HEAD_DIM = 64
ROPE_THETA = 10000.0
SGU_GROUPS = 4
SGU_HEAD = 64
SGU_W = SGU_GROUPS * SGU_HEAD
SGU_CHUNK = 128
POOL_WINDOWS = (2, 4, 8, 16)
POOL_HEAD = 64
POOL_W = len(POOL_WINDOWS) * POOL_HEAD
POOL_HALO = 8
Q_W = N_HEADS * HEAD_DIM
KV_W = N_KV_HEADS * HEAD_DIM
N_EXPERTS = 16
CAPACITY_FACTOR = 2
EPS = 1e-6

LANES = 128
A_W = Q_W + 2 * KV_W + 2 * SGU_W + POOL_W
VMEM_LIMIT = 56 * 1024 * 1024


def _cparams(*sem):
    return pltpu.CompilerParams(dimension_semantics=sem, vmem_limit_bytes=VMEM_LIMIT)


def _dot(a, b):
    return jnp.dot(a, b, preferred_element_type=F32)


def _dot_nt(a, b):
    return lax.dot_general(a, b, (((1,), (1,)), ((), ())), preferred_element_type=F32)


def _rms(x):
    return x * lax.rsqrt(jnp.mean(x * x, axis=-1, keepdims=True) + EPS)


def _head_norm_rope(xq, gc, gs, bd):
    sq = xq * xq
    hi = sq.astype(BF16)
    lo = (sq - hi.astype(F32)).astype(BF16)
    ss = _dot(hi, bd) + _dot(lo, bd)
    r = lax.rsqrt(ss * (1.0 / HEAD_DIM) + EPS)
    lane = lax.broadcasted_iota(jnp.int32, xq.shape, 1)
    first_half = (lane % 32) < 16
    partner = jnp.where(first_half, pltpu.roll(xq, LANES - 16, 1), pltpu.roll(xq, 16, 1))
    return r * (xq * gc + partner * gs)


def _inproj_kernel(x_ref, gmix_ref, w_ref, gcq_ref, gsq_ref, gck_ref, gsk_ref, bd_ref, gsgu_ref,
                   q_ref, k_ref, v_ref, u_ref, vn_ref, p_ref):
    h = (_rms(x_ref[...]) * gmix_ref[...]).astype(BF16)
    acc = _dot(h, w_ref[...])
    bd = bd_ref[...]
    for c in range(Q_W // LANES):
        sl = slice(c * LANES, (c + 1) * LANES)
        q_ref[:, sl] = _head_norm_rope(acc[:, sl], gcq_ref[...], gsq_ref[...], bd).astype(BF16)
    o = Q_W
    k_ref[...] = _head_norm_rope(acc[:, o:o + KV_W], gck_ref[...], gsk_ref[...], bd).astype(BF16)
    o += KV_W
    v_ref[...] = acc[:, o:o + KV_W].astype(BF16)
    o += KV_W
    u_ref[...] = acc[:, o:o + SGU_W]
    o += SGU_W
    vn_ref[...] = (_rms(acc[:, o:o + SGU_W]) * gsgu_ref[...]).astype(BF16)
    o += SGU_W
    p_ref[...] = acc[:, o:o + POOL_W]


def _inproj(x2, gmix, w_a, gcq, gsq, gck, gsk, bd, gsgu, S, tm):
    T, D = x2.shape
    nseq = S // tm
    row = lambda i: (i, 0)
    const = lambda i: (0, 0)
    pos = lambda i: (i % nseq, 0)
    return pl.pallas_call(
        _inproj_kernel,
        grid=(T // tm,),
        in_specs=[
            pl.BlockSpec((tm, D), row),
            pl.BlockSpec((1, D), const),
            pl.BlockSpec((D, A_W), const),
            pl.BlockSpec((tm, LANES), pos),
            pl.BlockSpec((tm, LANES), pos),
            pl.BlockSpec((tm, LANES), pos),
            pl.BlockSpec((tm, LANES), pos),
            pl.BlockSpec((LANES, LANES), const),
            pl.BlockSpec((1, SGU_W), const),
        ],
        out_specs=[
            pl.BlockSpec((tm, Q_W), row),
            pl.BlockSpec((tm, KV_W), row),
            pl.BlockSpec((tm, KV_W), row),
            pl.BlockSpec((tm, SGU_W), row),
            pl.BlockSpec((tm, SGU_W), row),
            pl.BlockSpec((tm, POOL_W), row),
        ],
        out_shape=[
            jax.ShapeDtypeStruct((T, Q_W), BF16),
            jax.ShapeDtypeStruct((T, KV_W), BF16),
            jax.ShapeDtypeStruct((T, KV_W), BF16),
            jax.ShapeDtypeStruct((T, SGU_W), F32),
            jax.ShapeDtypeStruct((T, SGU_W), BF16),
            jax.ShapeDtypeStruct((T, POOL_W), F32),
        ],
        compiler_params=_cparams("parallel"),
        name="inproj",
    )(x2, gmix, w_a, gcq, gsq, gck, gsk, bd, gsgu)


def _attn_kernel(q_ref, k_ref, v_ref, o_ref, kz_ref, vz_ref):
    @pl.when(pl.program_id(1) == 0)
    def _():
        lane = lax.broadcasted_iota(jnp.int32, k_ref.shape, 1)
        low = lane < HEAD_DIM
        for src, dst in ((k_ref, kz_ref), (v_ref, vz_ref)):
            a = src[...].astype(F32)
            ar = pltpu.roll(a, HEAD_DIM, 1)
            dst[0] = jnp.where(low, a, 0.0).astype(BF16)
            dst[1] = jnp.where(low, 0.0, ar).astype(BF16)
            dst[2] = jnp.where(low, ar, 0.0).astype(BF16)
            dst[3] = jnp.where(low, 0.0, a).astype(BF16)

    for pair in range(N_HEADS // 2):
        sl = slice(pair * LANES, (pair + 1) * LANES)
        kv = (2 * pair) // (N_HEADS // N_KV_HEADS)
        qp = q_ref[:, sl]
        out = None
        for parity in range(2):
            s = _dot_nt(qp, kz_ref[2 * kv + parity])
            e = jnp.exp(s - jnp.max(s, axis=-1, keepdims=True))
            inv = 1.0 / jnp.sum(e, axis=-1, keepdims=True)
            pv = _dot(e.astype(BF16), vz_ref[2 * kv + parity]) * inv
            out = pv if out is None else out + pv
        o_ref[:, sl] = out.astype(BF16)


def _attention(q, k, v, B, S, tq):
    T = q.shape[0]
    nq = S // tq
    return pl.pallas_call(
        _attn_kernel,
        grid=(B, nq),
        in_specs=[
            pl.BlockSpec((tq, Q_W), lambda b, i: (b * nq + i, 0)),
            pl.BlockSpec((S, KV_W), lambda b, i: (b, 0)),
            pl.BlockSpec((S, KV_W), lambda b, i: (b, 0)),
        ],
        out_specs=pl.BlockSpec((tq, Q_W), lambda b, i: (b * nq + i, 0)),
        out_shape=jax.ShapeDtypeStruct((T, Q_W), BF16),
        scratch_shapes=[pltpu.VMEM((4, S, KV_W), BF16), pltpu.VMEM((4, S, KV_W), BF16)],
        compiler_params=_cparams("parallel", "arbitrary"),
        name="attention",
    )(q, k, v)


def _merge_kernel(x_ref, gmix_ref, wgl_ref, attn_ref, u_ref, vn_ref, p_ref, pprev_ref, pnext_ref,
                  wsp_ref, bsp_ref, wpool_ref, pscale_ref, wao_ref, wso_ref, wpo_ref, wout_ref,
                  gffn_ref, wr_ref, x1_ref, h2_ref, lg_ref, *, S):
    ts = x_ref.shape[0]
    x = x_ref[...]
    h = (_rms(x) * gmix_ref[...]).astype(BF16)

    vn = vn_ref[...]
    lane_b = lax.broadcasted_iota(jnp.int32, (SGU_CHUNK, SGU_W), 1) // SGU_HEAD
    zs = []
    for c in range(ts // SGU_CHUNK):
        vc = vn[c * SGU_CHUNK:(c + 1) * SGU_CHUNK]
        stacked = jnp.concatenate(
            [jnp.where(lane_b == g, vc, jnp.zeros_like(vc)) for g in range(SGU_GROUPS)], axis=0)
        zs.append(_dot(wsp_ref[...], stacked) + bsp_ref[...])
    sgu = (u_ref[...] * jnp.concatenate(zs, axis=0)).astype(BF16)
    br_b = _dot(sgu, wso_ref[...])

    pos0 = (pl.program_id(0) % (S // ts)) * ts
    p = p_ref[...]
    prev = jnp.where(pos0 == 0, 0.0, pprev_ref[...])
    nxt = jnp.where(pos0 + ts == S, 0.0, pnext_ref[...])
    ext = jnp.concatenate([prev, p, nxt], axis=0)
    n = ts + 2 * POOL_HALO
    s2 = ext + pltpu.roll(ext, 1, 0)
    s4 = pltpu.roll(s2, 1, 0) + pltpu.roll(s2, n - 1, 0)
    s8 = pltpu.roll(s4, 2, 0) + pltpu.roll(s4, n - 2, 0)
    s16 = pltpu.roll(s8, 4, 0) + pltpu.roll(s8, n - 4, 0)
    mid = slice(POOL_HALO, POOL_HALO + ts)
    grp = lax.broadcasted_iota(jnp.int32, (ts, POOL_W), 1) // POOL_HEAD
    t = pos0 + lax.broadcasted_iota(jnp.int32, (ts, POOL_W), 0)
    half = jnp.left_shift(1, grp)
    cnt = jnp.minimum(t + half - 1, S - 1) - jnp.maximum(t - half, 0) + 1
    wsum = jnp.where(grp == 0, s2[mid], jnp.where(grp == 1, s4[mid],
                                                  jnp.where(grp == 2, s8[mid], s16[mid])))
    d = (wsum / cnt.astype(F32) - p).astype(BF16)
    pooled = (_dot(d, wpool_ref[...]) * pscale_ref[...]).astype(BF16)
    br_c = _dot(pooled, wpo_ref[...])

    br_a = _dot(attn_ref[...], wao_ref[...])

    gl = _dot(h, wgl_ref[...])
    D = x.shape[1]
    gate = lambda j: 0.5 * (1.0 + jnp.tanh(0.5 * gl[:, j * D:(j + 1) * D]))
    merged = (gate(0) * br_a + gate(1) * br_b + gate(2) * br_c).astype(BF16)
    x1 = x + _dot(merged, wout_ref[...])
    x1_ref[...] = x1
    h2 = (_rms(x1) * gffn_ref[...]).astype(BF16)
    h2_ref[...] = h2
    lg_ref[...] = _dot(h2, wr_ref[...])


def _merge(x2, gmix, w_gl, attn, u, vn, p, wsp, bsp, wpool, pscale, wao, wso, wpo, wout, gffn, wr,
           S, ts):
    T, D = x2.shape
    row = lambda i: (i, 0)
    const = lambda i: (0, 0)
    hb = ts // POOL_HALO
    nhb = T // POOL_HALO
    full = lambda a: pl.BlockSpec(a.shape, const)
    return pl.pallas_call(
        functools.partial(_merge_kernel, S=S),
        grid=(T // ts,),
        in_specs=[
            pl.BlockSpec((ts, D), row),
            full(gmix),
            full(w_gl),
            pl.BlockSpec((ts, Q_W), row),
            pl.BlockSpec((ts, SGU_W), row),
            pl.BlockSpec((ts, SGU_W), row),
            pl.BlockSpec((ts, POOL_W), row),
            pl.BlockSpec((POOL_HALO, POOL_W), lambda i: (jnp.maximum(i * hb - 1, 0), 0)),
            pl.BlockSpec((POOL_HALO, POOL_W), lambda i: (jnp.minimum((i + 1) * hb, nhb - 1), 0)),
            full(wsp), full(bsp), full(wpool), full(pscale), full(wao), full(wso), full(wpo),
            full(wout), full(gffn), full(wr),
        ],
        out_specs=[
            pl.BlockSpec((ts, D), row),
            pl.BlockSpec((ts, D), row),
            pl.BlockSpec((ts, LANES), row),
        ],
        out_shape=[
            jax.ShapeDtypeStruct((T, D), F32),
            jax.ShapeDtypeStruct((T, D), BF16),
            jax.ShapeDtypeStruct((T, LANES), F32),
        ],
        compiler_params=_cparams("parallel"),
        name="merge",
    )(x2, gmix, w_gl, attn, u, vn, p, p, p, wsp, bsp, wpool, pscale, wao, wso, wpo, wout, gffn, wr)


def _excl_cumsum(m01, tri_ones):
    off = jnp.zeros((m01.shape[0], LANES), F32)
    pieces = []
    for j in range(m01.shape[1] // LANES):
        res = _dot(m01[:, j * LANES:(j + 1) * LANES].astype(BF16), tri_ones)
        pieces.append(res[:, :LANES] + off)
        off = off + res[:, LANES:]
    return jnp.concatenate(pieces, axis=1)


def _route_kernel(lg_ref, tri_ref, rank_ref, rankt_ref, afft_ref, *, cap):
    nb, S, _ = lg_ref.shape
    E = N_EXPERTS
    rows = []
    for b in range(nb):
        lt = lg_ref[b].T[:E]
        e = jnp.exp(lt - jnp.max(lt, axis=0, keepdims=True))
        rows.append(e / jnp.sum(e, axis=0, keepdims=True))
    aff = jnp.concatenate(rows, axis=0)
    bits = pltpu.bitcast(aff, jnp.int32)

    def refine(i, thr):
        cand = thr | jnp.left_shift(jnp.int32(1), 30 - i)
        cnt = jnp.sum(jnp.where(bits >= cand, 1.0, 0.0), axis=1, keepdims=True)
        return jnp.where(cnt >= cap, cand, thr)

    thr = lax.fori_loop(0, 31, refine, jnp.zeros((nb * E, 1), jnp.int32))
    gt = jnp.where(bits > thr, 1.0, 0.0)
    eq = jnp.where(bits == thr, 1.0, 0.0)
    need = cap - jnp.sum(gt, axis=1, keepdims=True)
    tri = tri_ref[...]
    sel = gt + eq * jnp.where(_excl_cumsum(eq, tri) < need, 1.0, 0.0)
    rank = jnp.where(sel > 0.5, _excl_cumsum(sel, tri), -1.0)
    pad = jnp.zeros((LANES - E, S), F32)
    for b in range(nb):
        rb = rank[b * E:(b + 1) * E]
        rank_ref[b] = rb
        rankt_ref[b] = jnp.concatenate([rb, pad], axis=0).T
        afft_ref[b] = jnp.concatenate([aff[b * E:(b + 1) * E], pad], axis=0).T


def _route(lg3, tri, cap, nb):
    B, S, _ = lg3.shape
    blk = lambda b: (b, 0, 0)
    return pl.pallas_call(
        functools.partial(_route_kernel, cap=cap),
        grid=(B // nb,),
        in_specs=[pl.BlockSpec((nb, S, LANES), blk), pl.BlockSpec(tri.shape, lambda b: (0, 0))],
        out_specs=[
            pl.BlockSpec((nb, N_EXPERTS, S), blk),
            pl.BlockSpec((nb, S, LANES), blk),
            pl.BlockSpec((nb, S, LANES), blk),
        ],
        out_shape=[
            jax.ShapeDtypeStruct((B, N_EXPERTS, S), F32),
            jax.ShapeDtypeStruct((B, S, LANES), F32),
            jax.ShapeDtypeStruct((B, S, LANES), F32),
        ],
        compiler_params=_cparams("parallel"),
        name="route",
    )(lg3, tri)


def _moe_kernel(h2_ref, rank_ref, rankt_ref, afft_ref, wg_ref, wu_ref, wd_ref, x1c_ref, out_ref,
                *, cap, tchunk):
    S, D = h2_ref.shape
    e = pl.program_id(1)

    @pl.when(e == 0)
    def _():
        out_ref[...] = jnp.zeros_like(out_ref)

    slot = lax.broadcasted_iota(jnp.int32, (cap, tchunk), 0).astype(F32)
    xs = jnp.zeros((cap, D), F32)
    for c in range(S // tchunk):
        sl = slice(c * tchunk, (c + 1) * tchunk)
        onehot = jnp.where(rank_ref[:, sl] == slot, 1.0, 0.0).astype(BF16)
        xs = xs + _dot(onehot, h2_ref[sl, :])
    xs = xs.astype(BF16)
    a = _dot(xs, wg_ref[...])
    b = _dot(xs, wu_ref[...])
    hidden = (a * jax.nn.sigmoid(a) * b).astype(BF16)
    y = _dot(hidden, wd_ref[...]).astype(BF16)

    pick = jnp.where(lax.broadcasted_iota(jnp.int32, (LANES, LANES), 0) == e, 1.0, 0.0).astype(BF16)
    lane = lax.broadcasted_iota(jnp.int32, (tchunk, LANES), 1).astype(F32)
    for c in range(S // tchunk):
        sl = slice(c * tchunk, (c + 1) * tchunk)
        r = _dot(rankt_ref[sl, :].astype(BF16), pick)
        w = _dot(afft_ref[sl, :].astype(BF16), pick)
        scat = jnp.concatenate(
            [jnp.where(r == lane + float(j * LANES), w, 0.0) for j in range(cap // LANES)], axis=1)
        out_ref[sl, :] += _dot(scat.astype(BF16), y)

    rows = x1c_ref.shape[0]
    start = pl.multiple_of(e * rows, rows)
    out_ref[pl.ds(start, rows), :] += x1c_ref[...]


def _moe(h2, rank, rankt, afft, wg, wu, wd, x1, cap, tchunk):
    B, S, D = h2.shape
    E, _, F = wg.shape
    rows = S // E
    return pl.pallas_call(
        functools.partial(_moe_kernel, cap=cap, tchunk=tchunk),
        grid=(B, E),
        in_specs=[
            pl.BlockSpec((None, S, D), lambda b, e: (b, 0, 0)),
            pl.BlockSpec((None, None, 1, S), lambda b, e: (b, e, 0, 0)),
            pl.BlockSpec((None, S, LANES), lambda b, e: (b, 0, 0)),
            pl.BlockSpec((None, S, LANES), lambda b, e: (b, 0, 0)),
            pl.BlockSpec((None, D, F), lambda b, e: (e, 0, 0)),
            pl.BlockSpec((None, D, F), lambda b, e: (e, 0, 0)),
            pl.BlockSpec((None, F, D), lambda b, e: (e, 0, 0)),
            pl.BlockSpec((rows, D), lambda b, e: (b * E + e, 0)),
        ],
        out_specs=pl.BlockSpec((None, S, D), lambda b, e: (b, 0, 0)),
        out_shape=jax.ShapeDtypeStruct((B, S, D), F32),
        compiler_params=_cparams("parallel", "arbitrary"),
        name="moe",
    )(h2, rank.reshape(B, E, 1, S), rankt, afft, wg, wu, wd, x1)


def _final_kernel(x_ref, g_ref, o_ref):
    o_ref[...] = _rms(x_ref[...]) * g_ref[...]


def _final_norm(x2, g, tm):
    T, D = x2.shape
    return pl.pallas_call(
        _final_kernel,
        grid=(T // tm,),
        in_specs=[pl.BlockSpec((tm, D), lambda i: (i, 0)), pl.BlockSpec((1, D), lambda i: (0, 0))],
        out_specs=pl.BlockSpec((tm, D), lambda i: (i, 0)),
        out_shape=jax.ShapeDtypeStruct((T, D), F32),
        compiler_params=_cparams("parallel"),
        name="final_norm",
    )(x2, g)


def _rope_tables(S, g, scale):
    rows = S // GRID_W
    row = jnp.broadcast_to(jnp.arange(rows, dtype=F32)[:, None], (rows, GRID_W)).reshape(-1)
    col = jnp.broadcast_to(jnp.arange(GRID_W, dtype=F32)[None, :], (rows, GRID_W)).reshape(-1)
    part = HEAD_DIM // 2
    freqs = ROPE_THETA ** (-jnp.arange(0, part, 2, dtype=F32) / part)
    ang = jnp.stack([row[:, None] * freqs, col[:, None] * freqs], axis=1)
    cos, sin = jnp.cos(ang), jnp.sin(ang)
    c64 = jnp.concatenate([cos[:, 0], cos[:, 0], cos[:, 1], cos[:, 1]], axis=-1)
    s64 = jnp.concatenate([-sin[:, 0], sin[:, 0], -sin[:, 1], sin[:, 1]], axis=-1)
    g_swapped = g.reshape(2, 2, HEAD_DIM // 4)[:, ::-1, :].reshape(HEAD_DIM)
    gc = jnp.tile(c64 * g[None, :], (1, 2)) * scale
    gs = jnp.tile(s64 * g_swapped[None, :], (1, 2)) * scale
    return gc, gs


def _block_diag(blocks):
    n = blocks.shape[0]
    m = blocks.shape[1]
    eye = jnp.eye(n, dtype=blocks.dtype)
    return (eye[:, None, :, None] * blocks[:, :, None, :]).reshape(n * m, n * blocks.shape[2])


def kernel(x, g_mix, w_in, g_q, g_k, g_sgu, w_spatial, b_spatial, w_pool, pool_scale, w_attn_o,
           w_sgu_o, w_pool_o, w_out, g_ffn, w_router, w_gate_e, w_up_e, w_down_e, g_final):
    B, S, D = x.shape
    depth = w_in.shape[0]
    cap = CAPACITY_FACTOR * S // N_EXPERTS
    x2 = x.reshape(B * S, D)

    bd = _block_diag(jnp.ones((2, HEAD_DIM, HEAD_DIM), F32)).astype(BF16)
    k_i = lax.broadcasted_iota(jnp.int32, (LANES, LANES), 0)
    l_i = lax.broadcasted_iota(jnp.int32, (LANES, LANES), 1)
    tri = jnp.concatenate([(k_i < l_i).astype(BF16), jnp.ones((LANES, LANES), BF16)], axis=1)

    for l in range(depth):
        gcq, gsq = _rope_tables(S, g_q[l], HEAD_DIM ** -0.5)
        gck, gsk = _rope_tables(S, g_k[l], 1.0)
        w_a = w_in[l, :, :A_W].astype(BF16)
        w_gl = w_in[l, :, A_W:].astype(BF16)
        gmix = g_mix[l][None, :]
        q, k, v, u, vn, p = _inproj(x2, gmix, w_a, gcq, gsq, gck, gsk, bd, g_sgu[l][None, :], S, 512)
        attn = _attention(q, k, v, B, S, 256)

        wsp = w_spatial[l].transpose(1, 0, 2).reshape(SGU_CHUNK, SGU_GROUPS * SGU_CHUNK).astype(BF16)
        bsp = jnp.repeat(b_spatial[l].T, SGU_HEAD, axis=1)
        wpool = _block_diag(w_pool[l]).astype(BF16)
        wr = jnp.pad(w_router[l], ((0, 0), (0, LANES - N_EXPERTS))).astype(BF16)
        x1, h2, lg = _merge(
            x2, gmix, w_gl, attn, u, vn, p, wsp, bsp, wpool, pool_scale[l][None, :],
            w_attn_o[l].astype(BF16), w_sgu_o[l].astype(BF16), w_pool_o[l].astype(BF16),
            w_out[l].astype(BF16), g_ffn[l][None, :], wr, S, 256)

        rank, rankt, afft = _route(lg.reshape(B, S, LANES), tri, cap, math.gcd(B, 4))
        x2 = _moe(h2.reshape(B, S, D), rank, rankt, afft, w_gate_e[l].astype(BF16),
                  w_up_e[l].astype(BF16), w_down_e[l].astype(BF16), x1, cap, 512).reshape(B * S, D)

    return _final_norm(x2, g_final[None, :], 512).reshape(B, S, D)
```

```python
import functools
import math

import jax
import jax.numpy as jnp
from jax import lax
from jax.experimental import pallas as pl
from jax.experimental.pallas import tpu as pltpu

F32 = jnp.float32
BF16 = jnp.bfloat16

GRID_W = 64
N_HEADS = 8
N_KV_HEADS = 2
HEAD_DIM = 64
ROPE_THETA = 10000.0
SGU_GROUPS = 4
SGU_HEAD = 64
SGU_W = SGU_GROUPS * SGU_HEAD
SGU_CHUNK = 128
POOL_WINDOWS = (2, 4, 8, 16)
POOL_HEAD = 64
POOL_W = len(POOL_WINDOWS) * POOL_HEAD
POOL_HALO = 8
KEY_CHUNK = 256
Q_W = N_HEADS * HEAD_DIM
KV_W = N_KV_HEADS * HEAD_DIM
N_EXPERTS = 16
CAPACITY_FACTOR = 2
EPS = 1e-6

LANES = 128
A_W = Q_W + 2 * KV_W + 2 * SGU_W + POOL_W
VMEM_LIMIT = 56 * 1024 * 1024


def _cparams(*sem):
    return pltpu.CompilerParams(dimension_semantics=sem, vmem_limit_bytes=VMEM_LIMIT)


def _dot(a, b):
    return jnp.dot(a, b, preferred_element_type=F32)


def _dot_nt(a, b):
    return lax.dot_general(a, b, (((1,), (1,)), ((), ())), preferred_element_type=F32)


def _rms(x):
    return x * lax.rsqrt(jnp.mean(x * x, axis=-1, keepdims=True) + EPS)


def _head_norm_rope(xq, gc, gs, bd):
    sq = xq * xq
    hi = sq.astype(BF16)
    lo = (sq - hi.astype(F32)).astype(BF16)
    ss = _dot(hi, bd) + _dot(lo, bd)
    r = lax.rsqrt(ss * (1.0 / HEAD_DIM) + EPS)
    lane = lax.broadcasted_iota(jnp.int32, xq.shape, 1)
    first_half = (lane % 32) < 16
    partner = jnp.where(first_half, pltpu.roll(xq, LANES - 16, 1), pltpu.roll(xq, 16, 1))
    return r * (xq * gc + partner * gs)


def _inproj_kernel(x_ref, gmix_ref, w_ref, gcq_ref, gsq_ref, gck_ref, gsk_ref, bd_ref, gsgu_ref,
                   q_ref, k_ref, v_ref, u_ref, vn_ref, p_ref):
    h = (_rms(x_ref[...]) * gmix_ref[...]).astype(BF16)
    acc = _dot(h, w_ref[...])
    bd = bd_ref[...]
    for c in range(Q_W // LANES):
        sl = slice(c * LANES, (c + 1) * LANES)
        q_ref[:, sl] = _head_norm_rope(acc[:, sl], gcq_ref[...], gsq_ref[...], bd).astype(BF16)
    o = Q_W
    k_ref[...] = _head_norm_rope(acc[:, o:o + KV_W], gck_ref[...], gsk_ref[...], bd).astype(BF16)
    o += KV_W
    v_ref[...] = acc[:, o:o + KV_W].astype(BF16)
    o += KV_W
    u_ref[...] = acc[:, o:o + SGU_W]
    o += SGU_W
    vn_ref[...] = (_rms(acc[:, o:o + SGU_W]) * gsgu_ref[...]).astype(BF16)
    o += SGU_W
    p_ref[...] = acc[:, o:o + POOL_W]


def _inproj(x2, gmix, w_a, gcq, gsq, gck, gsk, bd, gsgu, S, tm):
    T, D = x2.shape
    nseq = S // tm
    row = lambda i: (i, 0)
    const = lambda i: (0, 0)
    pos = lambda i: (i % nseq, 0)
    return pl.pallas_call(
        _inproj_kernel,
        grid=(T // tm,),
        in_specs=[
            pl.BlockSpec((tm, D), row),
            pl.BlockSpec((1, D), const),
            pl.BlockSpec((D, A_W), const),
            pl.BlockSpec((tm, LANES), pos),
            pl.BlockSpec((tm, LANES), pos),
            pl.BlockSpec((tm, LANES), pos),
            pl.BlockSpec((tm, LANES), pos),
            pl.BlockSpec((LANES, LANES), const),
            pl.BlockSpec((1, SGU_W), const),
        ],
        out_specs=[
            pl.BlockSpec((tm, Q_W), row),
            pl.BlockSpec((tm, KV_W), row),
            pl.BlockSpec((tm, KV_W), row),
            pl.BlockSpec((tm, SGU_W), row),
            pl.BlockSpec((tm, SGU_W), row),
            pl.BlockSpec((tm, POOL_W), row),
        ],
        out_shape=[
            jax.ShapeDtypeStruct((T, Q_W), BF16),
            jax.ShapeDtypeStruct((T, KV_W), BF16),
            jax.ShapeDtypeStruct((T, KV_W), BF16),
            jax.ShapeDtypeStruct((T, SGU_W), F32),
            jax.ShapeDtypeStruct((T, SGU_W), BF16),
            jax.ShapeDtypeStruct((T, POOL_W), F32),
        ],
        compiler_params=_cparams("parallel"),
        name="inproj",
    )(x2, gmix, w_a, gcq, gsq, gck, gsk, bd, gsgu)


def _attn_kernel(q_ref, k_ref, v_ref, o_ref, kz_ref, vzt_ref, s_ref):
    @pl.when(pl.program_id(1) == 0)
    def _():
        lane = lax.broadcasted_iota(jnp.int32, k_ref.shape, 1)
        low = lane < HEAD_DIM
        k = k_ref[...].astype(F32)
        kr = pltpu.roll(k, HEAD_DIM, 1)
        kz_ref[0] = jnp.where(low, k, 0.0).astype(BF16)
        kz_ref[1] = jnp.where(low, 0.0, kr).astype(BF16)
        kz_ref[2] = jnp.where(low, kr, 0.0).astype(BF16)
        kz_ref[3] = jnp.where(low, 0.0, k).astype(BF16)
        vt = v_ref[...].astype(F32).T
        vtr = pltpu.roll(vt, HEAD_DIM, 0)
        row = lax.broadcasted_iota(jnp.int32, vt.shape, 0)
        top = row < HEAD_DIM
        ones_lo = jnp.where(row == 0, 1.0, 0.0)
        ones_hi = jnp.where(row == HEAD_DIM, 1.0, 0.0)
        vzt_ref[0] = jnp.where(top, vt, ones_hi).astype(BF16)
        vzt_ref[1] = jnp.where(top, ones_lo, vtr).astype(BF16)
        vzt_ref[2] = jnp.where(top, vtr, ones_hi).astype(BF16)
        vzt_ref[3] = jnp.where(top, ones_lo, vt).astype(BF16)

    tq = q_ref.shape[0]
    S = k_ref.shape[0]
    nchunk = S // KEY_CHUNK
    sub = 8
    slot0 = jnp.minimum(pl.program_id(1), 0)

    def scores(head, c, mrun):
        pair, parity = divmod(head, 2)
        kvi = 2 * ((2 * pair) // (N_HEADS // N_KV_HEADS)) + parity
        rows = slice(c * KEY_CHUNK, (c + 1) * KEY_CHUNK)
        s = _dot_nt(kz_ref[kvi, rows, :], q_ref[:, pair * LANES:(pair + 1) * LANES])
        s_ref[slot0 + head % 2, rows, :] = s
        m = jnp.max(s.reshape(KEY_CHUNK // sub, sub, tq), axis=0)
        return m if mrun is None else jnp.maximum(mrun, m)

    def weighted(head, c, m8, acc):
        pair, parity = divmod(head, 2)
        kvi = 2 * ((2 * pair) // (N_HEADS // N_KV_HEADS)) + parity
        rows = slice(c * KEY_CHUNK, (c + 1) * KEY_CHUNK)
        s3 = s_ref[slot0 + head % 2, rows, :].reshape(KEY_CHUNK // sub, sub, tq)
        e = jnp.exp2(s3 - m8[None]).reshape(KEY_CHUNK, tq).astype(BF16)
        d = _dot(vzt_ref[kvi, :, rows], e)
        return d if acc is None else acc + d

    top = lax.broadcasted_iota(jnp.int32, (LANES, tq), 0) < HEAD_DIM
    mrun = None
    for c in range(nchunk):
        mrun = scores(0, c, mrun)
    even = None
    for head in range(N_HEADS):
        m8 = jnp.broadcast_to(jnp.max(mrun, axis=0, keepdims=True), (sub, tq))
        mnext, acc = None, None
        for c in range(nchunk):
            if head + 1 < N_HEADS:
                mnext = scores(head + 1, c, mnext)
            acc = weighted(head, c, m8, acc)
        mrun = mnext
        pair, parity = divmod(head, 2)
        if parity == 0:
            even = acc * (1.0 / acc[HEAD_DIM:HEAD_DIM + 1])
        else:
            odd = acc * (1.0 / acc[0:1])
            o_ref[:, pair * LANES:(pair + 1) * LANES] = jnp.where(top, even, odd).T.astype(BF16)


def _attention(q, k, v, B, S, tq):
    T = q.shape[0]
    nq = S // tq
    return pl.pallas_call(
        _attn_kernel,
        grid=(B, nq),
        in_specs=[
            pl.BlockSpec((tq, Q_W), lambda b, i: (b * nq + i, 0)),
            pl.BlockSpec((S, KV_W), lambda b, i: (b, 0)),
            pl.BlockSpec((S, KV_W), lambda b, i: (b, 0)),
        ],
        out_specs=pl.BlockSpec((tq, Q_W), lambda b, i: (b * nq + i, 0)),
        out_shape=jax.ShapeDtypeStruct((T, Q_W), BF16),
        scratch_shapes=[pltpu.VMEM((4, S, KV_W), BF16), pltpu.VMEM((4, KV_W, S), BF16),
                        pltpu.VMEM((2, S, tq), F32)],
        compiler_params=_cparams("parallel", "arbitrary"),
        name="attention",
    )(q, k, v)


def _merge_kernel(x_ref, gmix_ref, wgl_ref, attn_ref, u_ref, vn_ref, p_ref, pprev_ref, pnext_ref,
                  wsp_ref, bsp_ref, wpool_ref, pscale_ref, wao_ref, wso_ref, wpo_ref, wout_ref,
                  gffn_ref, wr_ref, x1_ref, h2_ref, lg_ref, *, S):
    ts = x_ref.shape[0]
    x = x_ref[...]
    h = (_rms(x) * gmix_ref[...]).astype(BF16)

    vn = vn_ref[...]
    lane_b = lax.broadcasted_iota(jnp.int32, (SGU_CHUNK, SGU_W), 1) // SGU_HEAD
    zs = []
    for c in range(ts // SGU_CHUNK):
        vc = vn[c * SGU_CHUNK:(c + 1) * SGU_CHUNK]
        stacked = jnp.concatenate(
            [jnp.where(lane_b == g, vc, jnp.zeros_like(vc)) for g in range(SGU_GROUPS)], axis=0)
        zs.append(_dot(wsp_ref[...], stacked) + bsp_ref[...])
    sgu = (u_ref[...] * jnp.concatenate(zs, axis=0)).astype(BF16)
    br_b = _dot(sgu, wso_ref[...])

    pos0 = (pl.program_id(0) % (S // ts)) * ts
    p = p_ref[...]
    prev = jnp.where(pos0 == 0, 0.0, pprev_ref[...])
    nxt = jnp.where(pos0 + ts == S, 0.0, pnext_ref[...])
    ext = jnp.concatenate([prev, p, nxt], axis=0)
    n = ts + 2 * POOL_HALO
    s2 = ext + pltpu.roll(ext, 1, 0)
    s4 = pltpu.roll(s2, 1, 0) + pltpu.roll(s2, n - 1, 0)
    s8 = pltpu.roll(s4, 2, 0) + pltpu.roll(s4, n - 2, 0)
    s16 = pltpu.roll(s8, 4, 0) + pltpu.roll(s8, n - 4, 0)
    mid = slice(POOL_HALO, POOL_HALO + ts)
    grp = lax.broadcasted_iota(jnp.int32, (ts, POOL_W), 1) // POOL_HEAD
    t = pos0 + lax.broadcasted_iota(jnp.int32, (ts, POOL_W), 0)
    half = jnp.left_shift(1, grp)
    cnt = jnp.minimum(t + half - 1, S - 1) - jnp.maximum(t - half, 0) + 1
    wsum = jnp.where(grp == 0, s2[mid], jnp.where(grp == 1, s4[mid],
                                                  jnp.where(grp == 2, s8[mid], s16[mid])))
    d = (wsum / cnt.astype(F32) - p).astype(BF16)
    pooled = (_dot(d, wpool_ref[...]) * pscale_ref[...]).astype(BF16)
    br_c = _dot(pooled, wpo_ref[...])

    br_a = _dot(attn_ref[...], wao_ref[...])

    gl = _dot(h, wgl_ref[...])
    D = x.shape[1]
    gate = lambda j: 0.5 * (1.0 + jnp.tanh(0.5 * gl[:, j * D:(j + 1) * D]))
    merged = (gate(0) * br_a + gate(1) * br_b + gate(2) * br_c).astype(BF16)
    x1 = x + _dot(merged, wout_ref[...])
    x1_ref[...] = x1
    h2 = (_rms(x1) * gffn_ref[...]).astype(BF16)
    h2_ref[...] = h2
    lg_ref[...] = _dot(h2, wr_ref[...])


def _merge(x2, gmix, w_gl, attn, u, vn, p, wsp, bsp, wpool, pscale, wao, wso, wpo, wout, gffn, wr,
           S, ts):
    T, D = x2.shape
    row = lambda i: (i, 0)
    const = lambda i: (0, 0)
    hb = ts // POOL_HALO
    nhb = T // POOL_HALO
    full = lambda a: pl.BlockSpec(a.shape, const)
    return pl.pallas_call(
        functools.partial(_merge_kernel, S=S),
        grid=(T // ts,),
        in_specs=[
            pl.BlockSpec((ts, D), row),
            full(gmix),
            full(w_gl),
            pl.BlockSpec((ts, Q_W), row),
            pl.BlockSpec((ts, SGU_W), row),
            pl.BlockSpec((ts, SGU_W), row),
            pl.BlockSpec((ts, POOL_W), row),
            pl.BlockSpec((POOL_HALO, POOL_W), lambda i: (jnp.maximum(i * hb - 1, 0), 0)),
            pl.BlockSpec((POOL_HALO, POOL_W), lambda i: (jnp.minimum((i + 1) * hb, nhb - 1), 0)),
            full(wsp), full(bsp), full(wpool), full(pscale), full(wao), full(wso), full(wpo),
            full(wout), full(gffn), full(wr),
        ],
        out_specs=[
            pl.BlockSpec((ts, D), row),
            pl.BlockSpec((ts, D), row),
            pl.BlockSpec((ts, LANES), row),
        ],
        out_shape=[
            jax.ShapeDtypeStruct((T, D), F32),
            jax.ShapeDtypeStruct((T, D), BF16),
            jax.ShapeDtypeStruct((T, LANES), F32),
        ],
        compiler_params=_cparams("parallel"),
        name="merge",
    )(x2, gmix, w_gl, attn, u, vn, p, p, p, wsp, bsp, wpool, pscale, wao, wso, wpo, wout, gffn, wr)


def _excl_cumsum(m01, tri_ones):
    off = jnp.zeros((m01.shape[0], LANES), F32)
    pieces = []
    for j in range(m01.shape[1] // LANES):
        res = _dot(m01[:, j * LANES:(j + 1) * LANES].astype(BF16), tri_ones)
        pieces.append(res[:, :LANES] + off)
        off = off + res[:, LANES:]
    return jnp.concatenate(pieces, axis=1)


def _route_kernel(lg_ref, tri_ref, rank_ref, rankt_ref, afft_ref, *, cap):
    nb, S, _ = lg_ref.shape
    E = N_EXPERTS
    rows = []
    for b in range(nb):
        lt = lg_ref[b].T[:E]
        e = jnp.exp(lt - jnp.max(lt, axis=0, keepdims=True))
        rows.append(e / jnp.sum(e, axis=0, keepdims=True))
    aff = jnp.concatenate(rows, axis=0)
    bits = pltpu.bitcast(aff, jnp.int32)

    def refine(i, thr):
        cand = thr | jnp.left_shift(jnp.int32(1), 30 - i)
        cnt = jnp.sum(jnp.where(bits >= cand, 1.0, 0.0), axis=1, keepdims=True)
        return jnp.where(cnt >= cap, cand, thr)

    thr = lax.fori_loop(0, 31, refine, jnp.zeros((nb * E, 1), jnp.int32))
    gt = jnp.where(bits > thr, 1.0, 0.0)
    eq = jnp.where(bits == thr, 1.0, 0.0)
    need = cap - jnp.sum(gt, axis=1, keepdims=True)
    tri = tri_ref[...]
    sel = gt + eq * jnp.where(_excl_cumsum(eq, tri) < need, 1.0, 0.0)
    rank = jnp.where(sel > 0.5, _excl_cumsum(sel, tri), -1.0)
    pad = jnp.zeros((LANES - E, S), F32)
    for b in range(nb):
        rb = rank[b * E:(b + 1) * E]
        rank_ref[b] = rb
        rankt_ref[b] = jnp.concatenate([rb, pad], axis=0).T
        afft_ref[b] = jnp.concatenate([aff[b * E:(b + 1) * E], pad], axis=0).T


def _route(lg3, tri, cap, nb):
    B, S, _ = lg3.shape
    blk = lambda b: (b, 0, 0)
    return pl.pallas_call(
        functools.partial(_route_kernel, cap=cap),
        grid=(B // nb,),
        in_specs=[pl.BlockSpec((nb, S, LANES), blk), pl.BlockSpec(tri.shape, lambda b: (0, 0))],
        out_specs=[
            pl.BlockSpec((nb, N_EXPERTS, S), blk),
            pl.BlockSpec((nb, S, LANES), blk),
            pl.BlockSpec((nb, S, LANES), blk),
        ],
        out_shape=[
            jax.ShapeDtypeStruct((B, N_EXPERTS, S), F32),
            jax.ShapeDtypeStruct((B, S, LANES), F32),
            jax.ShapeDtypeStruct((B, S, LANES), F32),
        ],
        compiler_params=_cparams("parallel"),
        name="route",
    )(lg3, tri)


def _moe_kernel(h2_ref, rank_ref, rankt_ref, afft_ref, wg_ref, wu_ref, wd_ref, x1c_ref, gout_ref,
                out_ref, *, cap, tchunk, norm_out):
    S, D = h2_ref.shape
    e = pl.program_id(1)

    @pl.when(e == 0)
    def _():
        out_ref[...] = jnp.zeros_like(out_ref)

    slot = lax.broadcasted_iota(jnp.int32, (cap, tchunk), 0).astype(F32)
    xs = jnp.zeros((cap, D), F32)
    for c in range(S // tchunk):
        sl = slice(c * tchunk, (c + 1) * tchunk)
        onehot = jnp.where(rank_ref[:, sl] == slot, 1.0, 0.0).astype(BF16)
        xs = xs + _dot(onehot, h2_ref[sl, :])
    xs = xs.astype(BF16)
    a = _dot(xs, wg_ref[...])
    b = _dot(xs, wu_ref[...])
    hidden = (a * jax.nn.sigmoid(a) * b).astype(BF16)
    y = _dot(hidden, wd_ref[...]).astype(BF16)

    pick = jnp.where(lax.broadcasted_iota(jnp.int32, (LANES, LANES), 0) == e, 1.0, 0.0).astype(BF16)
    lane = lax.broadcasted_iota(jnp.int32, (tchunk, LANES), 1).astype(F32)
    for c in range(S // tchunk):
        sl = slice(c * tchunk, (c + 1) * tchunk)
        r = _dot(rankt_ref[sl, :].astype(BF16), pick)
        w = _dot(afft_ref[sl, :].astype(BF16), pick)
        scat = jnp.concatenate(
            [jnp.where(r == lane + float(j * LANES), w, 0.0) for j in range(cap // LANES)], axis=1)
        out_ref[sl, :] += _dot(scat.astype(BF16), y)

    rows = x1c_ref.shape[0]
    start = pl.multiple_of(e * rows, rows)
    out_ref[pl.ds(start, rows), :] += x1c_ref[...]

    if norm_out:
        @pl.when(e == pl.num_programs(1) - 1)
        def _():
            for c in range(S // tchunk):
                sl = slice(c * tchunk, (c + 1) * tchunk)
                out_ref[sl, :] = _rms(out_ref[sl, :]) * gout_ref[...]


def _moe(h2, rank, rankt, afft, wg, wu, wd, x1, gout, cap, tchunk, norm_out):
    B, S, D = h2.shape
    E, _, F = wg.shape
    rows = S // E
    return pl.pallas_call(
        functools.partial(_moe_kernel, cap=cap, tchunk=tchunk, norm_out=norm_out),
        grid=(B, E),
        in_specs=[
            pl.BlockSpec((None, S, D), lambda b, e: (b, 0, 0)),
            pl.BlockSpec((None, None, 1, S), lambda b, e: (b, e, 0, 0)),
            pl.BlockSpec((None, S, LANES), lambda b, e: (b, 0, 0)),
            pl.BlockSpec((None, S, LANES), lambda b, e: (b, 0, 0)),
            pl.BlockSpec((None, D, F), lambda b, e: (e, 0, 0)),
            pl.BlockSpec((None, D, F), lambda b, e: (e, 0, 0)),
            pl.BlockSpec((None, F, D), lambda b, e: (e, 0, 0)),
            pl.BlockSpec((rows, D), lambda b, e: (b * E + e, 0)),
            pl.BlockSpec((1, D), lambda b, e: (0, 0)),
        ],
        out_specs=pl.BlockSpec((None, S, D), lambda b, e: (b, 0, 0)),
        out_shape=jax.ShapeDtypeStruct((B, S, D), F32),
        compiler_params=_cparams("parallel", "arbitrary"),
        name="moe",
    )(h2, rank.reshape(B, E, 1, S), rankt, afft, wg, wu, wd, x1, gout)


def _rope_tables(S, g, scale):
    rows = S // GRID_W
    row = jnp.broadcast_to(jnp.arange(rows, dtype=F32)[:, None], (rows, GRID_W)).reshape(-1)
    col = jnp.broadcast_to(jnp.arange(GRID_W, dtype=F32)[None, :], (rows, GRID_W)).reshape(-1)
    part = HEAD_DIM // 2
    freqs = ROPE_THETA ** (-jnp.arange(0, part, 2, dtype=F32) / part)
    ang = jnp.stack([row[:, None] * freqs, col[:, None] * freqs], axis=1)
    cos, sin = jnp.cos(ang), jnp.sin(ang)
    c64 = jnp.concatenate([cos[:, 0], cos[:, 0], cos[:, 1], cos[:, 1]], axis=-1)
    s64 = jnp.concatenate([-sin[:, 0], sin[:, 0], -sin[:, 1], sin[:, 1]], axis=-1)
    g_swapped = g.reshape(2, 2, HEAD_DIM // 4)[:, ::-1, :].reshape(HEAD_DIM)
    gc = jnp.tile(c64 * g[None, :], (1, 2)) * scale
    gs = jnp.tile(s64 * g_swapped[None, :], (1, 2)) * scale
    return gc, gs


def _block_diag(blocks):
    n = blocks.shape[0]
    m = blocks.shape[1]
    eye = jnp.eye(n, dtype=blocks.dtype)
    return (eye[:, None, :, None] * blocks[:, :, None, :]).reshape(n * m, n * blocks.shape[2])


def kernel(x, g_mix, w_in, g_q, g_k, g_sgu, w_spatial, b_spatial, w_pool, pool_scale, w_attn_o,
           w_sgu_o, w_pool_o, w_out, g_ffn, w_router, w_gate_e, w_up_e, w_down_e, g_final):
    B, S, D = x.shape
    depth = w_in.shape[0]
    cap = CAPACITY_FACTOR * S // N_EXPERTS
    x2 = x.reshape(B * S, D)

    bd = _block_diag(jnp.ones((2, HEAD_DIM, HEAD_DIM), F32)).astype(BF16)
    k_i = lax.broadcasted_iota(jnp.int32, (LANES, LANES), 0)
    l_i = lax.broadcasted_iota(jnp.int32, (LANES, LANES), 1)
    tri = jnp.concatenate([(k_i < l_i).astype(BF16), jnp.ones((LANES, LANES), BF16)], axis=1)

    for l in range(depth):
        gcq, gsq = _rope_tables(S, g_q[l], HEAD_DIM ** -0.5 * math.log2(math.e))
        gck, gsk = _rope_tables(S, g_k[l], 1.0)
        w_a = w_in[l, :, :A_W].astype(BF16)
        w_gl = w_in[l, :, A_W:].astype(BF16)
        gmix = g_mix[l][None, :]
        q, k, v, u, vn, p = _inproj(x2, gmix, w_a, gcq, gsq, gck, gsk, bd, g_sgu[l][None, :], S, 512)
        attn = _attention(q, k, v, B, S, 512)

        wsp = w_spatial[l].transpose(1, 0, 2).reshape(SGU_CHUNK, SGU_GROUPS * SGU_CHUNK).astype(BF16)
        bsp = jnp.repeat(b_spatial[l].T, SGU_HEAD, axis=1)
        wpool = _block_diag(w_pool[l]).astype(BF16)
        wr = jnp.pad(w_router[l], ((0, 0), (0, LANES - N_EXPERTS))).astype(BF16)
        x1, h2, lg = _merge(
            x2, gmix, w_gl, attn, u, vn, p, wsp, bsp, wpool, pool_scale[l][None, :],
            w_attn_o[l].astype(BF16), w_sgu_o[l].astype(BF16), w_pool_o[l].astype(BF16),
            w_out[l].astype(BF16), g_ffn[l][None, :], wr, S, 512)

        rank, rankt, afft = _route(lg.reshape(B, S, LANES), tri, cap, math.gcd(B, 4))
        x2 = _moe(h2.reshape(B, S, D), rank, rankt, afft, w_gate_e[l].astype(BF16),
                  w_up_e[l].astype(BF16), w_down_e[l].astype(BF16), x1, g_final[None, :], cap, 512,
                  norm_out=(l == depth - 1)).reshape(B * S, D)

    return x2.reshape(B, S, D)
```

```python
import functools
import math

import jax
import jax.numpy as jnp
from jax import lax
from jax.experimental import pallas as pl
from jax.experimental.pallas import tpu as pltpu

F32 = jnp.float32
BF16 = jnp.bfloat16

GRID_W = 64
N_HEADS = 8
N_KV_HEADS = 2
HEAD_DIM = 64
ROPE_THETA = 10000.0
SGU_GROUPS = 4
SGU_HEAD = 64
SGU_W = SGU_GROUPS * SGU_HEAD
SGU_CHUNK = 128
POOL_WINDOWS = (2, 4, 8, 16)
POOL_HEAD = 64
POOL_W = len(POOL_WINDOWS) * POOL_HEAD
POOL_HALO = 8
KEY_CHUNK = 256
TOKEN_BLOCK = 256
TILE_SHIFT = 4
SLOT_TILE = 1 << TILE_SHIFT
CHUNK_SHIFT = 4
CHUNK_TILES = 1 << CHUNK_SHIFT
NO_SLOT = -1024
Q_W = N_HEADS * HEAD_DIM
KV_W = N_KV_HEADS * HEAD_DIM
N_EXPERTS = 16
CAPACITY_FACTOR = 2
EPS = 1e-6

LANES = 128
A_W = Q_W + 2 * KV_W + 2 * SGU_W + POOL_W
VMEM_LIMIT = 56 * 1024 * 1024


def _cparams(*sem):
    return pltpu.CompilerParams(dimension_semantics=sem, vmem_limit_bytes=VMEM_LIMIT)


def _dot(a, b):
    return jnp.dot(a, b, preferred_element_type=F32)


def _dot_nt(a, b):
    return lax.dot_general(a, b, (((1,), (1,)), ((), ())), preferred_element_type=F32)


def _rms(x):
    return x * lax.rsqrt(jnp.mean(x * x, axis=-1, keepdims=True) + EPS)


def _head_norm_rope(xq, gc, gs, bd):
    sq = xq * xq
    hi = sq.astype(BF16)
    lo = (sq - hi.astype(F32)).astype(BF16)
    ss = _dot(hi, bd) + _dot(lo, bd)
    r = lax.rsqrt(ss * (1.0 / HEAD_DIM) + EPS)
    lane = lax.broadcasted_iota(jnp.int32, xq.shape, 1)
    first_half = (lane % 32) < 16
    partner = jnp.where(first_half, pltpu.roll(xq, LANES - 16, 1), pltpu.roll(xq, 16, 1))
    return r * (xq * gc + partner * gs)


def _inproj_kernel(x_ref, gmix_ref, w_ref, gcq_ref, gsq_ref, gck_ref, gsk_ref, bd_ref, gsgu_ref,
                   q_ref, k_ref, v_ref, u_ref, vn_ref, p_ref):
    h = (_rms(x_ref[...]) * gmix_ref[...]).astype(BF16)
    acc = _dot(h, w_ref[...])
    bd = bd_ref[...]
    for c in range(Q_W // LANES):
        sl = slice(c * LANES, (c + 1) * LANES)
        q_ref[:, sl] = _head_norm_rope(acc[:, sl], gcq_ref[...], gsq_ref[...], bd).astype(BF16)
    o = Q_W
    k_ref[...] = _head_norm_rope(acc[:, o:o + KV_W], gck_ref[...], gsk_ref[...], bd).astype(BF16)
    o += KV_W
    v_ref[...] = acc[:, o:o + KV_W].astype(BF16)
    o += KV_W
    u_ref[...] = acc[:, o:o + SGU_W]
    o += SGU_W
    vn_ref[...] = (_rms(acc[:, o:o + SGU_W]) * gsgu_ref[...]).astype(BF16)
    o += SGU_W
    p_ref[...] = acc[:, o:o + POOL_W]


def _inproj(x2, gmix, w_a, gcq, gsq, gck, gsk, bd, gsgu, S, tm):
    T, D = x2.shape
    nseq = S // tm
    row = lambda i: (i, 0)
    const = lambda i: (0, 0)
    pos = lambda i: (i % nseq, 0)
    return pl.pallas_call(
        _inproj_kernel,
        grid=(T // tm,),
        in_specs=[
            pl.BlockSpec((tm, D), row),
            pl.BlockSpec((1, D), const),
            pl.BlockSpec((D, A_W), const),
            pl.BlockSpec((tm, LANES), pos),
            pl.BlockSpec((tm, LANES), pos),
            pl.BlockSpec((tm, LANES), pos),
            pl.BlockSpec((tm, LANES), pos),
            pl.BlockSpec((LANES, LANES), const),
            pl.BlockSpec((1, SGU_W), const),
        ],
        out_specs=[
            pl.BlockSpec((tm, Q_W), row),
            pl.BlockSpec((tm, KV_W), row),
            pl.BlockSpec((tm, KV_W), row),
            pl.BlockSpec((tm, SGU_W), row),
            pl.BlockSpec((tm, SGU_W), row),
            pl.BlockSpec((tm, POOL_W), row),
        ],
        out_shape=[
            jax.ShapeDtypeStruct((T, Q_W), BF16),
            jax.ShapeDtypeStruct((T, KV_W), BF16),
            jax.ShapeDtypeStruct((T, KV_W), BF16),
            jax.ShapeDtypeStruct((T, SGU_W), F32),
            jax.ShapeDtypeStruct((T, SGU_W), BF16),
            jax.ShapeDtypeStruct((T, POOL_W), F32),
        ],
        compiler_params=_cparams("parallel"),
        name="inproj",
    )(x2, gmix, w_a, gcq, gsq, gck, gsk, bd, gsgu)


def _attn_kernel(q_ref, k_ref, v_ref, o_ref, kz_ref, vzt_ref, s_ref):
    @pl.when(pl.program_id(1) == 0)
    def _():
        lane = lax.broadcasted_iota(jnp.int32, k_ref.shape, 1)
        low = lane < HEAD_DIM
        k = k_ref[...].astype(F32)
        kr = pltpu.roll(k, HEAD_DIM, 1)
        kz_ref[0] = jnp.where(low, k, 0.0).astype(BF16)
        kz_ref[1] = jnp.where(low, 0.0, kr).astype(BF16)
        kz_ref[2] = jnp.where(low, kr, 0.0).astype(BF16)
        kz_ref[3] = jnp.where(low, 0.0, k).astype(BF16)
        vt = v_ref[...].astype(F32).T
        vtr = pltpu.roll(vt, HEAD_DIM, 0)
        row = lax.broadcasted_iota(jnp.int32, vt.shape, 0)
        top = row < HEAD_DIM
        ones_lo = jnp.where(row == 0, 1.0, 0.0)
        ones_hi = jnp.where(row == HEAD_DIM, 1.0, 0.0)
        vzt_ref[0] = jnp.where(top, vt, ones_hi).astype(BF16)
        vzt_ref[1] = jnp.where(top, ones_lo, vtr).astype(BF16)
        vzt_ref[2] = jnp.where(top, vtr, ones_hi).astype(BF16)
        vzt_ref[3] = jnp.where(top, ones_lo, vt).astype(BF16)

    tq = q_ref.shape[0]
    S = k_ref.shape[0]
    nchunk = S // KEY_CHUNK
    sub = 8
    slot0 = jnp.minimum(pl.program_id(1), 0)

    def scores(head, c, mrun):
        pair, parity = divmod(head, 2)
        kvi = 2 * ((2 * pair) // (N_HEADS // N_KV_HEADS)) + parity
        rows = slice(c * KEY_CHUNK, (c + 1) * KEY_CHUNK)
        s = _dot_nt(kz_ref[kvi, rows, :], q_ref[:, pair * LANES:(pair + 1) * LANES])
        s_ref[slot0 + head % 2, rows, :] = s
        m = jnp.max(s.reshape(KEY_CHUNK // sub, sub, tq), axis=0)
        return m if mrun is None else jnp.maximum(mrun, m)

    def weighted(head, c, m8, acc):
        pair, parity = divmod(head, 2)
        kvi = 2 * ((2 * pair) // (N_HEADS // N_KV_HEADS)) + parity
        rows = slice(c * KEY_CHUNK, (c + 1) * KEY_CHUNK)
        s3 = s_ref[slot0 + head % 2, rows, :].reshape(KEY_CHUNK // sub, sub, tq)
        e = jnp.exp2(s3 - m8[None]).reshape(KEY_CHUNK, tq).astype(BF16)
        d = _dot(vzt_ref[kvi, :, rows], e)
        return d if acc is None else acc + d

    top = lax.broadcasted_iota(jnp.int32, (LANES, tq), 0) < HEAD_DIM
    mrun = None
    for c in range(nchunk):
        mrun = scores(0, c, mrun)
    even = None
    for head in range(N_HEADS):
        m8 = jnp.broadcast_to(jnp.max(mrun, axis=0, keepdims=True), (sub, tq))
        mnext, acc = None, None
        for c in range(nchunk):
            if head + 1 < N_HEADS:
                mnext = scores(head + 1, c, mnext)
            acc = weighted(head, c, m8, acc)
        mrun = mnext
        pair, parity = divmod(head, 2)
        if parity == 0:
            even = acc * (1.0 / acc[HEAD_DIM:HEAD_DIM + 1])
        else:
            odd = acc * (1.0 / acc[0:1])
            o_ref[:, pair * LANES:(pair + 1) * LANES] = jnp.where(top, even, odd).T.astype(BF16)


def _attention(q, k, v, B, S, tq):
    T = q.shape[0]
    nq = S // tq
    return pl.pallas_call(
        _attn_kernel,
        grid=(B, nq),
        in_specs=[
            pl.BlockSpec((tq, Q_W), lambda b, i: (b * nq + i, 0)),
            pl.BlockSpec((S, KV_W), lambda b, i: (b, 0)),
            pl.BlockSpec((S, KV_W), lambda b, i: (b, 0)),
        ],
        out_specs=pl.BlockSpec((tq, Q_W), lambda b, i: (b * nq + i, 0)),
        out_shape=jax.ShapeDtypeStruct((T, Q_W), BF16),
        scratch_shapes=[pltpu.VMEM((4, S, KV_W), BF16), pltpu.VMEM((4, KV_W, S), BF16),
                        pltpu.VMEM((2, S, tq), F32)],
        compiler_params=_cparams("parallel", "arbitrary"),
        name="attention",
    )(q, k, v)


def _merge_kernel(x_ref, gmix_ref, wgl_ref, attn_ref, u_ref, vn_ref, p_ref, pprev_ref, pnext_ref,
                  wsp_ref, bsp_ref, wpool_ref, pscale_ref, wao_ref, wso_ref, wpo_ref, wout_ref,
                  gffn_ref, wr_ref, x1_ref, h2_ref, lg_ref, *, S):
    ts = x_ref.shape[0]
    x = x_ref[...]
    h = (_rms(x) * gmix_ref[...]).astype(BF16)

    vn = vn_ref[...]
    lane_b = lax.broadcasted_iota(jnp.int32, (SGU_CHUNK, SGU_W), 1) // SGU_HEAD
    zs = []
    for c in range(ts // SGU_CHUNK):
        vc = vn[c * SGU_CHUNK:(c + 1) * SGU_CHUNK]
        stacked = jnp.concatenate(
            [jnp.where(lane_b == g, vc, jnp.zeros_like(vc)) for g in range(SGU_GROUPS)], axis=0)
        zs.append(_dot(wsp_ref[...], stacked) + bsp_ref[...])
    sgu = (u_ref[...] * jnp.concatenate(zs, axis=0)).astype(BF16)
    br_b = _dot(sgu, wso_ref[...])

    pos0 = (pl.program_id(0) % (S // ts)) * ts
    p = p_ref[...]
    prev = jnp.where(pos0 == 0, 0.0, pprev_ref[...])
    nxt = jnp.where(pos0 + ts == S, 0.0, pnext_ref[...])
    ext = jnp.concatenate([prev, p, nxt], axis=0)
    n = ts + 2 * POOL_HALO
    s2 = ext + pltpu.roll(ext, 1, 0)
    s4 = pltpu.roll(s2, 1, 0) + pltpu.roll(s2, n - 1, 0)
    s8 = pltpu.roll(s4, 2, 0) + pltpu.roll(s4, n - 2, 0)
    s16 = pltpu.roll(s8, 4, 0) + pltpu.roll(s8, n - 4, 0)
    mid = slice(POOL_HALO, POOL_HALO + ts)
    grp = lax.broadcasted_iota(jnp.int32, (ts, POOL_W), 1) // POOL_HEAD
    t = pos0 + lax.broadcasted_iota(jnp.int32, (ts, POOL_W), 0)
    half = jnp.left_shift(1, grp)
    cnt = jnp.minimum(t + half - 1, S - 1) - jnp.maximum(t - half, 0) + 1
    wsum = jnp.where(grp == 0, s2[mid], jnp.where(grp == 1, s4[mid],
                                                  jnp.where(grp == 2, s8[mid], s16[mid])))
    d = (wsum / cnt.astype(F32) - p).astype(BF16)
    pooled = (_dot(d, wpool_ref[...]) * pscale_ref[...]).astype(BF16)
    br_c = _dot(pooled, wpo_ref[...])

    br_a = _dot(attn_ref[...], wao_ref[...])

    gl = _dot(h, wgl_ref[...])
    D = x.shape[1]
    gate = lambda j: 0.5 * (1.0 + jnp.tanh(0.5 * gl[:, j * D:(j + 1) * D]))
    merged = (gate(0) * br_a + gate(1) * br_b + gate(2) * br_c).astype(BF16)
    x1 = x + _dot(merged, wout_ref[...])
    x1_ref[...] = x1
    h2 = (_rms(x1) * gffn_ref[...]).astype(BF16)
    h2_ref[...] = h2
    lg_ref[...] = _dot(h2, wr_ref[...])


def _merge(x2, gmix, w_gl, attn, u, vn, p, wsp, bsp, wpool, pscale, wao, wso, wpo, wout, gffn, wr,
           S, ts):
    T, D = x2.shape
    row = lambda i: (i, 0)
    const = lambda i: (0, 0)
    hb = ts // POOL_HALO
    nhb = T // POOL_HALO
    full = lambda a: pl.BlockSpec(a.shape, const)
    return pl.pallas_call(
        functools.partial(_merge_kernel, S=S),
        grid=(T // ts,),
        in_specs=[
            pl.BlockSpec((ts, D), row),
            full(gmix),
            full(w_gl),
            pl.BlockSpec((ts, Q_W), row),
            pl.BlockSpec((ts, SGU_W), row),
            pl.BlockSpec((ts, SGU_W), row),
            pl.BlockSpec((ts, POOL_W), row),
            pl.BlockSpec((POOL_HALO, POOL_W), lambda i: (jnp.maximum(i * hb - 1, 0), 0)),
            pl.BlockSpec((POOL_HALO, POOL_W), lambda i: (jnp.minimum((i + 1) * hb, nhb - 1), 0)),
            full(wsp), full(bsp), full(wpool), full(pscale), full(wao), full(wso), full(wpo),
            full(wout), full(gffn), full(wr),
        ],
        out_specs=[
            pl.BlockSpec((ts, D), row),
            pl.BlockSpec((ts, D), row),
            pl.BlockSpec((ts, LANES), row),
        ],
        out_shape=[
            jax.ShapeDtypeStruct((T, D), F32),
            jax.ShapeDtypeStruct((T, D), BF16),
            jax.ShapeDtypeStruct((T, LANES), F32),
        ],
        compiler_params=_cparams("parallel"),
        name="merge",
    )(x2, gmix, w_gl, attn, u, vn, p, p, p, wsp, bsp, wpool, pscale, wao, wso, wpo, wout, gffn, wr)


def _excl_cumsum(m01, tri_ones):
    off = jnp.zeros((m01.shape[0], LANES), F32)
    pieces = []
    for j in range(m01.shape[1] // LANES):
        res = _dot(m01[:, j * LANES:(j + 1) * LANES].astype(BF16), tri_ones)
        pieces.append(res[:, :LANES] + off)
        off = off + res[:, LANES:]
    return jnp.concatenate(pieces, axis=1)


def _route_kernel(lg_ref, tri_ref, rank_ref, rankt_ref, afft_ref, offs_ref, *, cap):
    nb, S, _ = lg_ref.shape
    E = N_EXPERTS
    rows = []
    for b in range(nb):
        lt = lg_ref[b].T[:E]
        e = jnp.exp(lt - jnp.max(lt, axis=0, keepdims=True))
        rows.append(e / jnp.sum(e, axis=0, keepdims=True))
    aff = jnp.concatenate(rows, axis=0)
    bits = pltpu.bitcast(aff, jnp.int32)

    def refine(i, thr):
        cand = thr | jnp.left_shift(jnp.int32(1), 30 - i)
        cnt = jnp.sum(jnp.where(bits >= cand, 1.0, 0.0), axis=1, keepdims=True)
        return jnp.where(cnt >= cap, cand, thr)

    thr = lax.fori_loop(0, 31, refine, jnp.zeros((nb * E, 1), jnp.int32))
    gt = jnp.where(bits > thr, 1.0, 0.0)
    eq = jnp.where(bits == thr, 1.0, 0.0)
    need = cap - jnp.sum(gt, axis=1, keepdims=True)
    tri = tri_ref[...]
    sel = gt + eq * jnp.where(_excl_cumsum(eq, tri) < need, 1.0, 0.0)
    before = _excl_cumsum(sel, tri)
    rank = jnp.where(sel > 0.5, before, -1.0)
    nblk = S // TOKEN_BLOCK
    lane = lax.broadcasted_iota(jnp.int32, (nb * E, LANES), 1)
    offs = jnp.where(lane == nblk, float(cap), 0.0)
    for j in range(nblk):
        offs = jnp.where(lane == j, before[:, j * TOKEN_BLOCK:j * TOKEN_BLOCK + 1], offs)
    pad = jnp.zeros((LANES - E, S), F32)
    for b in range(nb):
        rb = rank[b * E:(b + 1) * E]
        rank_ref[b] = rb
        rankt_ref[b] = jnp.concatenate([rb, pad], axis=0).T
        afft_ref[b] = jnp.concatenate([aff[b * E:(b + 1) * E], pad], axis=0).T
        offs_ref[b] = offs[b * E:(b + 1) * E]


def _route(lg3, tri, cap, nb):
    B, S, _ = lg3.shape
    blk = lambda b: (b, 0, 0)
    return pl.pallas_call(
        functools.partial(_route_kernel, cap=cap),
        grid=(B // nb,),
        in_specs=[pl.BlockSpec((nb, S, LANES), blk), pl.BlockSpec(tri.shape, lambda b: (0, 0))],
        out_specs=[
            pl.BlockSpec((nb, N_EXPERTS, S), blk),
            pl.BlockSpec((nb, S, LANES), blk),
            pl.BlockSpec((nb, S, LANES), blk),
            pl.BlockSpec((nb, N_EXPERTS, LANES), blk),
        ],
        out_shape=[
            jax.ShapeDtypeStruct((B, N_EXPERTS, S), F32),
            jax.ShapeDtypeStruct((B, S, LANES), F32),
            jax.ShapeDtypeStruct((B, S, LANES), F32),
            jax.ShapeDtypeStruct((B, N_EXPERTS, LANES), F32),
        ],
        compiler_params=_cparams("parallel"),
        name="route",
    )(lg3, tri)


def _enumerate_tiles(offs_ref, tile_e, tile_g, b, j, nblk):
    def per_expert(e, nt):
        base = (b * N_EXPERTS + e) * (nblk + 1) + j
        lo = offs_ref[base]
        hi = offs_ref[base + 1]
        g_lo = jnp.right_shift(lo, TILE_SHIFT)
        cnt = jnp.where(hi > lo, jnp.right_shift(hi + SLOT_TILE - 1, TILE_SHIFT) - g_lo, 0)

        def per_tile(g, nt):
            tile_e[nt] = e
            tile_g[nt] = g
            return nt + 1

        return lax.fori_loop(g_lo, g_lo + cnt, per_tile, nt)

    return lax.fori_loop(0, N_EXPERTS, per_expert, jnp.int32(0))


def _chunk_tiles(tile_e, tile_g, c, ntile):
    out = []
    for i in range(CHUNK_TILES):
        t = c * CHUNK_TILES + i
        valid = t < ntile
        tt = jnp.where(valid, t, 0)
        e_t = tile_e[tt]
        g_t = tile_g[tt]
        out.append((e_t, g_t, jnp.where(valid, g_t * SLOT_TILE, NO_SLOT)))
    return out


def _tile_rows(g_t):
    return pl.ds(pl.multiple_of(g_t * SLOT_TILE, SLOT_TILE), SLOT_TILE)


def _max_tiles(cap):
    worst = N_EXPERTS * (cap // SLOT_TILE + 2)
    return -(-worst // CHUNK_TILES) * CHUNK_TILES


def _gather_kernel(offs_ref, h2_ref, rank_ref, xs_ref, tile_e, tile_g):
    nblk = rank_ref.shape[1]
    tb = h2_ref.shape[0]
    b, j = pl.program_id(0), pl.program_id(1)

    @pl.when(j == 0)
    def _():
        xs_ref[...] = jnp.zeros_like(xs_ref)

    ntile = _enumerate_tiles(offs_ref, tile_e, tile_g, b, j, nblk)
    sub = lax.broadcasted_iota(jnp.int32, (SLOT_TILE, tb), 0)

    def chunk(c, carry):
        entries = _chunk_tiles(tile_e, tile_g, c, ntile)
        onehot = jnp.concatenate(
            [jnp.where(rank_ref[e_t, pl.ds(j, 1), :] == (slot0 + sub).astype(F32), 1.0, 0.0).astype(BF16)
             for e_t, _, slot0 in entries], axis=0)
        rows = _dot(onehot, h2_ref[...]).astype(BF16)
        for i, (e_t, g_t, _) in enumerate(entries):
            xs_ref[e_t, _tile_rows(g_t), :] += rows[i * SLOT_TILE:(i + 1) * SLOT_TILE, :]
        return carry

    lax.fori_loop(0, jnp.right_shift(ntile + CHUNK_TILES - 1, CHUNK_SHIFT), chunk, 0)


def _gather(h2, rank4, offs, cap):
    B, S, D = h2.shape
    _, E, nblk, tb = rank4.shape
    return pl.pallas_call(
        _gather_kernel,
        grid_spec=pltpu.PrefetchScalarGridSpec(
            num_scalar_prefetch=1,
            grid=(B, nblk),
            in_specs=[
                pl.BlockSpec((None, tb, D), lambda b, j, offs: (b, j, 0)),
                pl.BlockSpec((None, E, nblk, tb), lambda b, j, offs: (b, 0, 0, 0)),
            ],
            out_specs=pl.BlockSpec((None, E, cap, D), lambda b, j, offs: (b, 0, 0, 0)),
            scratch_shapes=[pltpu.SMEM((_max_tiles(cap),), jnp.int32)] * 2,
        ),
        out_shape=jax.ShapeDtypeStruct((B, E, cap, D), BF16),
        compiler_params=_cparams("arbitrary", "arbitrary"),
        name="gather",
    )(offs, h2, rank4)


def _ffn_kernel(xs_ref, wg_ref, wu_ref, wd_ref, y_ref, wg_b, wu_b, wd_b):
    @pl.when(pl.program_id(1) == 0)
    def _():
        wg_b[...] = wg_ref[...].astype(BF16)
        wu_b[...] = wu_ref[...].astype(BF16)
        wd_b[...] = wd_ref[...].astype(BF16)

    nb, cap, D = xs_ref.shape
    xs = xs_ref[...].reshape(nb * cap, D)
    a = _dot(xs, wg_b[...])
    b = _dot(xs, wu_b[...])
    hidden = (a * jax.nn.sigmoid(a) * b).astype(BF16)
    y_ref[...] = _dot(hidden, wd_b[...]).astype(BF16).reshape(nb, cap, D)


def _ffn(xs, wg, wu, wd, nb):
    B, E, cap, D = xs.shape
    F = wg.shape[2]
    blk = pl.BlockSpec((nb, None, cap, D), lambda e, b: (b, e, 0, 0))
    return pl.pallas_call(
        _ffn_kernel,
        grid=(E, B // nb),
        in_specs=[
            blk,
            pl.BlockSpec((None, D, F), lambda e, b: (e, 0, 0)),
            pl.BlockSpec((None, D, F), lambda e, b: (e, 0, 0)),
            pl.BlockSpec((None, F, D), lambda e, b: (e, 0, 0)),
        ],
        out_specs=blk,
        out_shape=jax.ShapeDtypeStruct((B, E, cap, D), BF16),
        scratch_shapes=[pltpu.VMEM((D, F), BF16), pltpu.VMEM((D, F), BF16), pltpu.VMEM((F, D), BF16)],
        compiler_params=_cparams("arbitrary", "arbitrary"),
        name="ffn",
    )(xs, wg, wu, wd)


def _scatter_kernel(offs_ref, y_ref, rankt_ref, afft_ref, x1_ref, gout_ref, out_ref, tile_e, tile_g,
                    *, nblk, norm_out):
    b, j = pl.program_id(0), pl.program_id(1)
    ntile = _enumerate_tiles(offs_ref, tile_e, tile_g, b, j, nblk)
    out_ref[...] = x1_ref[...]
    rt = rankt_ref[...].astype(BF16)
    at = afft_ref[...].astype(BF16)
    width = CHUNK_TILES * SLOT_TILE
    col = lax.broadcasted_iota(jnp.int32, (1, width), 1)
    grp = jnp.right_shift(col, TILE_SHIFT)
    within = col & (SLOT_TILE - 1)
    lane_e = lax.broadcasted_iota(jnp.int32, (LANES, width), 0)

    def chunk(c, carry):
        entries = _chunk_tiles(tile_e, tile_g, c, ntile)
        ecol = jnp.zeros((1, width), jnp.int32)
        scol = jnp.full((1, width), NO_SLOT, jnp.int32)
        for i, (e_t, _, slot0) in enumerate(entries):
            ecol = jnp.where(grp == i, e_t, ecol)
            scol = jnp.where(grp == i, slot0, scol)
        pick = jnp.where(lane_e == ecol, 1.0, 0.0).astype(BF16)
        r = _dot(rt, pick)
        w = _dot(at, pick)
        scat = jnp.where(r == (scol + within).astype(F32), w, 0.0).astype(BF16)
        ys = jnp.concatenate([y_ref[e_t, _tile_rows(g_t), :] for e_t, g_t, _ in entries], axis=0)
        out_ref[...] += _dot(scat, ys)
        return carry

    lax.fori_loop(0, jnp.right_shift(ntile + CHUNK_TILES - 1, CHUNK_SHIFT), chunk, 0)
    if norm_out:
        out_ref[...] = _rms(out_ref[...]) * gout_ref[...]


def _scatter(y, rankt, afft, x1, gout, offs, tb, norm_out):
    B, E, cap, D = y.shape
    S = rankt.shape[1]
    nblk = S // tb
    return pl.pallas_call(
        functools.partial(_scatter_kernel, nblk=nblk, norm_out=norm_out),
        grid_spec=pltpu.PrefetchScalarGridSpec(
            num_scalar_prefetch=1,
            grid=(B, nblk),
            in_specs=[
                pl.BlockSpec((None, E, cap, D), lambda b, j, offs: (b, 0, 0, 0)),
                pl.BlockSpec((None, tb, LANES), lambda b, j, offs: (b, j, 0)),
                pl.BlockSpec((None, tb, LANES), lambda b, j, offs: (b, j, 0)),
                pl.BlockSpec((tb, D), lambda b, j, offs: (b * nblk + j, 0)),
                pl.BlockSpec((1, D), lambda b, j, offs: (0, 0)),
            ],
            out_specs=pl.BlockSpec((tb, D), lambda b, j, offs: (b * nblk + j, 0)),
            scratch_shapes=[pltpu.SMEM((_max_tiles(cap),), jnp.int32)] * 2,
        ),
        out_shape=jax.ShapeDtypeStruct((B * S, D), F32),
        compiler_params=_cparams("arbitrary", "arbitrary"),
        name="scatter",
    )(offs, y, rankt, afft, x1, gout)


def _rope_tables(S, g, scale):
    rows = S // GRID_W
    row = jnp.broadcast_to(jnp.arange(rows, dtype=F32)[:, None], (rows, GRID_W)).reshape(-1)
    col = jnp.broadcast_to(jnp.arange(GRID_W, dtype=F32)[None, :], (rows, GRID_W)).reshape(-1)
    part = HEAD_DIM // 2
    freqs = ROPE_THETA ** (-jnp.arange(0, part, 2, dtype=F32) / part)
    ang = jnp.stack([row[:, None] * freqs, col[:, None] * freqs], axis=1)
    cos, sin = jnp.cos(ang), jnp.sin(ang)
    c64 = jnp.concatenate([cos[:, 0], cos[:, 0], cos[:, 1], cos[:, 1]], axis=-1)
    s64 = jnp.concatenate([-sin[:, 0], sin[:, 0], -sin[:, 1], sin[:, 1]], axis=-1)
    g_swapped = g.reshape(2, 2, HEAD_DIM // 4)[:, ::-1, :].reshape(HEAD_DIM)
    gc = jnp.tile(c64 * g[None, :], (1, 2)) * scale
    gs = jnp.tile(s64 * g_swapped[None, :], (1, 2)) * scale
    return gc, gs


def _block_diag(blocks):
    n = blocks.shape[0]
    m = blocks.shape[1]
    eye = jnp.eye(n, dtype=blocks.dtype)
    return (eye[:, None, :, None] * blocks[:, :, None, :]).reshape(n * m, n * blocks.shape[2])


def kernel(x, g_mix, w_in, g_q, g_k, g_sgu, w_spatial, b_spatial, w_pool, pool_scale, w_attn_o,
           w_sgu_o, w_pool_o, w_out, g_ffn, w_router, w_gate_e, w_up_e, w_down_e, g_final):
    B, S, D = x.shape
    depth = w_in.shape[0]
    cap = CAPACITY_FACTOR * S // N_EXPERTS
    x2 = x.reshape(B * S, D)

    bd = _block_diag(jnp.ones((2, HEAD_DIM, HEAD_DIM), F32)).astype(BF16)
    k_i = lax.broadcasted_iota(jnp.int32, (LANES, LANES), 0)
    l_i = lax.broadcasted_iota(jnp.int32, (LANES, LANES), 1)
    tri = jnp.concatenate([(k_i < l_i).astype(BF16), jnp.ones((LANES, LANES), BF16)], axis=1)

    for l in range(depth):
        gcq, gsq = _rope_tables(S, g_q[l], HEAD_DIM ** -0.5 * math.log2(math.e))
        gck, gsk = _rope_tables(S, g_k[l], 1.0)
        w_a = w_in[l, :, :A_W].astype(BF16)
        w_gl = w_in[l, :, A_W:].astype(BF16)
        gmix = g_mix[l][None, :]
        q, k, v, u, vn, p = _inproj(x2, gmix, w_a, gcq, gsq, gck, gsk, bd, g_sgu[l][None, :], S, 512)
        attn = _attention(q, k, v, B, S, 512)

        wsp = w_spatial[l].transpose(1, 0, 2).reshape(SGU_CHUNK, SGU_GROUPS * SGU_CHUNK).astype(BF16)
        bsp = jnp.repeat(b_spatial[l].T, SGU_HEAD, axis=1)
        wpool = _block_diag(w_pool[l]).astype(BF16)
        wr = jnp.pad(w_router[l], ((0, 0), (0, LANES - N_EXPERTS))).astype(BF16)
        x1, h2, lg = _merge(
            x2, gmix, w_gl, attn, u, vn, p, wsp, bsp, wpool, pool_scale[l][None, :],
            w_attn_o[l].astype(BF16), w_sgu_o[l].astype(BF16), w_pool_o[l].astype(BF16),
            w_out[l].astype(BF16), g_ffn[l][None, :], wr, S, 512)

        rank, rankt, afft, offs = _route(lg.reshape(B, S, LANES), tri, cap, math.gcd(B, 4))
        nblk = S // TOKEN_BLOCK
        offs = offs[:, :, :nblk + 1].astype(jnp.int32).reshape(-1)
        xs = _gather(h2.reshape(B, S, D), rank.reshape(B, N_EXPERTS, nblk, TOKEN_BLOCK), offs, cap)
        y = _ffn(xs, w_gate_e[l], w_up_e[l], w_down_e[l], math.gcd(B, 4))
        x2 = _scatter(y, rankt, afft, x1, g_final[None, :], offs, TOKEN_BLOCK,
                      norm_out=(l == depth - 1))

    return x2.reshape(B, S, D)
```

```python
import functools
import math

import jax
import jax.numpy as jnp
from jax import lax
from jax.experimental import pallas as pl
from jax.experimental.pallas import tpu as pltpu

F32 = jnp.float32
BF16 = jnp.bfloat16

GRID_W = 64
N_HEADS = 8
N_KV_HEADS = 2
HEAD_DIM = 64
ROPE_THETA = 10000.0
SGU_GROUPS = 4
SGU_HEAD = 64
SGU_W = SGU_GROUPS * SGU_HEAD
SGU_CHUNK = 128
POOL_WINDOWS = (2, 4, 8, 16)
POOL_HEAD = 64
POOL_W = len(POOL_WINDOWS) * POOL_HEAD
POOL_HALO = 8
KEY_CHUNK = 256
TOKEN_BLOCK = 256
TILE_SHIFT = 4
SLOT_TILE = 1 << TILE_SHIFT
CHUNK_SHIFT = 5
CHUNK_TILES = 1 << CHUNK_SHIFT
NO_SLOT = -1024
Q_W = N_HEADS * HEAD_DIM
KV_W = N_KV_HEADS * HEAD_DIM
N_EXPERTS = 16
CAPACITY_FACTOR = 2
EPS = 1e-6

LANES = 128
A_W = Q_W + 2 * KV_W + 2 * SGU_W + POOL_W
VMEM_LIMIT = 56 * 1024 * 1024


def _cparams(*sem):
    return pltpu.CompilerParams(dimension_semantics=sem, vmem_limit_bytes=VMEM_LIMIT)


def _dot(a, b):
    return jnp.dot(a, b, preferred_element_type=F32)


def _dot_nt(a, b):
    return lax.dot_general(a, b, (((1,), (1,)), ((), ())), preferred_element_type=F32)


def _rms(x):
    return x * lax.rsqrt(jnp.mean(x * x, axis=-1, keepdims=True) + EPS)


def _head_norm_rope(xq, gc, gs, bd):
    sq = xq * xq
    hi = sq.astype(BF16)
    lo = (sq - hi.astype(F32)).astype(BF16)
    ss = _dot(hi, bd) + _dot(lo, bd)
    r = lax.rsqrt(ss * (1.0 / HEAD_DIM) + EPS)
    lane = lax.broadcasted_iota(jnp.int32, xq.shape, 1)
    first_half = (lane % 32) < 16
    partner = jnp.where(first_half, pltpu.roll(xq, LANES - 16, 1), pltpu.roll(xq, 16, 1))
    return r * (xq * gc + partner * gs)


def _inproj_kernel(x_ref, gmix_ref, w_ref, gcq_ref, gsq_ref, gck_ref, gsk_ref, bd_ref, gsgu_ref,
                   q_ref, k_ref, v_ref, u_ref, vn_ref, p_ref):
    h = (_rms(x_ref[...]) * gmix_ref[...]).astype(BF16)
    acc = _dot(h, w_ref[...])
    bd = bd_ref[...]
    for c in range(Q_W // LANES):
        sl = slice(c * LANES, (c + 1) * LANES)
        q_ref[:, sl] = _head_norm_rope(acc[:, sl], gcq_ref[...], gsq_ref[...], bd).astype(BF16)
    o = Q_W
    k_ref[...] = _head_norm_rope(acc[:, o:o + KV_W], gck_ref[...], gsk_ref[...], bd).astype(BF16)
    o += KV_W
    v_ref[...] = acc[:, o:o + KV_W].astype(BF16)
    o += KV_W
    u_ref[...] = acc[:, o:o + SGU_W]
    o += SGU_W
    vn_ref[...] = (_rms(acc[:, o:o + SGU_W]) * gsgu_ref[...]).astype(BF16)
    o += SGU_W
    p_ref[...] = acc[:, o:o + POOL_W]


def _inproj(x2, gmix, w_a, gcq, gsq, gck, gsk, bd, gsgu, S, tm):
    T, D = x2.shape
    nseq = S // tm
    row = lambda i: (i, 0)
    const = lambda i: (0, 0)
    pos = lambda i: (i % nseq, 0)
    return pl.pallas_call(
        _inproj_kernel,
        grid=(T // tm,),
        in_specs=[
            pl.BlockSpec((tm, D), row),
            pl.BlockSpec((1, D), const),
            pl.BlockSpec((D, A_W), const),
            pl.BlockSpec((tm, LANES), pos),
            pl.BlockSpec((tm, LANES), pos),
            pl.BlockSpec((tm, LANES), pos),
            pl.BlockSpec((tm, LANES), pos),
            pl.BlockSpec((LANES, LANES), const),
            pl.BlockSpec((1, SGU_W), const),
        ],
        out_specs=[
            pl.BlockSpec((tm, Q_W), row),
            pl.BlockSpec((tm, KV_W), row),
            pl.BlockSpec((tm, KV_W), row),
            pl.BlockSpec((tm, SGU_W), row),
            pl.BlockSpec((tm, SGU_W), row),
            pl.BlockSpec((tm, POOL_W), row),
        ],
        out_shape=[
            jax.ShapeDtypeStruct((T, Q_W), BF16),
            jax.ShapeDtypeStruct((T, KV_W), BF16),
            jax.ShapeDtypeStruct((T, KV_W), BF16),
            jax.ShapeDtypeStruct((T, SGU_W), F32),
            jax.ShapeDtypeStruct((T, SGU_W), BF16),
            jax.ShapeDtypeStruct((T, POOL_W), F32),
        ],
        compiler_params=_cparams("parallel"),
        name="inproj",
    )(x2, gmix, w_a, gcq, gsq, gck, gsk, bd, gsgu)


def _attn_kernel(q_ref, k_ref, v_ref, o_ref, kz_ref, vzt_ref, s_ref):
    @pl.when(pl.program_id(1) == 0)
    def _():
        lane = lax.broadcasted_iota(jnp.int32, k_ref.shape, 1)
        low = lane < HEAD_DIM
        k = k_ref[...].astype(F32)
        kr = pltpu.roll(k, HEAD_DIM, 1)
        kz_ref[0] = jnp.where(low, k, 0.0).astype(BF16)
        kz_ref[1] = jnp.where(low, 0.0, kr).astype(BF16)
        kz_ref[2] = jnp.where(low, kr, 0.0).astype(BF16)
        kz_ref[3] = jnp.where(low, 0.0, k).astype(BF16)
        vt = v_ref[...].astype(F32).T
        vtr = pltpu.roll(vt, HEAD_DIM, 0)
        row = lax.broadcasted_iota(jnp.int32, vt.shape, 0)
        top = row < HEAD_DIM
        ones_lo = jnp.where(row == 0, 1.0, 0.0)
        ones_hi = jnp.where(row == HEAD_DIM, 1.0, 0.0)
        vzt_ref[0] = jnp.where(top, vt, ones_hi).astype(BF16)
        vzt_ref[1] = jnp.where(top, ones_lo, vtr).astype(BF16)
        vzt_ref[2] = jnp.where(top, vtr, ones_hi).astype(BF16)
        vzt_ref[3] = jnp.where(top, ones_lo, vt).astype(BF16)

    tq = q_ref.shape[0]
    S = k_ref.shape[0]
    nchunk = S // KEY_CHUNK
    sub = 8
    slot0 = jnp.minimum(pl.program_id(1), 0)

    def scores(head, c, mrun):
        pair, parity = divmod(head, 2)
        kvi = 2 * ((2 * pair) // (N_HEADS // N_KV_HEADS)) + parity
        rows = slice(c * KEY_CHUNK, (c + 1) * KEY_CHUNK)
        s = _dot_nt(kz_ref[kvi, rows, :], q_ref[:, pair * LANES:(pair + 1) * LANES])
        s_ref[slot0 + head % 2, rows, :] = s
        m = jnp.max(s.reshape(KEY_CHUNK // sub, sub, tq), axis=0)
        return m if mrun is None else jnp.maximum(mrun, m)

    def weighted(head, c, m8, acc):
        pair, parity = divmod(head, 2)
        kvi = 2 * ((2 * pair) // (N_HEADS // N_KV_HEADS)) + parity
        rows = slice(c * KEY_CHUNK, (c + 1) * KEY_CHUNK)
        s3 = s_ref[slot0 + head % 2, rows, :].reshape(KEY_CHUNK // sub, sub, tq)
        e = jnp.exp2(s3 - m8[None]).reshape(KEY_CHUNK, tq).astype(BF16)
        d = _dot(vzt_ref[kvi, :, rows], e)
        return d if acc is None else acc + d

    top = lax.broadcasted_iota(jnp.int32, (LANES, tq), 0) < HEAD_DIM
    mrun = None
    for c in range(nchunk):
        mrun = scores(0, c, mrun)
    even = None
    for head in range(N_HEADS):
        m8 = jnp.broadcast_to(jnp.max(mrun, axis=0, keepdims=True), (sub, tq))
        mnext, acc = None, None
        for c in range(nchunk):
            if head + 1 < N_HEADS:
                mnext = scores(head + 1, c, mnext)
            acc = weighted(head, c, m8, acc)
        mrun = mnext
        pair, parity = divmod(head, 2)
        if parity == 0:
            even = acc * (1.0 / acc[HEAD_DIM:HEAD_DIM + 1])
        else:
            odd = acc * (1.0 / acc[0:1])
            o_ref[:, pair * LANES:(pair + 1) * LANES] = jnp.where(top, even, odd).T.astype(BF16)


def _attention(q, k, v, B, S, tq):
    T = q.shape[0]
    nq = S // tq
    return pl.pallas_call(
        _attn_kernel,
        grid=(B, nq),
        in_specs=[
            pl.BlockSpec((tq, Q_W), lambda b, i: (b * nq + i, 0)),
            pl.BlockSpec((S, KV_W), lambda b, i: (b, 0)),
            pl.BlockSpec((S, KV_W), lambda b, i: (b, 0)),
        ],
        out_specs=pl.BlockSpec((tq, Q_W), lambda b, i: (b * nq + i, 0)),
        out_shape=jax.ShapeDtypeStruct((T, Q_W), BF16),
        scratch_shapes=[pltpu.VMEM((4, S, KV_W), BF16), pltpu.VMEM((4, KV_W, S), BF16),
                        pltpu.VMEM((2, S, tq), F32)],
        compiler_params=_cparams("parallel", "arbitrary"),
        name="attention",
    )(q, k, v)


def _merge_kernel(x_ref, gmix_ref, wgl_ref, attn_ref, u_ref, vn_ref, p_ref, pprev_ref, pnext_ref,
                  wsp_ref, bsp_ref, wpool_ref, pscale_ref, wao_ref, wso_ref, wpo_ref, wout_ref,
                  gffn_ref, wr_ref, x1_ref, h2_ref, lg_ref, *, S):
    ts = x_ref.shape[0]
    x = x_ref[...]
    h = (_rms(x) * gmix_ref[...]).astype(BF16)

    vn = vn_ref[...]
    lane_b = lax.broadcasted_iota(jnp.int32, (SGU_CHUNK, SGU_W), 1) // SGU_HEAD
    zs = []
    for c in range(ts // SGU_CHUNK):
        vc = vn[c * SGU_CHUNK:(c + 1) * SGU_CHUNK]
        stacked = jnp.concatenate(
            [jnp.where(lane_b == g, vc, jnp.zeros_like(vc)) for g in range(SGU_GROUPS)], axis=0)
        zs.append(_dot(wsp_ref[...], stacked) + bsp_ref[...])
    sgu = (u_ref[...] * jnp.concatenate(zs, axis=0)).astype(BF16)
    br_b = _dot(sgu, wso_ref[...])

    pos0 = (pl.program_id(0) % (S // ts)) * ts
    p = p_ref[...]
    prev = jnp.where(pos0 == 0, 0.0, pprev_ref[...])
    nxt = jnp.where(pos0 + ts == S, 0.0, pnext_ref[...])
    ext = jnp.concatenate([prev, p, nxt], axis=0)
    n = ts + 2 * POOL_HALO
    s2 = ext + pltpu.roll(ext, 1, 0)
    s4 = pltpu.roll(s2, 1, 0) + pltpu.roll(s2, n - 1, 0)
    s8 = pltpu.roll(s4, 2, 0) + pltpu.roll(s4, n - 2, 0)
    s16 = pltpu.roll(s8, 4, 0) + pltpu.roll(s8, n - 4, 0)
    mid = slice(POOL_HALO, POOL_HALO + ts)
    grp = lax.broadcasted_iota(jnp.int32, (ts, POOL_W), 1) // POOL_HEAD
    t = pos0 + lax.broadcasted_iota(jnp.int32, (ts, POOL_W), 0)
    half = jnp.left_shift(1, grp)
    cnt = jnp.minimum(t + half - 1, S - 1) - jnp.maximum(t - half, 0) + 1
    wsum = jnp.where(grp == 0, s2[mid], jnp.where(grp == 1, s4[mid],
                                                  jnp.where(grp == 2, s8[mid], s16[mid])))
    d = (wsum / cnt.astype(F32) - p).astype(BF16)
    pooled = (_dot(d, wpool_ref[...]) * pscale_ref[...]).astype(BF16)
    br_c = _dot(pooled, wpo_ref[...])

    br_a = _dot(attn_ref[...], wao_ref[...])

    gl = _dot(h, wgl_ref[...])
    D = x.shape[1]
    gate = lambda j: 0.5 * (1.0 + jnp.tanh(0.5 * gl[:, j * D:(j + 1) * D]))
    merged = (gate(0) * br_a + gate(1) * br_b + gate(2) * br_c).astype(BF16)
    x1 = x + _dot(merged, wout_ref[...])
    x1_ref[...] = x1
    h2 = (_rms(x1) * gffn_ref[...]).astype(BF16)
    h2_ref[...] = h2
    lg_ref[...] = _dot(h2, wr_ref[...])


def _merge(x2, gmix, w_gl, attn, u, vn, p, wsp, bsp, wpool, pscale, wao, wso, wpo, wout, gffn, wr,
           S, ts):
    T, D = x2.shape
    row = lambda i: (i, 0)
    const = lambda i: (0, 0)
    hb = ts // POOL_HALO
    nhb = T // POOL_HALO
    full = lambda a: pl.BlockSpec(a.shape, const)
    return pl.pallas_call(
        functools.partial(_merge_kernel, S=S),
        grid=(T // ts,),
        in_specs=[
            pl.BlockSpec((ts, D), row),
            full(gmix),
            full(w_gl),
            pl.BlockSpec((ts, Q_W), row),
            pl.BlockSpec((ts, SGU_W), row),
            pl.BlockSpec((ts, SGU_W), row),
            pl.BlockSpec((ts, POOL_W), row),
            pl.BlockSpec((POOL_HALO, POOL_W), lambda i: (jnp.maximum(i * hb - 1, 0), 0)),
            pl.BlockSpec((POOL_HALO, POOL_W), lambda i: (jnp.minimum((i + 1) * hb, nhb - 1), 0)),
            full(wsp), full(bsp), full(wpool), full(pscale), full(wao), full(wso), full(wpo),
            full(wout), full(gffn), full(wr),
        ],
        out_specs=[
            pl.BlockSpec((ts, D), row),
            pl.BlockSpec((ts, D), row),
            pl.BlockSpec((ts, LANES), row),
        ],
        out_shape=[
            jax.ShapeDtypeStruct((T, D), F32),
            jax.ShapeDtypeStruct((T, D), BF16),
            jax.ShapeDtypeStruct((T, LANES), F32),
        ],
        compiler_params=_cparams("parallel"),
        name="merge",
    )(x2, gmix, w_gl, attn, u, vn, p, p, p, wsp, bsp, wpool, pscale, wao, wso, wpo, wout, gffn, wr)


def _excl_cumsum(m01, tri_ones):
    off = jnp.zeros((m01.shape[0], LANES), F32)
    pieces = []
    for j in range(m01.shape[1] // LANES):
        res = _dot(m01[:, j * LANES:(j + 1) * LANES].astype(BF16), tri_ones)
        pieces.append(res[:, :LANES] + off)
        off = off + res[:, LANES:]
    return jnp.concatenate(pieces, axis=1)


def _tile_lists(offs, nblk, list_len):
    E = offs.shape[0]
    lane = lax.broadcasted_iota(jnp.int32, offs.shape, 1)
    hi = pltpu.roll(offs, LANES - 1, 1)
    g_lo = jnp.floor(offs * (1.0 / SLOT_TILE))
    g_hi = jnp.floor((hi + (SLOT_TILE - 1)) * (1.0 / SLOT_TILE))
    cnt = jnp.where((hi > offs) & (lane < nblk), g_hi - g_lo, 0.0)
    below = (lax.broadcasted_iota(jnp.int32, (E, E), 1) < lax.broadcasted_iota(jnp.int32, (E, E), 0))
    base = _dot(jnp.where(below, 1.0, 0.0).astype(BF16), cnt.astype(BF16))
    end = base + cnt
    t = lax.broadcasted_iota(jnp.int32, (1, list_len), 1).astype(F32)
    row_e = lax.broadcasted_iota(jnp.int32, (E, list_len), 0).astype(F32)
    blocks = []
    for j in range(nblk):
        col = slice(j, j + 1)
        e_t = jnp.sum(jnp.where(end[:, col] <= t, 1.0, 0.0), axis=0, keepdims=True)
        shift = jnp.sum(jnp.where(row_e == e_t, g_lo[:, col] - base[:, col], 0.0), axis=0, keepdims=True)
        valid = e_t < E
        g_t = jnp.where(valid, t + shift, 0.0)
        total = jnp.broadcast_to(end[E - 1:E, col], (1, list_len))
        rows = [jnp.where(valid, e_t, 0.0), g_t, jnp.where(valid, g_t * SLOT_TILE, float(NO_SLOT)),
                total, jnp.zeros((4, list_len), F32)]
        blocks.append(jnp.concatenate(rows, axis=0).astype(jnp.int32)[None])
    return jnp.concatenate(blocks, axis=0)


def _route_kernel(lg_ref, tri_ref, rank_ref, aff_ref, lists_ref, *, cap):
    nb, S, _ = lg_ref.shape
    E = N_EXPERTS
    rows = []
    for b in range(nb):
        lt = lg_ref[b].T[:E]
        e = jnp.exp(lt - jnp.max(lt, axis=0, keepdims=True))
        rows.append(e / jnp.sum(e, axis=0, keepdims=True))
    aff = jnp.concatenate(rows, axis=0)
    bits = pltpu.bitcast(aff, jnp.int32)

    def refine(i, thr):
        cand = thr | jnp.left_shift(jnp.int32(1), 30 - i)
        cnt = jnp.sum(jnp.where(bits >= cand, 1.0, 0.0), axis=1, keepdims=True)
        return jnp.where(cnt >= cap, cand, thr)

    thr = lax.fori_loop(0, 31, refine, jnp.zeros((nb * E, 1), jnp.int32))
    gt = jnp.where(bits > thr, 1.0, 0.0)
    eq = jnp.where(bits == thr, 1.0, 0.0)
    need = cap - jnp.sum(gt, axis=1, keepdims=True)
    tri = tri_ref[...]
    sel = gt + eq * jnp.where(_excl_cumsum(eq, tri) < need, 1.0, 0.0)
    before = _excl_cumsum(sel, tri)
    rank = jnp.where(sel > 0.5, before, -1.0)
    nblk = S // TOKEN_BLOCK
    lane = lax.broadcasted_iota(jnp.int32, (nb * E, LANES), 1)
    offs = jnp.where(lane == nblk, float(cap), 0.0)
    for j in range(nblk):
        offs = jnp.where(lane == j, before[:, j * TOKEN_BLOCK:j * TOKEN_BLOCK + 1], offs)
    list_len = lists_ref.shape[-1]
    for b in range(nb):
        rank_ref[b] = rank[b * E:(b + 1) * E]
        aff_ref[b] = aff[b * E:(b + 1) * E]
        lists_ref[b] = _tile_lists(offs[b * E:(b + 1) * E], nblk, list_len)


def _list_len(cap):
    worst = N_EXPERTS * (cap // SLOT_TILE + 2)
    return -(-worst // LANES) * LANES


def _route(lg3, tri, cap, nb):
    B, S, _ = lg3.shape
    nblk = S // TOKEN_BLOCK
    blk = lambda b: (b, 0, 0)
    lists_shape = (nblk, 8, _list_len(cap))
    return pl.pallas_call(
        functools.partial(_route_kernel, cap=cap),
        grid=(B // nb,),
        in_specs=[pl.BlockSpec((nb, S, LANES), blk), pl.BlockSpec(tri.shape, lambda b: (0, 0))],
        out_specs=[
            pl.BlockSpec((nb, N_EXPERTS, S), blk),
            pl.BlockSpec((nb, N_EXPERTS, S), blk),
            pl.BlockSpec((nb,) + lists_shape, lambda b: (b, 0, 0, 0)),
        ],
        out_shape=[
            jax.ShapeDtypeStruct((B, N_EXPERTS, S), F32),
            jax.ShapeDtypeStruct((B, N_EXPERTS, S), F32),
            jax.ShapeDtypeStruct((B,) + lists_shape, jnp.int32),
        ],
        compiler_params=_cparams("parallel"),
        name="route",
    )(lg3, tri)


def _chunk_tiles(list_ref, c):
    out = []
    for i in range(CHUNK_TILES):
        t = c * CHUNK_TILES + i
        out.append((list_ref[0, t], list_ref[1, t], list_ref[2, t]))
    return out


def _num_chunks(list_ref):
    return jnp.right_shift(list_ref[3, 0] + CHUNK_TILES - 1, CHUNK_SHIFT)


def _tile_rows(g_t):
    return pl.ds(pl.multiple_of(g_t * SLOT_TILE, SLOT_TILE), SLOT_TILE)


def _onehot(entries, rank_ref, j, weight_ref=None):
    tb = rank_ref.shape[-1]
    sub = lax.broadcasted_iota(jnp.int32, (SLOT_TILE, tb), 0)
    tiles = []
    for e_t, _, slot0 in entries:
        hit = rank_ref[e_t, pl.ds(j, 1), :] == (slot0 + sub).astype(F32)
        val = 1.0 if weight_ref is None else weight_ref[e_t, pl.ds(j, 1), :]
        tiles.append(jnp.where(hit, val, 0.0).astype(BF16))
    return jnp.concatenate(tiles, axis=0)


def _gather_kernel(list_ref, h2_ref, rank_ref, xs_ref):
    j = pl.program_id(1)

    @pl.when(j == 0)
    def _():
        xs_ref[...] = jnp.zeros_like(xs_ref)

    def chunk(c, carry):
        entries = _chunk_tiles(list_ref, c)
        rows = _dot(_onehot(entries, rank_ref, j), h2_ref[...]).astype(BF16)
        for i, (e_t, g_t, _) in enumerate(entries):
            xs_ref[e_t, _tile_rows(g_t), :] += rows[i * SLOT_TILE:(i + 1) * SLOT_TILE, :]
        return carry

    lax.fori_loop(0, _num_chunks(list_ref), chunk, 0)


def _list_spec(lists):
    return pl.BlockSpec((None, None) + lists.shape[2:], lambda b, j: (b, j, 0, 0),
                        memory_space=pltpu.SMEM)


def _gather(h2, rank4, lists, cap):
    B, S, D = h2.shape
    _, E, nblk, tb = rank4.shape
    return pl.pallas_call(
        _gather_kernel,
        grid=(B, nblk),
        in_specs=[
            _list_spec(lists),
            pl.BlockSpec((None, tb, D), lambda b, j: (b, j, 0)),
            pl.BlockSpec((None, E, nblk, tb), lambda b, j: (b, 0, 0, 0)),
        ],
        out_specs=pl.BlockSpec((None, E, cap, D), lambda b, j: (b, 0, 0, 0)),
        out_shape=jax.ShapeDtypeStruct((B, E, cap, D), BF16),
        compiler_params=_cparams("arbitrary", "arbitrary"),
        name="gather",
    )(lists, h2, rank4)


def _ffn_kernel(xs_ref, wg_ref, wu_ref, wd_ref, y_ref, wg_b, wu_b, wd_b):
    @pl.when(pl.program_id(1) == 0)
    def _():
        wg_b[...] = wg_ref[...].astype(BF16)
        wu_b[...] = wu_ref[...].astype(BF16)
        wd_b[...] = wd_ref[...].astype(BF16)

    nb, cap, D = xs_ref.shape
    xs = xs_ref[...].reshape(nb * cap, D)
    a = _dot(xs, wg_b[...])
    b = _dot(xs, wu_b[...])
    hidden = (a * jax.nn.sigmoid(a) * b).astype(BF16)
    y_ref[...] = _dot(hidden, wd_b[...]).astype(BF16).reshape(nb, cap, D)


def _ffn(xs, wg, wu, wd, layer, nb):
    B, E, cap, D = xs.shape
    F = wg.shape[3]
    blk = pl.BlockSpec((nb, None, cap, D), lambda e, b: (b, e, 0, 0))
    return pl.pallas_call(
        _ffn_kernel,
        grid=(E, B // nb),
        in_specs=[
            blk,
            pl.BlockSpec((None, None, D, F), lambda e, b: (layer, e, 0, 0)),
            pl.BlockSpec((None, None, D, F), lambda e, b: (layer, e, 0, 0)),
            pl.BlockSpec((None, None, F, D), lambda e, b: (layer, e, 0, 0)),
        ],
        out_specs=blk,
        out_shape=jax.ShapeDtypeStruct((B, E, cap, D), BF16),
        scratch_shapes=[pltpu.VMEM((D, F), BF16), pltpu.VMEM((D, F), BF16), pltpu.VMEM((F, D), BF16)],
        compiler_params=_cparams("arbitrary", "arbitrary"),
        name="ffn",
    )(xs, wg, wu, wd)


def _scatter_kernel(list_ref, y_ref, rank_ref, aff_ref, x1_ref, gout_ref, out_ref, *, norm_out):
    j = pl.program_id(1)
    out_ref[...] = x1_ref[...]

    def chunk(c, carry):
        entries = _chunk_tiles(list_ref, c)
        weights = _onehot(entries, rank_ref, j, aff_ref)
        ys = jnp.concatenate([y_ref[e_t, _tile_rows(g_t), :] for e_t, g_t, _ in entries], axis=0)
        out_ref[...] += lax.dot_general(weights, ys, (((0,), (0,)), ((), ())),
                                        preferred_element_type=F32)
        return carry

    lax.fori_loop(0, _num_chunks(list_ref), chunk, 0)
    if norm_out:
        out_ref[...] = _rms(out_ref[...]) * gout_ref[...]


def _scatter(y, rank4, aff4, x1, gout, lists, norm_out):
    B, E, cap, D = y.shape
    _, _, nblk, tb = rank4.shape
    per_seq = pl.BlockSpec((None, E, nblk, tb), lambda b, j: (b, 0, 0, 0))
    return pl.pallas_call(
        functools.partial(_scatter_kernel, norm_out=norm_out),
        grid=(B, nblk),
        in_specs=[
            _list_spec(lists),
            pl.BlockSpec((None, E, cap, D), lambda b, j: (b, 0, 0, 0)),
            per_seq,
            per_seq,
            pl.BlockSpec((tb, D), lambda b, j: (b * nblk + j, 0)),
            pl.BlockSpec((1, D), lambda b, j: (0, 0)),
        ],
        out_specs=pl.BlockSpec((tb, D), lambda b, j: (b * nblk + j, 0)),
        out_shape=jax.ShapeDtypeStruct((B * nblk * tb, D), F32),
        compiler_params=_cparams("arbitrary", "arbitrary"),
        name="scatter",
    )(lists, y, rank4, aff4, x1, gout)


def _rope_tables(S, g, scale):
    rows = S // GRID_W
    row = jnp.broadcast_to(jnp.arange(rows, dtype=F32)[:, None], (rows, GRID_W)).reshape(-1)
    col = jnp.broadcast_to(jnp.arange(GRID_W, dtype=F32)[None, :], (rows, GRID_W)).reshape(-1)
    part = HEAD_DIM // 2
    freqs = ROPE_THETA ** (-jnp.arange(0, part, 2, dtype=F32) / part)
    ang = jnp.stack([row[:, None] * freqs, col[:, None] * freqs], axis=1)
    cos, sin = jnp.cos(ang), jnp.sin(ang)
    c64 = jnp.concatenate([cos[:, 0], cos[:, 0], cos[:, 1], cos[:, 1]], axis=-1)
    s64 = jnp.concatenate([-sin[:, 0], sin[:, 0], -sin[:, 1], sin[:, 1]], axis=-1)
    g_swapped = g.reshape(2, 2, HEAD_DIM // 4)[:, ::-1, :].reshape(HEAD_DIM)
    gc = jnp.tile(c64 * g[None, :], (1, 2)) * scale
    gs = jnp.tile(s64 * g_swapped[None, :], (1, 2)) * scale
    return gc, gs


def _block_diag(blocks):
    n = blocks.shape[0]
    m = blocks.shape[1]
    eye = jnp.eye(n, dtype=blocks.dtype)
    return (eye[:, None, :, None] * blocks[:, :, None, :]).reshape(n * m, n * blocks.shape[2])


def kernel(x, g_mix, w_in, g_q, g_k, g_sgu, w_spatial, b_spatial, w_pool, pool_scale, w_attn_o,
           w_sgu_o, w_pool_o, w_out, g_ffn, w_router, w_gate_e, w_up_e, w_down_e, g_final):
    B, S, D = x.shape
    depth = w_in.shape[0]
    cap = CAPACITY_FACTOR * S // N_EXPERTS
    x2 = x.reshape(B * S, D)

    bd = _block_diag(jnp.ones((2, HEAD_DIM, HEAD_DIM), F32)).astype(BF16)
    k_i = lax.broadcasted_iota(jnp.int32, (LANES, LANES), 0)
    l_i = lax.broadcasted_iota(jnp.int32, (LANES, LANES), 1)
    tri = jnp.concatenate([(k_i < l_i).astype(BF16), jnp.ones((LANES, LANES), BF16)], axis=1)

    for l in range(depth):
        gcq, gsq = _rope_tables(S, g_q[l], HEAD_DIM ** -0.5 * math.log2(math.e))
        gck, gsk = _rope_tables(S, g_k[l], 1.0)
        w_a = w_in[l, :, :A_W].astype(BF16)
        w_gl = w_in[l, :, A_W:].astype(BF16)
        gmix = g_mix[l][None, :]
        q, k, v, u, vn, p = _inproj(x2, gmix, w_a, gcq, gsq, gck, gsk, bd, g_sgu[l][None, :], S, 512)
        attn = _attention(q, k, v, B, S, 512)

        wsp = w_spatial[l].transpose(1, 0, 2).reshape(SGU_CHUNK, SGU_GROUPS * SGU_CHUNK).astype(BF16)
        bsp = jnp.repeat(b_spatial[l].T, SGU_HEAD, axis=1)
        wpool = _block_diag(w_pool[l]).astype(BF16)
        wr = jnp.pad(w_router[l], ((0, 0), (0, LANES - N_EXPERTS))).astype(BF16)
        x1, h2, lg = _merge(
            x2, gmix, w_gl, attn, u, vn, p, wsp, bsp, wpool, pool_scale[l][None, :],
            w_attn_o[l].astype(BF16), w_sgu_o[l].astype(BF16), w_pool_o[l].astype(BF16),
            w_out[l].astype(BF16), g_ffn[l][None, :], wr, S, 512)

        rank, aff, lists = _route(lg.reshape(B, S, LANES), tri, cap, math.gcd(B, 4))
        per_block = (B, N_EXPERTS, S // TOKEN_BLOCK, TOKEN_BLOCK)
        rank4, aff4 = rank.reshape(per_block), aff.reshape(per_block)
        xs = _gather(h2.reshape(B, S, D), rank4, lists, cap)
        y = _ffn(xs, w_gate_e, w_up_e, w_down_e, l, math.gcd(B, 4))
        x2 = _scatter(y, rank4, aff4, x1, g_final[None, :], lists, norm_out=(l == depth - 1))

    return x2.reshape(B, S, D)
```

```python
import functools
import math

import jax
import jax.numpy as jnp
from jax import lax
from jax.experimental import pallas as pl
from jax.experimental.pallas import tpu as pltpu

F32 = jnp.float32
BF16 = jnp.bfloat16

GRID_W = 64
N_HEADS = 8
N_KV_HEADS = 2
HEAD_DIM = 64
ROPE_THETA = 10000.0
SGU_GROUPS = 4
SGU_HEAD = 64
SGU_W = SGU_GROUPS * SGU_HEAD
SGU_CHUNK = 128
POOL_WINDOWS = (2, 4, 8, 16)
POOL_HEAD = 64
POOL_W = len(POOL_WINDOWS) * POOL_HEAD
POOL_HALO = 8
KEY_CHUNK = 256
QUERY_SUB = 512
MERGE_SUB = 256
TOKEN_BLOCK = 256
TILE_SHIFT = 4
SLOT_TILE = 1 << TILE_SHIFT
CHUNK_SHIFT = 6
CHUNK_TILES = 1 << CHUNK_SHIFT
NO_SLOT = -1024
Q_W = N_HEADS * HEAD_DIM
KV_W = N_KV_HEADS * HEAD_DIM
N_EXPERTS = 16
CAPACITY_FACTOR = 2
EPS = 1e-6

LANES = 128
F32_SUBLANES = 8
A_W = Q_W + 2 * KV_W + 2 * SGU_W + POOL_W
VMEM_LIMIT = 56 * 1024 * 1024


def _cparams(*sem):
    return pltpu.CompilerParams(dimension_semantics=sem, vmem_limit_bytes=VMEM_LIMIT)


def _dot(a, b):
    return jnp.dot(a, b, preferred_element_type=F32)


def _dot_nt(a, b):
    return lax.dot_general(a, b, (((1,), (1,)), ((), ())), preferred_element_type=F32)


def _rms(x):
    return x * lax.rsqrt(jnp.mean(x * x, axis=-1, keepdims=True) + EPS)


def _head_norm_rope(xq, gc, gs, bd):
    sq = xq * xq
    hi = sq.astype(BF16)
    lo = (sq - hi.astype(F32)).astype(BF16)
    ss = _dot(hi, bd) + _dot(lo, bd)
    r = lax.rsqrt(ss * (1.0 / HEAD_DIM) + EPS)
    lane = lax.broadcasted_iota(jnp.int32, xq.shape, 1)
    first_half = (lane % 32) < 16
    partner = jnp.where(first_half, pltpu.roll(xq, LANES - 16, 1), pltpu.roll(xq, 16, 1))
    return r * (xq * gc + partner * gs)


def _inproj_kernel(x_ref, gmix_ref, w_ref, gcq_ref, gsq_ref, gck_ref, gsk_ref, bd_ref, gsgu_ref,
                   q_ref, k_ref, v_ref, u_ref, vn_ref, p_ref):
    h = (_rms(x_ref[...]) * gmix_ref[...]).astype(BF16)
    acc = _dot(h, w_ref[...])
    bd = bd_ref[...]
    for c in range(Q_W // LANES):
        sl = slice(c * LANES, (c + 1) * LANES)
        q_ref[:, sl] = _head_norm_rope(acc[:, sl], gcq_ref[...], gsq_ref[...], bd).astype(BF16)
    o = Q_W
    k_ref[...] = _head_norm_rope(acc[:, o:o + KV_W], gck_ref[...], gsk_ref[...], bd).astype(BF16)
    o += KV_W
    v_ref[...] = acc[:, o:o + KV_W].astype(BF16)
    o += KV_W
    u_ref[...] = acc[:, o:o + SGU_W]
    o += SGU_W
    vn_ref[...] = (_rms(acc[:, o:o + SGU_W]) * gsgu_ref[...]).astype(BF16)
    o += SGU_W
    p_ref[...] = acc[:, o:o + POOL_W]


def _inproj(x2, gmix, w_a, gcq, gsq, gck, gsk, bd, gsgu, S, tm):
    T, D = x2.shape
    nseq = S // tm
    row = lambda i: (i, 0)
    const = lambda i: (0, 0)
    pos = lambda i: (i % nseq, 0)
    return pl.pallas_call(
        _inproj_kernel,
        grid=(T // tm,),
        in_specs=[
            pl.BlockSpec((tm, D), row),
            pl.BlockSpec((1, D), const),
            pl.BlockSpec((D, A_W), const),
            pl.BlockSpec((tm, LANES), pos),
            pl.BlockSpec((tm, LANES), pos),
            pl.BlockSpec((tm, LANES), pos),
            pl.BlockSpec((tm, LANES), pos),
            pl.BlockSpec((LANES, LANES), const),
            pl.BlockSpec((1, SGU_W), const),
        ],
        out_specs=[
            pl.BlockSpec((tm, Q_W), row),
            pl.BlockSpec((tm, KV_W), row),
            pl.BlockSpec((tm, KV_W), row),
            pl.BlockSpec((tm, SGU_W), row),
            pl.BlockSpec((tm, SGU_W), row),
            pl.BlockSpec((tm, POOL_W), row),
        ],
        out_shape=[
            jax.ShapeDtypeStruct((T, Q_W), BF16),
            jax.ShapeDtypeStruct((T, KV_W), BF16),
            jax.ShapeDtypeStruct((T, KV_W), BF16),
            jax.ShapeDtypeStruct((T, SGU_W), F32),
            jax.ShapeDtypeStruct((T, SGU_W), BF16),
            jax.ShapeDtypeStruct((T, POOL_W), F32),
        ],
        compiler_params=_cparams("parallel"),
        name="inproj",
    )(x2, gmix, w_a, gcq, gsq, gck, gsk, bd, gsgu)


def _attn_kernel(q_ref, k_ref, v_ref, o_ref, kz_ref, vzt_ref, s_ref):
    @pl.when(pl.program_id(1) == 0)
    def _():
        lane = lax.broadcasted_iota(jnp.int32, k_ref.shape, 1)
        low = lane < HEAD_DIM
        k = k_ref[...].astype(F32)
        kr = pltpu.roll(k, HEAD_DIM, 1)
        kz_ref[0] = jnp.where(low, k, 0.0).astype(BF16)
        kz_ref[1] = jnp.where(low, 0.0, kr).astype(BF16)
        kz_ref[2] = jnp.where(low, kr, 0.0).astype(BF16)
        kz_ref[3] = jnp.where(low, 0.0, k).astype(BF16)
        vt = v_ref[...].astype(F32).T
        vtr = pltpu.roll(vt, HEAD_DIM, 0)
        row = lax.broadcasted_iota(jnp.int32, vt.shape, 0)
        top = row < HEAD_DIM
        ones_lo = jnp.where(row == 0, 1.0, 0.0)
        ones_hi = jnp.where(row == HEAD_DIM, 1.0, 0.0)
        vzt_ref[0] = jnp.where(top, vt, ones_hi).astype(BF16)
        vzt_ref[1] = jnp.where(top, ones_lo, vtr).astype(BF16)
        vzt_ref[2] = jnp.where(top, vtr, ones_hi).astype(BF16)
        vzt_ref[3] = jnp.where(top, ones_lo, vt).astype(BF16)

    tq = s_ref.shape[2]
    S = k_ref.shape[0]
    nchunk = S // KEY_CHUNK
    sub = F32_SUBLANES
    slot0 = jnp.minimum(pl.program_id(1), 0)

    def unit_parts(unit):
        qt, head = divmod(unit, N_HEADS)
        pair, parity = divmod(head, 2)
        kvi = 2 * ((2 * pair) // (N_HEADS // N_KV_HEADS)) + parity
        return slice(qt * tq, (qt + 1) * tq), slice(pair * LANES, (pair + 1) * LANES), parity, kvi

    def scores(unit, c, mrun):
        qrows, lanes, _, kvi = unit_parts(unit)
        rows = slice(c * KEY_CHUNK, (c + 1) * KEY_CHUNK)
        s = _dot_nt(kz_ref[kvi, rows, :], q_ref[qrows, lanes])
        s_ref[slot0 + unit % 2, rows, :] = s
        m = jnp.max(s.reshape(KEY_CHUNK // sub, sub, tq), axis=0)
        return m if mrun is None else jnp.maximum(mrun, m)

    def weighted(unit, c, m8, acc):
        kvi = unit_parts(unit)[3]
        rows = slice(c * KEY_CHUNK, (c + 1) * KEY_CHUNK)
        s3 = s_ref[slot0 + unit % 2, rows, :].reshape(KEY_CHUNK // sub, sub, tq)
        e = jnp.exp2(s3 - m8[None]).reshape(KEY_CHUNK, tq).astype(BF16)
        d = _dot(vzt_ref[kvi, :, rows], e)
        return d if acc is None else acc + d

    nunit = (q_ref.shape[0] // tq) * N_HEADS
    top = lax.broadcasted_iota(jnp.int32, (LANES, tq), 0) < HEAD_DIM
    mrun = None
    for c in range(nchunk):
        mrun = scores(0, c, mrun)
    even = None
    for unit in range(nunit):
        m8 = jnp.broadcast_to(jnp.max(mrun, axis=0, keepdims=True), (sub, tq))
        mnext, acc = None, None
        for c in range(nchunk):
            if unit + 1 < nunit:
                mnext = scores(unit + 1, c, mnext)
            acc = weighted(unit, c, m8, acc)
        mrun = mnext
        qrows, lanes, parity, _ = unit_parts(unit)
        if parity == 0:
            even = acc * (1.0 / acc[HEAD_DIM:HEAD_DIM + 1])
        else:
            odd = acc * (1.0 / acc[0:1])
            o_ref[qrows, lanes] = jnp.where(top, even, odd).T.astype(BF16)


def _attention(q, k, v, B, S, tq):
    T = q.shape[0]
    nq = S // tq
    return pl.pallas_call(
        _attn_kernel,
        grid=(B, nq),
        in_specs=[
            pl.BlockSpec((tq, Q_W), lambda b, i: (b * nq + i, 0)),
            pl.BlockSpec((S, KV_W), lambda b, i: (b, 0)),
            pl.BlockSpec((S, KV_W), lambda b, i: (b, 0)),
        ],
        out_specs=pl.BlockSpec((tq, Q_W), lambda b, i: (b * nq + i, 0)),
        out_shape=jax.ShapeDtypeStruct((T, Q_W), BF16),
        scratch_shapes=[pltpu.VMEM((4, S, KV_W), BF16), pltpu.VMEM((4, KV_W, S), BF16),
                        pltpu.VMEM((2, S, QUERY_SUB), F32)],
        compiler_params=_cparams("parallel", "arbitrary"),
        name="attention",
    )(q, k, v)


def _merge_kernel(x_ref, gmix_ref, wgl_ref, attn_ref, u_ref, vn_ref, p_ref, pprev_ref, pnext_ref,
                  wsp_ref, bsp_ref, wpool_ref, pscale_ref, wao_ref, wso_ref, wpo_ref, wout_ref,
                  gffn_ref, wr_ref, x1_ref, h2_ref, lg_ref, *, S):
    ts = x_ref.shape[0]

    vn = vn_ref[...]
    lane_b = lax.broadcasted_iota(jnp.int32, (SGU_CHUNK, SGU_W), 1) // SGU_HEAD
    zs = []
    for c in range(ts // SGU_CHUNK):
        vc = vn[c * SGU_CHUNK:(c + 1) * SGU_CHUNK]
        stacked = jnp.concatenate(
            [jnp.where(lane_b == g, vc, jnp.zeros_like(vc)) for g in range(SGU_GROUPS)], axis=0)
        zs.append(_dot(wsp_ref[...], stacked) + bsp_ref[...])
    sgu = (u_ref[...] * jnp.concatenate(zs, axis=0)).astype(BF16)

    pos0 = (pl.program_id(0) % (S // ts)) * ts
    p = p_ref[...]
    prev = jnp.where(pos0 == 0, 0.0, pprev_ref[...])
    nxt = jnp.where(pos0 + ts == S, 0.0, pnext_ref[...])
    ext = jnp.concatenate([prev, p, nxt], axis=0)
    n = ts + 2 * POOL_HALO
    s2 = ext + pltpu.roll(ext, 1, 0)
    s4 = pltpu.roll(s2, 1, 0) + pltpu.roll(s2, n - 1, 0)
    s8 = pltpu.roll(s4, 2, 0) + pltpu.roll(s4, n - 2, 0)
    s16 = pltpu.roll(s8, 4, 0) + pltpu.roll(s8, n - 4, 0)
    mid = slice(POOL_HALO, POOL_HALO + ts)
    grp = lax.broadcasted_iota(jnp.int32, (ts, POOL_W), 1) // POOL_HEAD
    t = pos0 + lax.broadcasted_iota(jnp.int32, (ts, POOL_W), 0)
    half = jnp.left_shift(1, grp)
    cnt = jnp.minimum(t + half - 1, S - 1) - jnp.maximum(t - half, 0) + 1
    wsum = jnp.where(grp == 0, s2[mid], jnp.where(grp == 1, s4[mid],
                                                  jnp.where(grp == 2, s8[mid], s16[mid])))
    d = (wsum / cnt.astype(F32) - p).astype(BF16)
    pooled = (_dot(d, wpool_ref[...]) * pscale_ref[...]).astype(BF16)

    D = x_ref.shape[1]
    for r0 in range(0, ts, MERGE_SUB):
        rows = slice(r0, r0 + MERGE_SUB)
        x = x_ref[rows, :]
        h = (_rms(x) * gmix_ref[...]).astype(BF16)
        br_a = _dot(attn_ref[rows, :], wao_ref[...])
        br_b = _dot(sgu[rows], wso_ref[...])
        br_c = _dot(pooled[rows], wpo_ref[...])
        gl = _dot(h, wgl_ref[...])
        gate = lambda j: 0.5 * (1.0 + jnp.tanh(0.5 * gl[:, j * D:(j + 1) * D]))
        merged = (gate(0) * br_a + gate(1) * br_b + gate(2) * br_c).astype(BF16)
        x1 = x + _dot(merged, wout_ref[...])
        x1_ref[rows, :] = x1
        h2 = (_rms(x1) * gffn_ref[...]).astype(BF16)
        h2_ref[rows, :] = h2
        lg_ref[rows, :] = _dot(h2, wr_ref[...])


def _merge(x2, gmix, w_gl, attn, u, vn, p, wsp, bsp, wpool, pscale, wao, wso, wpo, wout, gffn, wr,
           S, ts):
    T, D = x2.shape
    row = lambda i: (i, 0)
    const = lambda i: (0, 0)
    hb = ts // POOL_HALO
    nhb = T // POOL_HALO
    full = lambda a: pl.BlockSpec(a.shape, const)
    return pl.pallas_call(
        functools.partial(_merge_kernel, S=S),
        grid=(T // ts,),
        in_specs=[
            pl.BlockSpec((ts, D), row),
            full(gmix),
            full(w_gl),
            pl.BlockSpec((ts, Q_W), row),
            pl.BlockSpec((ts, SGU_W), row),
            pl.BlockSpec((ts, SGU_W), row),
            pl.BlockSpec((ts, POOL_W), row),
            pl.BlockSpec((POOL_HALO, POOL_W), lambda i: (jnp.maximum(i * hb - 1, 0), 0)),
            pl.BlockSpec((POOL_HALO, POOL_W), lambda i: (jnp.minimum((i + 1) * hb, nhb - 1), 0)),
            full(wsp), full(bsp), full(wpool), full(pscale), full(wao), full(wso), full(wpo),
            full(wout), full(gffn), full(wr),
        ],
        out_specs=[
            pl.BlockSpec((ts, D), row),
            pl.BlockSpec((ts, D), row),
            pl.BlockSpec((ts, LANES), row),
        ],
        out_shape=[
            jax.ShapeDtypeStruct((T, D), F32),
            jax.ShapeDtypeStruct((T, D), BF16),
            jax.ShapeDtypeStruct((T, LANES), F32),
        ],
        compiler_params=_cparams("parallel"),
        name="merge",
    )(x2, gmix, w_gl, attn, u, vn, p, p, p, wsp, bsp, wpool, pscale, wao, wso, wpo, wout, gffn, wr)


def _excl_cumsum(m01, tri_ones):
    off = jnp.zeros((m01.shape[0], LANES), F32)
    pieces = []
    for j in range(m01.shape[1] // LANES):
        res = _dot(m01[:, j * LANES:(j + 1) * LANES].astype(BF16), tri_ones)
        pieces.append(res[:, :LANES] + off)
        off = off + res[:, LANES:]
    return jnp.concatenate(pieces, axis=1)


def _tile_lists(offs, nblk, list_len):
    E = offs.shape[0]
    lane = lax.broadcasted_iota(jnp.int32, offs.shape, 1)
    hi = pltpu.roll(offs, LANES - 1, 1)
    g_lo = jnp.floor(offs * (1.0 / SLOT_TILE))
    g_hi = jnp.floor((hi + (SLOT_TILE - 1)) * (1.0 / SLOT_TILE))
    cnt = jnp.where((hi > offs) & (lane < nblk), g_hi - g_lo, 0.0)
    below = (lax.broadcasted_iota(jnp.int32, (E, E), 1) < lax.broadcasted_iota(jnp.int32, (E, E), 0))
    base = _dot(jnp.where(below, 1.0, 0.0).astype(BF16), cnt.astype(BF16))
    end = base + cnt
    t = lax.broadcasted_iota(jnp.int32, (1, list_len), 1).astype(F32)
    row_e = lax.broadcasted_iota(jnp.int32, (E, list_len), 0).astype(F32)
    blocks = []
    for j in range(nblk):
        col = slice(j, j + 1)
        e_t = jnp.sum(jnp.where(end[:, col] <= t, 1.0, 0.0), axis=0, keepdims=True)
        shift = jnp.sum(jnp.where(row_e == e_t, g_lo[:, col] - base[:, col], 0.0), axis=0, keepdims=True)
        valid = e_t < E
        g_t = jnp.where(valid, t + shift, 0.0)
        total = jnp.broadcast_to(end[E - 1:E, col], (1, list_len))
        rows = [jnp.where(valid, e_t, 0.0), g_t, jnp.where(valid, g_t * SLOT_TILE, float(NO_SLOT)),
                total, jnp.zeros((4, list_len), F32)]
        blocks.append(jnp.concatenate(rows, axis=0).astype(jnp.int32)[None])
    return jnp.concatenate(blocks, axis=0)


def _route_kernel(lg_ref, tri_ref, rank_ref, aff_ref, lists_ref, *, cap):
    nb, S, _ = lg_ref.shape
    E = N_EXPERTS
    rows = []
    for b in range(nb):
        lt = lg_ref[b].T[:E]
        e = jnp.exp(lt - jnp.max(lt, axis=0, keepdims=True))
        rows.append(e / jnp.sum(e, axis=0, keepdims=True))
    aff = jnp.concatenate(rows, axis=0)
    bits = pltpu.bitcast(aff, jnp.int32)

    def refine(i, thr):
        cand = thr | jnp.left_shift(jnp.int32(1), 30 - i)
        cnt = jnp.sum(jnp.where(bits >= cand, 1.0, 0.0), axis=1, keepdims=True)
        return jnp.where(cnt >= cap, cand, thr)

    thr = lax.fori_loop(0, 31, refine, jnp.zeros((nb * E, 1), jnp.int32))
    gt = jnp.where(bits > thr, 1.0, 0.0)
    eq = jnp.where(bits == thr, 1.0, 0.0)
    need = cap - jnp.sum(gt, axis=1, keepdims=True)
    tri = tri_ref[...]
    sel = gt + eq * jnp.where(_excl_cumsum(eq, tri) < need, 1.0, 0.0)
    before = _excl_cumsum(sel, tri)
    rank = jnp.where(sel > 0.5, before, -1.0)
    nblk = S // TOKEN_BLOCK
    lane = lax.broadcasted_iota(jnp.int32, (nb * E, LANES), 1)
    offs = jnp.where(lane == nblk, float(cap), 0.0)
    for j in range(nblk):
        offs = jnp.where(lane == j, before[:, j * TOKEN_BLOCK:j * TOKEN_BLOCK + 1], offs)
    list_len = lists_ref.shape[-1]
    for b in range(nb):
        rank_ref[b] = rank[b * E:(b + 1) * E]
        aff_ref[b] = aff[b * E:(b + 1) * E]
        lists_ref[b] = _tile_lists(offs[b * E:(b + 1) * E], nblk, list_len)


def _list_len(cap):
    worst = N_EXPERTS * (cap // SLOT_TILE + 2)
    return -(-worst // LANES) * LANES


def _route(lg3, tri, cap, nb):
    B, S, _ = lg3.shape
    nblk = S // TOKEN_BLOCK
    blk = lambda b: (b, 0, 0)
    lists_shape = (nblk, 8, _list_len(cap))
    return pl.pallas_call(
        functools.partial(_route_kernel, cap=cap),
        grid=(B // nb,),
        in_specs=[pl.BlockSpec((nb, S, LANES), blk), pl.BlockSpec(tri.shape, lambda b: (0, 0))],
        out_specs=[
            pl.BlockSpec((nb, N_EXPERTS, S), blk),
            pl.BlockSpec((nb, N_EXPERTS, S), blk),
            pl.BlockSpec((nb,) + lists_shape, lambda b: (b, 0, 0, 0)),
        ],
        out_shape=[
            jax.ShapeDtypeStruct((B, N_EXPERTS, S), F32),
            jax.ShapeDtypeStruct((B, N_EXPERTS, S), F32),
            jax.ShapeDtypeStruct((B,) + lists_shape, jnp.int32),
        ],
        compiler_params=_cparams("parallel"),
        name="route",
    )(lg3, tri)


def _chunk_tiles(list_ref, c):
    out = []
    for i in range(CHUNK_TILES):
        t = c * CHUNK_TILES + i
        out.append((list_ref[0, t], list_ref[1, t], list_ref[2, t]))
    return out


def _num_chunks(list_ref):
    return jnp.right_shift(list_ref[3, 0] + CHUNK_TILES - 1, CHUNK_SHIFT)


def _tile_rows(g_t):
    return pl.ds(pl.multiple_of(g_t * SLOT_TILE, SLOT_TILE), SLOT_TILE)


def _onehot(entries, rank_ref, j, weight_ref=None):
    tb = rank_ref.shape[-1]
    sub = lax.broadcasted_iota(jnp.int32, (SLOT_TILE, tb), 0)
    tiles = []
    for e_t, _, slot0 in entries:
        hit = rank_ref[e_t, pl.ds(j, 1), :] == (slot0 + sub).astype(F32)
        val = 1.0 if weight_ref is None else weight_ref[e_t, pl.ds(j, 1), :]
        tiles.append(jnp.where(hit, val, 0.0).astype(BF16))
    return jnp.concatenate(tiles, axis=0)


def _gather_kernel(list_ref, h2_ref, rank_ref, xs_ref):
    j = pl.program_id(1)

    @pl.when(j == 0)
    def _():
        xs_ref[...] = jnp.zeros_like(xs_ref)

    def chunk(c, carry):
        entries = _chunk_tiles(list_ref, c)
        rows = _dot(_onehot(entries, rank_ref, j), h2_ref[...]).astype(BF16)
        for i, (e_t, g_t, _) in enumerate(entries):
            xs_ref[e_t, _tile_rows(g_t), :] += rows[i * SLOT_TILE:(i + 1) * SLOT_TILE, :]
        return carry

    lax.fori_loop(0, _num_chunks(list_ref), chunk, 0)


def _list_spec(lists):
    return pl.BlockSpec((None, None) + lists.shape[2:], lambda b, j: (b, j, 0, 0),
                        memory_space=pltpu.SMEM)


def _gather(h2, rank4, lists, cap):
    B, S, D = h2.shape
    _, E, nblk, tb = rank4.shape
    return pl.pallas_call(
        _gather_kernel,
        grid=(B, nblk),
        in_specs=[
            _list_spec(lists),
            pl.BlockSpec((None, tb, D), lambda b, j: (b, j, 0)),
            pl.BlockSpec((None, E, nblk, tb), lambda b, j: (b, 0, 0, 0)),
        ],
        out_specs=pl.BlockSpec((None, E, cap, D), lambda b, j: (b, 0, 0, 0)),
        out_shape=jax.ShapeDtypeStruct((B, E, cap, D), BF16),
        compiler_params=_cparams("arbitrary", "arbitrary"),
        name="gather",
    )(lists, h2, rank4)


def _ffn_kernel(xs_ref, wg_ref, wu_ref, wd_ref, y_ref, wg_b, wu_b, wd_b):
    @pl.when(pl.program_id(1) == 0)
    def _():
        wg_b[...] = wg_ref[...].astype(BF16)
        wu_b[...] = wu_ref[...].astype(BF16)
        wd_b[...] = wd_ref[...].astype(BF16)

    nb, cap, D = xs_ref.shape
    xs = xs_ref[...].reshape(nb * cap, D)
    a = _dot(xs, wg_b[...])
    b = _dot(xs, wu_b[...])
    hidden = (a * jax.nn.sigmoid(a) * b).astype(BF16)
    y_ref[...] = _dot(hidden, wd_b[...]).astype(BF16).reshape(nb, cap, D)


def _ffn(xs, wg, wu, wd, layer, nb):
    B, E, cap, D = xs.shape
    F = wg.shape[3]
    blk = pl.BlockSpec((nb, None, cap, D), lambda e, b: (b, e, 0, 0))
    return pl.pallas_call(
        _ffn_kernel,
        grid=(E, B // nb),
        in_specs=[
            blk,
            pl.BlockSpec((None, None, D, F), lambda e, b: (layer, e, 0, 0)),
            pl.BlockSpec((None, None, D, F), lambda e, b: (layer, e, 0, 0)),
            pl.BlockSpec((None, None, F, D), lambda e, b: (layer, e, 0, 0)),
        ],
        out_specs=blk,
        out_shape=jax.ShapeDtypeStruct((B, E, cap, D), BF16),
        scratch_shapes=[pltpu.VMEM((D, F), BF16), pltpu.VMEM((D, F), BF16), pltpu.VMEM((F, D), BF16)],
        compiler_params=_cparams("arbitrary", "arbitrary"),
        name="ffn",
    )(xs, wg, wu, wd)


def _scatter_kernel(list_ref, y_ref, rank_ref, aff_ref, x1_ref, gout_ref, out_ref, *, norm_out):
    j = pl.program_id(1)
    out_ref[...] = x1_ref[...]

    def chunk(c, carry):
        entries = _chunk_tiles(list_ref, c)
        weights = _onehot(entries, rank_ref, j, aff_ref)
        ys = jnp.concatenate([y_ref[e_t, _tile_rows(g_t), :] for e_t, g_t, _ in entries], axis=0)
        out_ref[...] += lax.dot_general(weights, ys, (((0,), (0,)), ((), ())),
                                        preferred_element_type=F32)
        return carry

    lax.fori_loop(0, _num_chunks(list_ref), chunk, 0)
    if norm_out:
        out_ref[...] = _rms(out_ref[...]) * gout_ref[...]


def _scatter(y, rank4, aff4, x1, gout, lists, norm_out):
    B, E, cap, D = y.shape
    _, _, nblk, tb = rank4.shape
    per_seq = pl.BlockSpec((None, E, nblk, tb), lambda b, j: (b, 0, 0, 0))
    return pl.pallas_call(
        functools.partial(_scatter_kernel, norm_out=norm_out),
        grid=(B, nblk),
        in_specs=[
            _list_spec(lists),
            pl.BlockSpec((None, E, cap, D), lambda b, j: (b, 0, 0, 0)),
            per_seq,
            per_seq,
            pl.BlockSpec((tb, D), lambda b, j: (b * nblk + j, 0)),
            pl.BlockSpec((1, D), lambda b, j: (0, 0)),
        ],
        out_specs=pl.BlockSpec((tb, D), lambda b, j: (b * nblk + j, 0)),
        out_shape=jax.ShapeDtypeStruct((B * nblk * tb, D), F32),
        compiler_params=_cparams("arbitrary", "arbitrary"),
        name="scatter",
    )(lists, y, rank4, aff4, x1, gout)


def _rope_tables(S, g, scale):
    rows = S // GRID_W
    row = jnp.broadcast_to(jnp.arange(rows, dtype=F32)[:, None], (rows, GRID_W)).reshape(-1)
    col = jnp.broadcast_to(jnp.arange(GRID_W, dtype=F32)[None, :], (rows, GRID_W)).reshape(-1)
    part = HEAD_DIM // 2
    freqs = ROPE_THETA ** (-jnp.arange(0, part, 2, dtype=F32) / part)
    ang = jnp.stack([row[:, None] * freqs, col[:, None] * freqs], axis=1)
    cos, sin = jnp.cos(ang), jnp.sin(ang)
    c64 = jnp.concatenate([cos[:, 0], cos[:, 0], cos[:, 1], cos[:, 1]], axis=-1)
    s64 = jnp.concatenate([-sin[:, 0], sin[:, 0], -sin[:, 1], sin[:, 1]], axis=-1)
    g_swapped = g.reshape(2, 2, HEAD_DIM // 4)[:, ::-1, :].reshape(HEAD_DIM)
    gc = jnp.tile(c64 * g[None, :], (1, 2)) * scale
    gs = jnp.tile(s64 * g_swapped[None, :], (1, 2)) * scale
    return gc, gs


def _block_diag(blocks):
    n = blocks.shape[0]
    m = blocks.shape[1]
    eye = jnp.eye(n, dtype=blocks.dtype)
    return (eye[:, None, :, None] * blocks[:, :, None, :]).reshape(n * m, n * blocks.shape[2])


def kernel(x, g_mix, w_in, g_q, g_k, g_sgu, w_spatial, b_spatial, w_pool, pool_scale, w_attn_o,
           w_sgu_o, w_pool_o, w_out, g_ffn, w_router, w_gate_e, w_up_e, w_down_e, g_final):
    B, S, D = x.shape
    depth = w_in.shape[0]
    cap = CAPACITY_FACTOR * S // N_EXPERTS
    x2 = x.reshape(B * S, D)

    bd = _block_diag(jnp.ones((2, HEAD_DIM, HEAD_DIM), F32)).astype(BF16)
    k_i = lax.broadcasted_iota(jnp.int32, (LANES, LANES), 0)
    l_i = lax.broadcasted_iota(jnp.int32, (LANES, LANES), 1)
    tri = jnp.concatenate([(k_i < l_i).astype(BF16), jnp.ones((LANES, LANES), BF16)], axis=1)

    for l in range(depth):
        gcq, gsq = _rope_tables(S, g_q[l], HEAD_DIM ** -0.5 * math.log2(math.e))
        gck, gsk = _rope_tables(S, g_k[l], 1.0)
        w_a = w_in[l, :, :A_W].astype(BF16)
        w_gl = w_in[l, :, A_W:].astype(BF16)
        gmix = g_mix[l][None, :]
        q, k, v, u, vn, p = _inproj(x2, gmix, w_a, gcq, gsq, gck, gsk, bd, g_sgu[l][None, :], S, 512)
        attn = _attention(q, k, v, B, S, 2 * QUERY_SUB)

        wsp = w_spatial[l].transpose(1, 0, 2).reshape(SGU_CHUNK, SGU_GROUPS * SGU_CHUNK).astype(BF16)
        bsp = jnp.repeat(b_spatial[l].T, SGU_HEAD, axis=1)
        wpool = _block_diag(w_pool[l]).astype(BF16)
        wr = jnp.pad(w_router[l], ((0, 0), (0, LANES - N_EXPERTS))).astype(BF16)
        x1, h2, lg = _merge(
            x2, gmix, w_gl, attn, u, vn, p, wsp, bsp, wpool, pool_scale[l][None, :],
            w_attn_o[l].astype(BF16), w_sgu_o[l].astype(BF16), w_pool_o[l].astype(BF16),
            w_out[l].astype(BF16), g_ffn[l][None, :], wr, S, 512)

        rank, aff, lists = _route(lg.reshape(B, S, LANES), tri, cap, math.gcd(B, 4))
        per_block = (B, N_EXPERTS, S // TOKEN_BLOCK, TOKEN_BLOCK)
        rank4, aff4 = rank.reshape(per_block), aff.reshape(per_block)
        xs = _gather(h2.reshape(B, S, D), rank4, lists, cap)
        y = _ffn(xs, w_gate_e, w_up_e, w_down_e, l, math.gcd(B, 4))
        x2 = _scatter(y, rank4, aff4, x1, g_final[None, :], lists, norm_out=(l == depth - 1))

    return x2.reshape(B, S, D)
```

```python
import functools
import math

import jax
import jax.numpy as jnp
from jax import lax
from jax.experimental import pallas as pl
from jax.experimental.pallas import tpu as pltpu

F32 = jnp.float32
BF16 = jnp.bfloat16

GRID_W = 64
N_HEADS = 8
N_KV_HEADS = 2
HEAD_DIM = 64
ROPE_THETA = 10000.0
SGU_GROUPS = 4
SGU_HEAD = 64
SGU_W = SGU_GROUPS * SGU_HEAD
SGU_CHUNK = 128
POOL_WINDOWS = (2, 4, 8, 16)
POOL_HEAD = 64
POOL_W = len(POOL_WINDOWS) * POOL_HEAD
POOL_HALO = 8
KEY_CHUNK = 256
QUERY_SUB = 512
TOKEN_BLOCK = 256
BLOCKS_PER_STEP = 4
TILE_SHIFT = 4
SLOT_TILE = 1 << TILE_SHIFT
CHUNK_SHIFT = 6
CHUNK_TILES = 1 << CHUNK_SHIFT
NO_SLOT = -1024
Q_W = N_HEADS * HEAD_DIM
KV_W = N_KV_HEADS * HEAD_DIM
N_EXPERTS = 16
CAPACITY_FACTOR = 2
EPS = 1e-6

LANES = 128
F32_SUBLANES = 8
A_W = Q_W + 2 * KV_W + 2 * SGU_W + POOL_W
VMEM_LIMIT = 56 * 1024 * 1024


def _cparams(*sem):
    return pltpu.CompilerParams(dimension_semantics=sem, vmem_limit_bytes=VMEM_LIMIT)


def _dot(a, b):
    return jnp.dot(a, b, preferred_element_type=F32)


def _dot_nt(a, b):
    return lax.dot_general(a, b, (((1,), (1,)), ((), ())), preferred_element_type=F32)


def _rms(x):
    return x * lax.rsqrt(jnp.mean(x * x, axis=-1, keepdims=True) + EPS)


def _head_norm_rope(xq, gc, gs, bd):
    sq = xq * xq
    hi = sq.astype(BF16)
    lo = (sq - hi.astype(F32)).astype(BF16)
    ss = _dot(hi, bd) + _dot(lo, bd)
    r = lax.rsqrt(ss * (1.0 / HEAD_DIM) + EPS)
    lane = lax.broadcasted_iota(jnp.int32, xq.shape, 1)
    first_half = (lane % 32) < 16
    partner = jnp.where(first_half, pltpu.roll(xq, LANES - 16, 1), pltpu.roll(xq, 16, 1))
    return r * (xq * gc + partner * gs)


def _inproj_kernel(x_ref, gmix_ref, w_ref, gcq_ref, gsq_ref, gck_ref, gsk_ref, bd_ref, gsgu_ref,
                   q_ref, k_ref, v_ref, u_ref, vn_ref, p_ref):
    h = (_rms(x_ref[...]) * gmix_ref[...]).astype(BF16)
    acc = _dot(h, w_ref[...])
    bd = bd_ref[...]
    for c in range(Q_W // LANES):
        sl = slice(c * LANES, (c + 1) * LANES)
        q_ref[:, sl] = _head_norm_rope(acc[:, sl], gcq_ref[...], gsq_ref[...], bd).astype(BF16)
    o = Q_W
    k_ref[...] = _head_norm_rope(acc[:, o:o + KV_W], gck_ref[...], gsk_ref[...], bd).astype(BF16)
    o += KV_W
    v_ref[...] = acc[:, o:o + KV_W].astype(BF16)
    o += KV_W
    u_ref[...] = acc[:, o:o + SGU_W]
    o += SGU_W
    vn_ref[...] = (_rms(acc[:, o:o + SGU_W]) * gsgu_ref[...]).astype(BF16)
    o += SGU_W
    p_ref[...] = acc[:, o:o + POOL_W]


def _inproj(x2, gmix, w_a, gcq, gsq, gck, gsk, bd, gsgu, S, tm):
    T, D = x2.shape
    nseq = S // tm
    row = lambda i: (i, 0)
    const = lambda i: (0, 0)
    pos = lambda i: (i % nseq, 0)
    return pl.pallas_call(
        _inproj_kernel,
        grid=(T // tm,),
        in_specs=[
            pl.BlockSpec((tm, D), row),
            pl.BlockSpec((1, D), const),
            pl.BlockSpec((D, A_W), const),
            pl.BlockSpec((tm, LANES), pos),
            pl.BlockSpec((tm, LANES), pos),
            pl.BlockSpec((tm, LANES), pos),
            pl.BlockSpec((tm, LANES), pos),
            pl.BlockSpec((LANES, LANES), const),
            pl.BlockSpec((1, SGU_W), const),
        ],
        out_specs=[
            pl.BlockSpec((tm, Q_W), row),
            pl.BlockSpec((tm, KV_W), row),
            pl.BlockSpec((tm, KV_W), row),
            pl.BlockSpec((tm, SGU_W), row),
            pl.BlockSpec((tm, SGU_W), row),
            pl.BlockSpec((tm, POOL_W), row),
        ],
        out_shape=[
            jax.ShapeDtypeStruct((T, Q_W), BF16),
            jax.ShapeDtypeStruct((T, KV_W), BF16),
            jax.ShapeDtypeStruct((T, KV_W), BF16),
            jax.ShapeDtypeStruct((T, SGU_W), F32),
            jax.ShapeDtypeStruct((T, SGU_W), BF16),
            jax.ShapeDtypeStruct((T, POOL_W), F32),
        ],
        compiler_params=_cparams("parallel"),
        name="inproj",
    )(x2, gmix, w_a, gcq, gsq, gck, gsk, bd, gsgu)


def _attn_kernel(q_ref, k_ref, v_ref, o_ref, kz_ref, vzt_ref, s_ref):
    @pl.when(pl.program_id(1) == 0)
    def _():
        lane = lax.broadcasted_iota(jnp.int32, k_ref.shape, 1)
        low = lane < HEAD_DIM
        k = k_ref[...].astype(F32)
        kr = pltpu.roll(k, HEAD_DIM, 1)
        kz_ref[0] = jnp.where(low, k, 0.0).astype(BF16)
        kz_ref[1] = jnp.where(low, 0.0, kr).astype(BF16)
        kz_ref[2] = jnp.where(low, kr, 0.0).astype(BF16)
        kz_ref[3] = jnp.where(low, 0.0, k).astype(BF16)
        vt = v_ref[...].astype(F32).T
        vtr = pltpu.roll(vt, HEAD_DIM, 0)
        row = lax.broadcasted_iota(jnp.int32, vt.shape, 0)
        top = row < HEAD_DIM
        ones_lo = jnp.where(row == 0, 1.0, 0.0)
        ones_hi = jnp.where(row == HEAD_DIM, 1.0, 0.0)
        vzt_ref[0] = jnp.where(top, vt, ones_hi).astype(BF16)
        vzt_ref[1] = jnp.where(top, ones_lo, vtr).astype(BF16)
        vzt_ref[2] = jnp.where(top, vtr, ones_hi).astype(BF16)
        vzt_ref[3] = jnp.where(top, ones_lo, vt).astype(BF16)

    tq = s_ref.shape[2]
    S = k_ref.shape[0]
    nchunk = S // KEY_CHUNK
    sub = F32_SUBLANES
    slot0 = jnp.minimum(pl.program_id(1), 0)

    def unit_parts(unit):
        qt, head = divmod(unit, N_HEADS)
        pair, parity = divmod(head, 2)
        kvi = 2 * ((2 * pair) // (N_HEADS // N_KV_HEADS)) + parity
        return slice(qt * tq, (qt + 1) * tq), slice(pair * LANES, (pair + 1) * LANES), parity, kvi

    def scores(unit, c, mrun):
        qrows, lanes, _, kvi = unit_parts(unit)
        rows = slice(c * KEY_CHUNK, (c + 1) * KEY_CHUNK)
        s = _dot_nt(kz_ref[kvi, rows, :], q_ref[qrows, lanes])
        s_ref[slot0 + unit % 2, rows, :] = s
        m = jnp.max(s.reshape(KEY_CHUNK // sub, sub, tq), axis=0)
        return m if mrun is None else jnp.maximum(mrun, m)

    def weighted(unit, c, m8, acc):
        kvi = unit_parts(unit)[3]
        rows = slice(c * KEY_CHUNK, (c + 1) * KEY_CHUNK)
        s3 = s_ref[slot0 + unit % 2, rows, :].reshape(KEY_CHUNK // sub, sub, tq)
        e = jnp.exp2(s3 - m8[None]).reshape(KEY_CHUNK, tq).astype(BF16)
        d = _dot(vzt_ref[kvi, :, rows], e)
        return d if acc is None else acc + d

    nunit = (q_ref.shape[0] // tq) * N_HEADS
    top = lax.broadcasted_iota(jnp.int32, (LANES, tq), 0) < HEAD_DIM
    mrun = None
    for c in range(nchunk):
        mrun = scores(0, c, mrun)
    even = None
    for unit in range(nunit):
        m8 = jnp.broadcast_to(jnp.max(mrun, axis=0, keepdims=True), (sub, tq))
        mnext, acc = None, None
        for c in range(nchunk):
            if unit + 1 < nunit:
                mnext = scores(unit + 1, c, mnext)
            acc = weighted(unit, c, m8, acc)
        mrun = mnext
        qrows, lanes, parity, _ = unit_parts(unit)
        if parity == 0:
            even = acc * (1.0 / acc[HEAD_DIM:HEAD_DIM + 1])
        else:
            odd = acc * (1.0 / acc[0:1])
            o_ref[qrows, lanes] = jnp.where(top, even, odd).T.astype(BF16)


def _attention(q, k, v, B, S, tq):
    T = q.shape[0]
    nq = S // tq
    return pl.pallas_call(
        _attn_kernel,
        grid=(B, nq),
        in_specs=[
            pl.BlockSpec((tq, Q_W), lambda b, i: (b * nq + i, 0)),
            pl.BlockSpec((S, KV_W), lambda b, i: (b, 0)),
            pl.BlockSpec((S, KV_W), lambda b, i: (b, 0)),
        ],
        out_specs=pl.BlockSpec((tq, Q_W), lambda b, i: (b * nq + i, 0)),
        out_shape=jax.ShapeDtypeStruct((T, Q_W), BF16),
        scratch_shapes=[pltpu.VMEM((4, S, KV_W), BF16), pltpu.VMEM((4, KV_W, S), BF16),
                        pltpu.VMEM((2, S, QUERY_SUB), F32)],
        compiler_params=_cparams("parallel", "arbitrary"),
        name="attention",
    )(q, k, v)


def _merge_kernel(x_ref, gmix_ref, wgl_ref, attn_ref, u_ref, vn_ref, p_ref, pprev_ref, pnext_ref,
                  wsp_ref, bsp_ref, wpool_ref, pscale_ref, wao_ref, wso_ref, wpo_ref, wout_ref,
                  gffn_ref, wr_ref, x1_ref, h2_ref, lg_ref, *, S):
    ts = x_ref.shape[0]

    vn = vn_ref[...]
    lane_b = lax.broadcasted_iota(jnp.int32, (SGU_CHUNK, SGU_W), 1) // SGU_HEAD
    zs = []
    for c in range(ts // SGU_CHUNK):
        vc = vn[c * SGU_CHUNK:(c + 1) * SGU_CHUNK]
        stacked = jnp.concatenate(
            [jnp.where(lane_b == g, vc, jnp.zeros_like(vc)) for g in range(SGU_GROUPS)], axis=0)
        zs.append(_dot(wsp_ref[...], stacked) + bsp_ref[...])
    sgu = (u_ref[...] * jnp.concatenate(zs, axis=0)).astype(BF16)

    pos0 = (pl.program_id(0) % (S // ts)) * ts
    p = p_ref[...]
    prev = jnp.where(pos0 == 0, 0.0, pprev_ref[...])
    nxt = jnp.where(pos0 + ts == S, 0.0, pnext_ref[...])
    ext = jnp.concatenate([prev, p, nxt], axis=0)
    n = ts + 2 * POOL_HALO
    s2 = ext + pltpu.roll(ext, 1, 0)
    s4 = pltpu.roll(s2, 1, 0) + pltpu.roll(s2, n - 1, 0)
    s8 = pltpu.roll(s4, 2, 0) + pltpu.roll(s4, n - 2, 0)
    s16 = pltpu.roll(s8, 4, 0) + pltpu.roll(s8, n - 4, 0)
    mid = slice(POOL_HALO, POOL_HALO + ts)
    grp = lax.broadcasted_iota(jnp.int32, (ts, POOL_W), 1) // POOL_HEAD
    t = pos0 + lax.broadcasted_iota(jnp.int32, (ts, POOL_W), 0)
    half = jnp.left_shift(1, grp)
    cnt = jnp.minimum(t + half - 1, S - 1) - jnp.maximum(t - half, 0) + 1
    wsum = jnp.where(grp == 0, s2[mid], jnp.where(grp == 1, s4[mid],
                                                  jnp.where(grp == 2, s8[mid], s16[mid])))
    d = (wsum / cnt.astype(F32) - p).astype(BF16)
    pooled = (_dot(d, wpool_ref[...]) * pscale_ref[...]).astype(BF16)

    x = x_ref[...]
    D = x.shape[1]
    h = (_rms(x) * gmix_ref[...]).astype(BF16)
    br_a = _dot(attn_ref[...], wao_ref[...])
    br_b = _dot(sgu, wso_ref[...])
    br_c = _dot(pooled, wpo_ref[...])
    gl = _dot(h, wgl_ref[...])
    gate = lambda j: 0.5 * (1.0 + jnp.tanh(0.5 * gl[:, j * D:(j + 1) * D]))
    merged = (gate(0) * br_a + gate(1) * br_b + gate(2) * br_c).astype(BF16)
    x1 = x + _dot(merged, wout_ref[...])
    x1_ref[...] = x1
    h2 = (_rms(x1) * gffn_ref[...]).astype(BF16)
    h2_ref[...] = h2
    lg_ref[...] = _dot(h2, wr_ref[...])


def _merge(x2, gmix, w_gl, attn, u, vn, p, wsp, bsp, wpool, pscale, wao, wso, wpo, wout, gffn, wr,
           S, ts):
    T, D = x2.shape
    row = lambda i: (i, 0)
    const = lambda i: (0, 0)
    hb = ts // POOL_HALO
    nhb = T // POOL_HALO
    full = lambda a: pl.BlockSpec(a.shape, const)
    return pl.pallas_call(
        functools.partial(_merge_kernel, S=S),
        grid=(T // ts,),
        in_specs=[
            pl.BlockSpec((ts, D), row),
            full(gmix),
            full(w_gl),
            pl.BlockSpec((ts, Q_W), row),
            pl.BlockSpec((ts, SGU_W), row),
            pl.BlockSpec((ts, SGU_W), row),
            pl.BlockSpec((ts, POOL_W), row),
            pl.BlockSpec((POOL_HALO, POOL_W), lambda i: (jnp.maximum(i * hb - 1, 0), 0)),
            pl.BlockSpec((POOL_HALO, POOL_W), lambda i: (jnp.minimum((i + 1) * hb, nhb - 1), 0)),
            full(wsp), full(bsp), full(wpool), full(pscale), full(wao), full(wso), full(wpo),
            full(wout), full(gffn), full(wr),
        ],
        out_specs=[
            pl.BlockSpec((ts, D), row),
            pl.BlockSpec((ts, D), row),
            pl.BlockSpec((ts, LANES), row),
        ],
        out_shape=[
            jax.ShapeDtypeStruct((T, D), F32),
            jax.ShapeDtypeStruct((T, D), BF16),
            jax.ShapeDtypeStruct((T, LANES), F32),
        ],
        compiler_params=_cparams("parallel"),
        name="merge",
    )(x2, gmix, w_gl, attn, u, vn, p, p, p, wsp, bsp, wpool, pscale, wao, wso, wpo, wout, gffn, wr)


def _excl_cumsum(m01, tri_ones):
    off = jnp.zeros((m01.shape[0], LANES), F32)
    pieces = []
    for j in range(m01.shape[1] // LANES):
        res = _dot(m01[:, j * LANES:(j + 1) * LANES].astype(BF16), tri_ones)
        pieces.append(res[:, :LANES] + off)
        off = off + res[:, LANES:]
    return jnp.concatenate(pieces, axis=1)


def _tile_lists(offs, nblk, list_len):
    E = offs.shape[0]
    lane = lax.broadcasted_iota(jnp.int32, offs.shape, 1)
    hi = pltpu.roll(offs, LANES - 1, 1)
    g_lo = jnp.floor(offs * (1.0 / SLOT_TILE))
    g_hi = jnp.floor((hi + (SLOT_TILE - 1)) * (1.0 / SLOT_TILE))
    cnt = jnp.where((hi > offs) & (lane < nblk), g_hi - g_lo, 0.0)
    below = (lax.broadcasted_iota(jnp.int32, (E, E), 1) < lax.broadcasted_iota(jnp.int32, (E, E), 0))
    base = _dot(jnp.where(below, 1.0, 0.0).astype(BF16), cnt.astype(BF16))
    end = base + cnt
    t = lax.broadcasted_iota(jnp.int32, (1, list_len), 1).astype(F32)
    row_e = lax.broadcasted_iota(jnp.int32, (E, list_len), 0).astype(F32)
    blocks = []
    for j in range(nblk):
        col = slice(j, j + 1)
        e_t = jnp.sum(jnp.where(end[:, col] <= t, 1.0, 0.0), axis=0, keepdims=True)
        shift = jnp.sum(jnp.where(row_e == e_t, g_lo[:, col] - base[:, col], 0.0), axis=0, keepdims=True)
        valid = e_t < E
        g_t = jnp.where(valid, t + shift, 0.0)
        total = jnp.broadcast_to(end[E - 1:E, col], (1, list_len))
        rows = [jnp.where(valid, e_t, 0.0), g_t, jnp.where(valid, g_t * SLOT_TILE, float(NO_SLOT)),
                total, jnp.zeros((4, list_len), F32)]
        blocks.append(jnp.concatenate(rows, axis=0).astype(jnp.int32)[None])
    return jnp.concatenate(blocks, axis=0)


def _route_kernel(lg_ref, tri_ref, rank_ref, aff_ref, lists_ref, *, cap):
    nb, S, _ = lg_ref.shape
    E = N_EXPERTS
    rows = []
    for b in range(nb):
        lt = lg_ref[b].T[:E]
        e = jnp.exp(lt - jnp.max(lt, axis=0, keepdims=True))
        rows.append(e / jnp.sum(e, axis=0, keepdims=True))
    aff = jnp.concatenate(rows, axis=0)
    bits = pltpu.bitcast(aff, jnp.int32)

    def refine(i, thr):
        cand = thr | jnp.left_shift(jnp.int32(1), 30 - i)
        cnt = jnp.sum(jnp.where(bits >= cand, 1.0, 0.0), axis=1, keepdims=True)
        return jnp.where(cnt >= cap, cand, thr)

    thr = lax.fori_loop(0, 31, refine, jnp.zeros((nb * E, 1), jnp.int32))
    gt = jnp.where(bits > thr, 1.0, 0.0)
    eq = jnp.where(bits == thr, 1.0, 0.0)
    need = cap - jnp.sum(gt, axis=1, keepdims=True)
    tri = tri_ref[...]
    sel = gt + eq * jnp.where(_excl_cumsum(eq, tri) < need, 1.0, 0.0)
    before = _excl_cumsum(sel, tri)
    rank = jnp.where(sel > 0.5, before, -1.0)
    nblk = S // TOKEN_BLOCK
    lane = lax.broadcasted_iota(jnp.int32, (nb * E, LANES), 1)
    offs = jnp.where(lane == nblk, float(cap), 0.0)
    for j in range(nblk):
        offs = jnp.where(lane == j, before[:, j * TOKEN_BLOCK:j * TOKEN_BLOCK + 1], offs)
    list_len = lists_ref.shape[-1]
    for b in range(nb):
        rank_ref[b] = rank[b * E:(b + 1) * E]
        aff_ref[b] = aff[b * E:(b + 1) * E]
        lists_ref[b] = _tile_lists(offs[b * E:(b + 1) * E], nblk, list_len)


def _list_len(cap):
    worst = N_EXPERTS * (cap // SLOT_TILE + 2)
    return -(-worst // LANES) * LANES


def _route(lg3, tri, cap, nb):
    B, S, _ = lg3.shape
    nblk = S // TOKEN_BLOCK
    blk = lambda b: (b, 0, 0)
    lists_shape = (nblk, 8, _list_len(cap))
    return pl.pallas_call(
        functools.partial(_route_kernel, cap=cap),
        grid=(B // nb,),
        in_specs=[pl.BlockSpec((nb, S, LANES), blk), pl.BlockSpec(tri.shape, lambda b: (0, 0))],
        out_specs=[
            pl.BlockSpec((nb, N_EXPERTS, S), blk),
            pl.BlockSpec((nb, N_EXPERTS, S), blk),
            pl.BlockSpec((nb,) + lists_shape, lambda b: (b, 0, 0, 0)),
        ],
        out_shape=[
            jax.ShapeDtypeStruct((B, N_EXPERTS, S), F32),
            jax.ShapeDtypeStruct((B, N_EXPERTS, S), F32),
            jax.ShapeDtypeStruct((B,) + lists_shape, jnp.int32),
        ],
        compiler_params=_cparams("parallel"),
        name="route",
    )(lg3, tri)


def _chunk_tiles(list_ref, k, c):
    out = []
    for i in range(CHUNK_TILES):
        t = c * CHUNK_TILES + i
        out.append((list_ref[k, 0, t], list_ref[k, 1, t], list_ref[k, 2, t]))
    return out


def _num_chunks(list_ref, k):
    return jnp.right_shift(list_ref[k, 3, 0] + CHUNK_TILES - 1, CHUNK_SHIFT)


def _for_each_chunk(list_ref, body):
    for k in range(BLOCKS_PER_STEP):
        body(k, 0, True)
    for k in range(BLOCKS_PER_STEP):
        def rest(c, carry, k=k):
            body(k, c, False)
            return carry
        lax.fori_loop(1, _num_chunks(list_ref, k), rest, 0)


def _tile_rows(g_t):
    return pl.ds(pl.multiple_of(g_t * SLOT_TILE, SLOT_TILE), SLOT_TILE)


def _onehot(entries, rank_ref, j, weight_ref=None):
    tb = rank_ref.shape[-1]
    sub = lax.broadcasted_iota(jnp.int32, (SLOT_TILE, tb), 0)
    tiles = []
    for e_t, _, slot0 in entries:
        hit = rank_ref[e_t, pl.ds(j, 1), :] == (slot0 + sub).astype(F32)
        val = 1.0 if weight_ref is None else weight_ref[e_t, pl.ds(j, 1), :]
        tiles.append(jnp.where(hit, val, 0.0).astype(BF16))
    return jnp.concatenate(tiles, axis=0)


def _gather_kernel(list_ref, h2_ref, rank_ref, xs_ref):
    tb = rank_ref.shape[-1]
    step = pl.program_id(1)

    @pl.when(step == 0)
    def _():
        xs_ref[...] = jnp.zeros_like(xs_ref)

    def chunk(k, c, first):
        entries = _chunk_tiles(list_ref, k, c)
        onehot = _onehot(entries, rank_ref, step * BLOCKS_PER_STEP + k)
        rows = _dot(onehot, h2_ref[k * tb:(k + 1) * tb, :]).astype(BF16)
        for i, (e_t, g_t, _) in enumerate(entries):
            xs_ref[e_t, _tile_rows(g_t), :] += rows[i * SLOT_TILE:(i + 1) * SLOT_TILE, :]

    _for_each_chunk(list_ref, chunk)


def _list_spec(lists):
    return pl.BlockSpec((None, BLOCKS_PER_STEP) + lists.shape[2:], lambda b, j: (b, j, 0, 0),
                        memory_space=pltpu.SMEM)


def _gather(h2, rank4, lists, cap):
    B, S, D = h2.shape
    _, E, nblk, tb = rank4.shape
    rows = BLOCKS_PER_STEP * tb
    return pl.pallas_call(
        _gather_kernel,
        grid=(B, nblk // BLOCKS_PER_STEP),
        in_specs=[
            _list_spec(lists),
            pl.BlockSpec((None, rows, D), lambda b, j: (b, j, 0)),
            pl.BlockSpec((None, E, nblk, tb), lambda b, j: (b, 0, 0, 0)),
        ],
        out_specs=pl.BlockSpec((None, E, cap, D), lambda b, j: (b, 0, 0, 0)),
        out_shape=jax.ShapeDtypeStruct((B, E, cap, D), BF16),
        compiler_params=_cparams("arbitrary", "arbitrary"),
        name="gather",
    )(lists, h2, rank4)


def _ffn_kernel(xs_ref, wg_ref, wu_ref, wd_ref, y_ref, wg_b, wu_b, wd_b):
    @pl.when(pl.program_id(1) == 0)
    def _():
        wg_b[...] = wg_ref[...].astype(BF16)
        wu_b[...] = wu_ref[...].astype(BF16)
        wd_b[...] = wd_ref[...].astype(BF16)

    nb, cap, D = xs_ref.shape
    xs = xs_ref[...].reshape(nb * cap, D)
    a = _dot(xs, wg_b[...])
    b = _dot(xs, wu_b[...])
    hidden = (a * jax.nn.sigmoid(a) * b).astype(BF16)
    y_ref[...] = _dot(hidden, wd_b[...]).astype(BF16).reshape(nb, cap, D)


def _ffn(xs, wg, wu, wd, layer, nb):
    B, E, cap, D = xs.shape
    F = wg.shape[3]
    blk = pl.BlockSpec((nb, None, cap, D), lambda e, b: (b, e, 0, 0))
    return pl.pallas_call(
        _ffn_kernel,
        grid=(E, B // nb),
        in_specs=[
            blk,
            pl.BlockSpec((None, None, D, F), lambda e, b: (layer, e, 0, 0)),
            pl.BlockSpec((None, None, D, F), lambda e, b: (layer, e, 0, 0)),
            pl.BlockSpec((None, None, F, D), lambda e, b: (layer, e, 0, 0)),
        ],
        out_specs=blk,
        out_shape=jax.ShapeDtypeStruct((B, E, cap, D), BF16),
        scratch_shapes=[pltpu.VMEM((D, F), BF16), pltpu.VMEM((D, F), BF16), pltpu.VMEM((F, D), BF16)],
        compiler_params=_cparams("arbitrary", "arbitrary"),
        name="ffn",
    )(xs, wg, wu, wd)


def _scatter_kernel(list_ref, y_ref, rank_ref, aff_ref, x1_ref, gout_ref, out_ref, *, norm_out):
    tb = rank_ref.shape[-1]
    step = pl.program_id(1)

    def chunk(k, c, first):
        entries = _chunk_tiles(list_ref, k, c)
        weights = _onehot(entries, rank_ref, step * BLOCKS_PER_STEP + k, aff_ref)
        ys = jnp.concatenate([y_ref[e_t, _tile_rows(g_t), :] for e_t, g_t, _ in entries], axis=0)
        add = lax.dot_general(weights, ys, (((0,), (0,)), ((), ())), preferred_element_type=F32)
        rows = slice(k * tb, (k + 1) * tb)
        out_ref[rows, :] = add + (x1_ref[rows, :] if first else out_ref[rows, :])

    _for_each_chunk(list_ref, chunk)
    if norm_out:
        out_ref[...] = _rms(out_ref[...]) * gout_ref[...]


def _scatter(y, rank4, aff4, x1, gout, lists, norm_out):
    B, E, cap, D = y.shape
    _, _, nblk, tb = rank4.shape
    steps = nblk // BLOCKS_PER_STEP
    rows = BLOCKS_PER_STEP * tb
    per_seq = pl.BlockSpec((None, E, nblk, tb), lambda b, j: (b, 0, 0, 0))
    return pl.pallas_call(
        functools.partial(_scatter_kernel, norm_out=norm_out),
        grid=(B, steps),
        in_specs=[
            _list_spec(lists),
            pl.BlockSpec((None, E, cap, D), lambda b, j: (b, 0, 0, 0)),
            per_seq,
            per_seq,
            pl.BlockSpec((rows, D), lambda b, j: (b * steps + j, 0)),
            pl.BlockSpec((1, D), lambda b, j: (0, 0)),
        ],
        out_specs=pl.BlockSpec((rows, D), lambda b, j: (b * steps + j, 0)),
        out_shape=jax.ShapeDtypeStruct((B * nblk * tb, D), F32),
        compiler_params=_cparams("arbitrary", "arbitrary"),
        name="scatter",
    )(lists, y, rank4, aff4, x1, gout)


def _rope_tables(S, g, scale):
    rows = S // GRID_W
    row = jnp.broadcast_to(jnp.arange(rows, dtype=F32)[:, None], (rows, GRID_W)).reshape(-1)
    col = jnp.broadcast_to(jnp.arange(GRID_W, dtype=F32)[None, :], (rows, GRID_W)).reshape(-1)
    part = HEAD_DIM // 2
    freqs = ROPE_THETA ** (-jnp.arange(0, part, 2, dtype=F32) / part)
    ang = jnp.stack([row[:, None] * freqs, col[:, None] * freqs], axis=1)
    cos, sin = jnp.cos(ang), jnp.sin(ang)
    c64 = jnp.concatenate([cos[:, 0], cos[:, 0], cos[:, 1], cos[:, 1]], axis=-1)
    s64 = jnp.concatenate([-sin[:, 0], sin[:, 0], -sin[:, 1], sin[:, 1]], axis=-1)
    g_swapped = g.reshape(2, 2, HEAD_DIM // 4)[:, ::-1, :].reshape(HEAD_DIM)
    gc = jnp.tile(c64 * g[None, :], (1, 2)) * scale
    gs = jnp.tile(s64 * g_swapped[None, :], (1, 2)) * scale
    return gc, gs


def _block_diag(blocks):
    n = blocks.shape[0]
    m = blocks.shape[1]
    eye = jnp.eye(n, dtype=blocks.dtype)
    return (eye[:, None, :, None] * blocks[:, :, None, :]).reshape(n * m, n * blocks.shape[2])


def kernel(x, g_mix, w_in, g_q, g_k, g_sgu, w_spatial, b_spatial, w_pool, pool_scale, w_attn_o,
           w_sgu_o, w_pool_o, w_out, g_ffn, w_router, w_gate_e, w_up_e, w_down_e, g_final):
    B, S, D = x.shape
    depth = w_in.shape[0]
    cap = CAPACITY_FACTOR * S // N_EXPERTS
    x2 = x.reshape(B * S, D)

    bd = _block_diag(jnp.ones((2, HEAD_DIM, HEAD_DIM), F32)).astype(BF16)
    k_i = lax.broadcasted_iota(jnp.int32, (LANES, LANES), 0)
    l_i = lax.broadcasted_iota(jnp.int32, (LANES, LANES), 1)
    tri = jnp.concatenate([(k_i < l_i).astype(BF16), jnp.ones((LANES, LANES), BF16)], axis=1)

    for l in range(depth):
        gcq, gsq = _rope_tables(S, g_q[l], HEAD_DIM ** -0.5 * math.log2(math.e))
        gck, gsk = _rope_tables(S, g_k[l], 1.0)
        w_a = w_in[l, :, :A_W].astype(BF16)
        w_gl = w_in[l, :, A_W:].astype(BF16)
        gmix = g_mix[l][None, :]
        q, k, v, u, vn, p = _inproj(x2, gmix, w_a, gcq, gsq, gck, gsk, bd, g_sgu[l][None, :], S, 512)
        attn = _attention(q, k, v, B, S, 2 * QUERY_SUB)

        wsp = w_spatial[l].transpose(1, 0, 2).reshape(SGU_CHUNK, SGU_GROUPS * SGU_CHUNK).astype(BF16)
        bsp = jnp.repeat(b_spatial[l].T, SGU_HEAD, axis=1)
        wpool = _block_diag(w_pool[l]).astype(BF16)
        wr = jnp.pad(w_router[l], ((0, 0), (0, LANES - N_EXPERTS))).astype(BF16)
        x1, h2, lg = _merge(
            x2, gmix, w_gl, attn, u, vn, p, wsp, bsp, wpool, pool_scale[l][None, :],
            w_attn_o[l].astype(BF16), w_sgu_o[l].astype(BF16), w_pool_o[l].astype(BF16),
            w_out[l].astype(BF16), g_ffn[l][None, :], wr, S, 512)

        rank, aff, lists = _route(lg.reshape(B, S, LANES), tri, cap, math.gcd(B, 4))
        per_block = (B, N_EXPERTS, S // TOKEN_BLOCK, TOKEN_BLOCK)
        rank4, aff4 = rank.reshape(per_block), aff.reshape(per_block)
        xs = _gather(h2.reshape(B, S, D), rank4, lists, cap)
        y = _ffn(xs, w_gate_e, w_up_e, w_down_e, l, math.gcd(B, 4))
        x2 = _scatter(y, rank4, aff4, x1, g_final[None, :], lists, norm_out=(l == depth - 1))

    return x2.reshape(B, S, D)
```

```python
import functools
import math

import jax
import jax.numpy as jnp
from jax import lax
from jax.experimental import pallas as pl
from jax.experimental.pallas import tpu as pltpu

F32 = jnp.float32
BF16 = jnp.bfloat16

GRID_W = 64
N_HEADS = 8
N_KV_HEADS = 2
HEAD_DIM = 64
ROPE_THETA = 10000.0
SGU_GROUPS = 4
SGU_HEAD = 64
SGU_W = SGU_GROUPS * SGU_HEAD
SGU_CHUNK = 128
POOL_WINDOWS = (2, 4, 8, 16)
POOL_HEAD = 64
POOL_W = len(POOL_WINDOWS) * POOL_HEAD
POOL_HALO = 8
KEY_CHUNK = 256
QUERY_SUB = 512
TOKEN_BLOCK = 256
BLOCKS_PER_STEP = 4
INPROJ_ROWS = 512
MERGE_ROWS = 512
ATTN_QUERY_SUBS = 2
SEQS_PER_STEP = 4
TILE_SHIFT = 4
SLOT_TILE = 1 << TILE_SHIFT
CHUNK_SHIFT = 6
CHUNK_TILES = 1 << CHUNK_SHIFT
NO_SLOT = -1024
Q_W = N_HEADS * HEAD_DIM
KV_W = N_KV_HEADS * HEAD_DIM
N_EXPERTS = 16
CAPACITY_FACTOR = 2
EPS = 1e-6

LANES = 128
F32_SUBLANES = 8
A_W = Q_W + 2 * KV_W + 2 * SGU_W + POOL_W
VMEM_LIMIT = 56 * 1024 * 1024


def _cparams(*sem):
    return pltpu.CompilerParams(dimension_semantics=sem, vmem_limit_bytes=VMEM_LIMIT)


def _dot(a, b):
    return jnp.dot(a, b, preferred_element_type=F32)


def _dot_nt(a, b):
    return lax.dot_general(a, b, (((1,), (1,)), ((), ())), preferred_element_type=F32)


def _rms(x):
    return x * lax.rsqrt(jnp.mean(x * x, axis=-1, keepdims=True) + EPS)


def _head_norm_rope(xq, gc, gs, bd):
    sq = xq * xq
    hi = sq.astype(BF16)
    lo = (sq - hi.astype(F32)).astype(BF16)
    ss = _dot(hi, bd) + _dot(lo, bd)
    r = lax.rsqrt(ss * (1.0 / HEAD_DIM) + EPS)
    lane = lax.broadcasted_iota(jnp.int32, xq.shape, 1)
    first_half = (lane % 32) < 16
    partner = jnp.where(first_half, pltpu.roll(xq, LANES - 16, 1), pltpu.roll(xq, 16, 1))
    return r * (xq * gc + partner * gs)


def _project_rows(x, gmix_ref, w_ref, gcq_ref, gsq_ref, gck_ref, gsk_ref, bd_ref, gsgu_ref,
                  q_ref, k_ref, v_ref, u_ref, vn_ref, p_ref):
    h = (_rms(x) * gmix_ref[...]).astype(BF16)
    acc = _dot(h, w_ref[...])
    bd = bd_ref[...]
    for c in range(Q_W // LANES):
        sl = slice(c * LANES, (c + 1) * LANES)
        q_ref[:, sl] = _head_norm_rope(acc[:, sl], gcq_ref[...], gsq_ref[...], bd).astype(BF16)
    o = Q_W
    k_ref[...] = _head_norm_rope(acc[:, o:o + KV_W], gck_ref[...], gsk_ref[...], bd).astype(BF16)
    o += KV_W
    v_ref[...] = acc[:, o:o + KV_W].astype(BF16)
    o += KV_W
    u_ref[...] = acc[:, o:o + SGU_W]
    o += SGU_W
    vn_ref[...] = (_rms(acc[:, o:o + SGU_W]) * gsgu_ref[...]).astype(BF16)
    o += SGU_W
    p_ref[...] = acc[:, o:o + POOL_W]


def _inproj_kernel(x_ref, *refs):
    _project_rows(x_ref[...], *refs)


def _project_specs(T, D, tm, row, pos, const):
    in_specs = [
        pl.BlockSpec((1, D), const),
        pl.BlockSpec((D, A_W), const),
        pl.BlockSpec((tm, LANES), pos),
        pl.BlockSpec((tm, LANES), pos),
        pl.BlockSpec((tm, LANES), pos),
        pl.BlockSpec((tm, LANES), pos),
        pl.BlockSpec((LANES, LANES), const),
        pl.BlockSpec((1, SGU_W), const),
    ]
    widths = ((Q_W, BF16), (KV_W, BF16), (KV_W, BF16), (SGU_W, F32), (SGU_W, BF16), (POOL_W, F32))
    out_specs = [pl.BlockSpec((tm, w), row) for w, _ in widths]
    out_shape = [jax.ShapeDtypeStruct((T, w), dt) for w, dt in widths]
    return in_specs, out_specs, out_shape


def _inproj(x2, params, S, tm):
    T, D = x2.shape
    nseq = S // tm
    row = lambda i: (i, 0)
    in_specs, out_specs, out_shape = _project_specs(
        T, D, tm, row, lambda i: (i % nseq, 0), lambda i: (0, 0))
    return pl.pallas_call(
        _inproj_kernel,
        grid=(T // tm,),
        in_specs=[pl.BlockSpec((tm, D), row)] + in_specs,
        out_specs=out_specs,
        out_shape=out_shape,
        compiler_params=_cparams("parallel"),
        name="inproj",
    )(x2, *params)


def _attn_kernel(q_ref, k_ref, v_ref, o_ref, kz_ref, vzt_ref, s_ref):
    @pl.when(pl.program_id(1) == 0)
    def _():
        lane = lax.broadcasted_iota(jnp.int32, k_ref.shape, 1)
        low = lane < HEAD_DIM
        k = k_ref[...].astype(F32)
        kr = pltpu.roll(k, HEAD_DIM, 1)
        kz_ref[0] = jnp.where(low, k, 0.0).astype(BF16)
        kz_ref[1] = jnp.where(low, 0.0, kr).astype(BF16)
        kz_ref[2] = jnp.where(low, kr, 0.0).astype(BF16)
        kz_ref[3] = jnp.where(low, 0.0, k).astype(BF16)
        vt = v_ref[...].astype(F32).T
        vtr = pltpu.roll(vt, HEAD_DIM, 0)
        row = lax.broadcasted_iota(jnp.int32, vt.shape, 0)
        top = row < HEAD_DIM
        ones_lo = jnp.where(row == 0, 1.0, 0.0)
        ones_hi = jnp.where(row == HEAD_DIM, 1.0, 0.0)
        vzt_ref[0] = jnp.where(top, vt, ones_hi).astype(BF16)
        vzt_ref[1] = jnp.where(top, ones_lo, vtr).astype(BF16)
        vzt_ref[2] = jnp.where(top, vtr, ones_hi).astype(BF16)
        vzt_ref[3] = jnp.where(top, ones_lo, vt).astype(BF16)

    tq = s_ref.shape[2]
    S = k_ref.shape[0]
    nchunk = S // KEY_CHUNK
    sub = F32_SUBLANES
    slot0 = jnp.minimum(pl.program_id(1), 0)

    def unit_parts(unit):
        qt, head = divmod(unit, N_HEADS)
        pair, parity = divmod(head, 2)
        kvi = 2 * ((2 * pair) // (N_HEADS // N_KV_HEADS)) + parity
        return slice(qt * tq, (qt + 1) * tq), slice(pair * LANES, (pair + 1) * LANES), parity, kvi

    def scores(unit, c, mrun):
        qrows, lanes, _, kvi = unit_parts(unit)
        rows = slice(c * KEY_CHUNK, (c + 1) * KEY_CHUNK)
        s = _dot_nt(kz_ref[kvi, rows, :], q_ref[qrows, lanes])
        s_ref[slot0 + unit % 2, rows, :] = s
        m = jnp.max(s.reshape(KEY_CHUNK // sub, sub, tq), axis=0)
        return m if mrun is None else jnp.maximum(mrun, m)

    def weighted(unit, c, m8, acc):
        kvi = unit_parts(unit)[3]
        rows = slice(c * KEY_CHUNK, (c + 1) * KEY_CHUNK)
        s3 = s_ref[slot0 + unit % 2, rows, :].reshape(KEY_CHUNK // sub, sub, tq)
        e = jnp.exp2(s3 - m8[None]).reshape(KEY_CHUNK, tq).astype(BF16)
        d = _dot(vzt_ref[kvi, :, rows], e)
        return d if acc is None else acc + d

    nunit = (q_ref.shape[0] // tq) * N_HEADS
    top = lax.broadcasted_iota(jnp.int32, (LANES, tq), 0) < HEAD_DIM
    mrun = None
    for c in range(nchunk):
        mrun = scores(0, c, mrun)
    even = None
    for unit in range(nunit):
        m8 = jnp.broadcast_to(jnp.max(mrun, axis=0, keepdims=True), (sub, tq))
        mnext, acc = None, None
        for c in range(nchunk):
            if unit + 1 < nunit:
                mnext = scores(unit + 1, c, mnext)
            acc = weighted(unit, c, m8, acc)
        mrun = mnext
        qrows, lanes, parity, _ = unit_parts(unit)
        if parity == 0:
            even = acc * (1.0 / acc[HEAD_DIM:HEAD_DIM + 1])
        else:
            odd = acc * (1.0 / acc[0:1])
            o_ref[qrows, lanes] = jnp.where(top, even, odd).T.astype(BF16)


def _attention(q, k, v, B, S, tq):
    T = q.shape[0]
    nq = S // tq
    return pl.pallas_call(
        _attn_kernel,
        grid=(B, nq),
        in_specs=[
            pl.BlockSpec((tq, Q_W), lambda b, i: (b * nq + i, 0)),
            pl.BlockSpec((S, KV_W), lambda b, i: (b, 0)),
            pl.BlockSpec((S, KV_W), lambda b, i: (b, 0)),
        ],
        out_specs=pl.BlockSpec((tq, Q_W), lambda b, i: (b * nq + i, 0)),
        out_shape=jax.ShapeDtypeStruct((T, Q_W), BF16),
        scratch_shapes=[pltpu.VMEM((4, S, KV_W), BF16), pltpu.VMEM((4, KV_W, S), BF16),
                        pltpu.VMEM((2, S, QUERY_SUB), F32)],
        compiler_params=_cparams("parallel", "arbitrary"),
        name="attention",
    )(q, k, v)


def _merge_kernel(x_ref, gmix_ref, wgl_ref, attn_ref, u_ref, vn_ref, p_ref, pprev_ref, pnext_ref,
                  wsp_ref, bsp_ref, wpool_ref, pscale_ref, wao_ref, wso_ref, wpo_ref, wout_ref,
                  gffn_ref, wr_ref, x1_ref, h2_ref, lg_ref, *, S):
    ts = x_ref.shape[0]

    vn = vn_ref[...]
    lane_b = lax.broadcasted_iota(jnp.int32, (SGU_CHUNK, SGU_W), 1) // SGU_HEAD
    zs = []
    for c in range(ts // SGU_CHUNK):
        vc = vn[c * SGU_CHUNK:(c + 1) * SGU_CHUNK]
        stacked = jnp.concatenate(
            [jnp.where(lane_b == g, vc, jnp.zeros_like(vc)) for g in range(SGU_GROUPS)], axis=0)
        zs.append(_dot(wsp_ref[...], stacked) + bsp_ref[...])
    sgu = (u_ref[...] * jnp.concatenate(zs, axis=0)).astype(BF16)

    pos0 = (pl.program_id(0) % (S // ts)) * ts
    p = p_ref[...]
    prev = jnp.where(pos0 == 0, 0.0, pprev_ref[...])
    nxt = jnp.where(pos0 + ts == S, 0.0, pnext_ref[...])
    ext = jnp.concatenate([prev, p, nxt], axis=0)
    n = ts + 2 * POOL_HALO
    s2 = ext + pltpu.roll(ext, 1, 0)
    s4 = pltpu.roll(s2, 1, 0) + pltpu.roll(s2, n - 1, 0)
    s8 = pltpu.roll(s4, 2, 0) + pltpu.roll(s4, n - 2, 0)
    s16 = pltpu.roll(s8, 4, 0) + pltpu.roll(s8, n - 4, 0)
    mid = slice(POOL_HALO, POOL_HALO + ts)
    grp = lax.broadcasted_iota(jnp.int32, (ts, POOL_W), 1) // POOL_HEAD
    t = pos0 + lax.broadcasted_iota(jnp.int32, (ts, POOL_W), 0)
    half = jnp.left_shift(1, grp)
    cnt = jnp.minimum(t + half - 1, S - 1) - jnp.maximum(t - half, 0) + 1
    wsum = jnp.where(grp == 0, s2[mid], jnp.where(grp == 1, s4[mid],
                                                  jnp.where(grp == 2, s8[mid], s16[mid])))
    d = (wsum / cnt.astype(F32) - p).astype(BF16)
    pooled = (_dot(d, wpool_ref[...]) * pscale_ref[...]).astype(BF16)

    x = x_ref[...]
    D = x.shape[1]
    h = (_rms(x) * gmix_ref[...]).astype(BF16)
    br_a = _dot(attn_ref[...], wao_ref[...])
    br_b = _dot(sgu, wso_ref[...])
    br_c = _dot(pooled, wpo_ref[...])
    gl = _dot(h, wgl_ref[...])
    gate = lambda j: 0.5 * (1.0 + jnp.tanh(0.5 * gl[:, j * D:(j + 1) * D]))
    merged = (gate(0) * br_a + gate(1) * br_b + gate(2) * br_c).astype(BF16)
    x1 = x + _dot(merged, wout_ref[...])
    x1_ref[...] = x1
    h2 = (_rms(x1) * gffn_ref[...]).astype(BF16)
    h2_ref[...] = h2
    lg_ref[...] = _dot(h2, wr_ref[...])


def _merge(x2, gmix, w_gl, attn, u, vn, p, wsp, bsp, wpool, pscale, wao, wso, wpo, wout, gffn, wr,
           S, ts):
    T, D = x2.shape
    row = lambda i: (i, 0)
    const = lambda i: (0, 0)
    hb = ts // POOL_HALO
    nhb = T // POOL_HALO
    full = lambda a: pl.BlockSpec(a.shape, const)
    return pl.pallas_call(
        functools.partial(_merge_kernel, S=S),
        grid=(T // ts,),
        in_specs=[
            pl.BlockSpec((ts, D), row),
            full(gmix),
            full(w_gl),
            pl.BlockSpec((ts, Q_W), row),
            pl.BlockSpec((ts, SGU_W), row),
            pl.BlockSpec((ts, SGU_W), row),
            pl.BlockSpec((ts, POOL_W), row),
            pl.BlockSpec((POOL_HALO, POOL_W), lambda i: (jnp.maximum(i * hb - 1, 0), 0)),
            pl.BlockSpec((POOL_HALO, POOL_W), lambda i: (jnp.minimum((i + 1) * hb, nhb - 1), 0)),
            full(wsp), full(bsp), full(wpool), full(pscale), full(wao), full(wso), full(wpo),
            full(wout), full(gffn), full(wr),
        ],
        out_specs=[
            pl.BlockSpec((ts, D), row),
            pl.BlockSpec((ts, D), row),
            pl.BlockSpec((ts, LANES), row),
        ],
        out_shape=[
            jax.ShapeDtypeStruct((T, D), F32),
            jax.ShapeDtypeStruct((T, D), BF16),
            jax.ShapeDtypeStruct((T, LANES), F32),
        ],
        compiler_params=_cparams("parallel"),
        name="merge",
    )(x2, gmix, w_gl, attn, u, vn, p, p, p, wsp, bsp, wpool, pscale, wao, wso, wpo, wout, gffn, wr)


def _excl_cumsum(m01, tri_ones):
    off = jnp.zeros((m01.shape[0], LANES), F32)
    pieces = []
    for j in range(m01.shape[1] // LANES):
        res = _dot(m01[:, j * LANES:(j + 1) * LANES].astype(BF16), tri_ones)
        pieces.append(res[:, :LANES] + off)
        off = off + res[:, LANES:]
    return jnp.concatenate(pieces, axis=1)


def _tile_lists(offs, nblk, list_len):
    E = offs.shape[0]
    lane = lax.broadcasted_iota(jnp.int32, offs.shape, 1)
    hi = pltpu.roll(offs, LANES - 1, 1)
    g_lo = jnp.floor(offs * (1.0 / SLOT_TILE))
    g_hi = jnp.floor((hi + (SLOT_TILE - 1)) * (1.0 / SLOT_TILE))
    cnt = jnp.where((hi > offs) & (lane < nblk), g_hi - g_lo, 0.0)
    below = (lax.broadcasted_iota(jnp.int32, (E, E), 1) < lax.broadcasted_iota(jnp.int32, (E, E), 0))
    base = _dot(jnp.where(below, 1.0, 0.0).astype(BF16), cnt.astype(BF16))
    end = base + cnt
    t = lax.broadcasted_iota(jnp.int32, (1, list_len), 1).astype(F32)
    row_e = lax.broadcasted_iota(jnp.int32, (E, list_len), 0).astype(F32)
    blocks = []
    for j in range(nblk):
        col = slice(j, j + 1)
        e_t = jnp.sum(jnp.where(end[:, col] <= t, 1.0, 0.0), axis=0, keepdims=True)
        shift = jnp.sum(jnp.where(row_e == e_t, g_lo[:, col] - base[:, col], 0.0), axis=0, keepdims=True)
        valid = e_t < E
        g_t = jnp.where(valid, t + shift, 0.0)
        total = jnp.broadcast_to(end[E - 1:E, col], (1, list_len))
        rows = [jnp.where(valid, e_t, 0.0), g_t, jnp.where(valid, g_t * SLOT_TILE, float(NO_SLOT)),
                total, jnp.zeros((4, list_len), F32)]
        blocks.append(jnp.concatenate(rows, axis=0).astype(jnp.int32)[None])
    return jnp.concatenate(blocks, axis=0)


def _route_kernel(lg_ref, tri_ref, rank_ref, aff_ref, lists_ref, *, cap):
    nb, S, _ = lg_ref.shape
    E = N_EXPERTS
    rows = []
    for b in range(nb):
        lt = lg_ref[b].T[:E]
        e = jnp.exp(lt - jnp.max(lt, axis=0, keepdims=True))
        rows.append(e / jnp.sum(e, axis=0, keepdims=True))
    aff = jnp.concatenate(rows, axis=0)
    bits = pltpu.bitcast(aff, jnp.int32)

    def refine(i, thr):
        cand = thr | jnp.left_shift(jnp.int32(1), 30 - i)
        cnt = jnp.sum(jnp.where(bits >= cand, 1.0, 0.0), axis=1, keepdims=True)
        return jnp.where(cnt >= cap, cand, thr)

    thr = lax.fori_loop(0, 31, refine, jnp.zeros((nb * E, 1), jnp.int32))
    gt = jnp.where(bits > thr, 1.0, 0.0)
    eq = jnp.where(bits == thr, 1.0, 0.0)
    need = cap - jnp.sum(gt, axis=1, keepdims=True)
    tri = tri_ref[...]
    sel = gt + eq * jnp.where(_excl_cumsum(eq, tri) < need, 1.0, 0.0)
    before = _excl_cumsum(sel, tri)
    rank = jnp.where(sel > 0.5, before, -1.0)
    nblk = S // TOKEN_BLOCK
    lane = lax.broadcasted_iota(jnp.int32, (nb * E, LANES), 1)
    offs = jnp.where(lane == nblk, float(cap), 0.0)
    for j in range(nblk):
        offs = jnp.where(lane == j, before[:, j * TOKEN_BLOCK:j * TOKEN_BLOCK + 1], offs)
    list_len = lists_ref.shape[-1]
    for b in range(nb):
        rank_ref[b] = rank[b * E:(b + 1) * E]
        aff_ref[b] = aff[b * E:(b + 1) * E]
        lists_ref[b] = _tile_lists(offs[b * E:(b + 1) * E], nblk, list_len)


def _list_len(cap):
    worst = N_EXPERTS * (cap // SLOT_TILE + 2)
    return -(-worst // LANES) * LANES


def _route(lg3, tri, cap, nb):
    B, S, _ = lg3.shape
    nblk = S // TOKEN_BLOCK
    blk = lambda b: (b, 0, 0)
    lists_shape = (nblk, 8, _list_len(cap))
    return pl.pallas_call(
        functools.partial(_route_kernel, cap=cap),
        grid=(B // nb,),
        in_specs=[pl.BlockSpec((nb, S, LANES), blk), pl.BlockSpec(tri.shape, lambda b: (0, 0))],
        out_specs=[
            pl.BlockSpec((nb, N_EXPERTS, S), blk),
            pl.BlockSpec((nb, N_EXPERTS, S), blk),
            pl.BlockSpec((nb,) + lists_shape, lambda b: (b, 0, 0, 0)),
        ],
        out_shape=[
            jax.ShapeDtypeStruct((B, N_EXPERTS, S), F32),
            jax.ShapeDtypeStruct((B, N_EXPERTS, S), F32),
            jax.ShapeDtypeStruct((B,) + lists_shape, jnp.int32),
        ],
        compiler_params=_cparams("parallel"),
        name="route",
    )(lg3, tri)


def _chunk_tiles(list_ref, k, c):
    out = []
    for i in range(CHUNK_TILES):
        t = c * CHUNK_TILES + i
        out.append((list_ref[k, 0, t], list_ref[k, 1, t], list_ref[k, 2, t]))
    return out


def _num_chunks(list_ref, k):
    return jnp.right_shift(list_ref[k, 3, 0] + CHUNK_TILES - 1, CHUNK_SHIFT)


def _for_each_chunk(list_ref, body):
    for k in range(list_ref.shape[0]):
        body(k, 0, True)
    for k in range(list_ref.shape[0]):
        def rest(c, carry, k=k):
            body(k, c, False)
            return carry
        lax.fori_loop(1, _num_chunks(list_ref, k), rest, 0)


def _tile_rows(g_t):
    return pl.ds(pl.multiple_of(g_t * SLOT_TILE, SLOT_TILE), SLOT_TILE)


def _onehot(entries, rank_ref, j, weight_ref=None):
    tb = rank_ref.shape[-1]
    sub = lax.broadcasted_iota(jnp.int32, (SLOT_TILE, tb), 0)
    tiles = []
    for e_t, _, slot0 in entries:
        hit = rank_ref[e_t, pl.ds(j, 1), :] == (slot0 + sub).astype(F32)
        val = 1.0 if weight_ref is None else weight_ref[e_t, pl.ds(j, 1), :]
        tiles.append(jnp.where(hit, val, 0.0).astype(BF16))
    return jnp.concatenate(tiles, axis=0)


def _gather_kernel(list_ref, h2_ref, rank_ref, xs_ref):
    tb = rank_ref.shape[-1]
    step = pl.program_id(1)

    @pl.when(step == 0)
    def _():
        xs_ref[...] = jnp.zeros_like(xs_ref)

    def chunk(k, c, first):
        entries = _chunk_tiles(list_ref, k, c)
        onehot = _onehot(entries, rank_ref, step * list_ref.shape[0] + k)
        rows = _dot(onehot, h2_ref[k * tb:(k + 1) * tb, :]).astype(BF16)
        for i, (e_t, g_t, _) in enumerate(entries):
            xs_ref[e_t, _tile_rows(g_t), :] += rows[i * SLOT_TILE:(i + 1) * SLOT_TILE, :]

    _for_each_chunk(list_ref, chunk)


def _list_spec(lists, nblocks):
    return pl.BlockSpec((None, nblocks) + lists.shape[2:], lambda b, j: (b, j, 0, 0),
                        memory_space=pltpu.SMEM)


def _gather(h2, rank4, lists, cap):
    B, S, D = h2.shape
    _, E, nblk, tb = rank4.shape
    rows = BLOCKS_PER_STEP * tb
    return pl.pallas_call(
        _gather_kernel,
        grid=(B, nblk // BLOCKS_PER_STEP),
        in_specs=[
            _list_spec(lists, BLOCKS_PER_STEP),
            pl.BlockSpec((None, rows, D), lambda b, j: (b, j, 0)),
            pl.BlockSpec((None, E, nblk, tb), lambda b, j: (b, 0, 0, 0)),
        ],
        out_specs=pl.BlockSpec((None, E, cap, D), lambda b, j: (b, 0, 0, 0)),
        out_shape=jax.ShapeDtypeStruct((B, E, cap, D), BF16),
        compiler_params=_cparams("arbitrary", "arbitrary"),
        name="gather",
    )(lists, h2, rank4)


def _ffn_kernel(xs_ref, wg_ref, wu_ref, wd_ref, y_ref, wg_b, wu_b, wd_b):
    @pl.when(pl.program_id(1) == 0)
    def _():
        wg_b[...] = wg_ref[...].astype(BF16)
        wu_b[...] = wu_ref[...].astype(BF16)
        wd_b[...] = wd_ref[...].astype(BF16)

    nb, cap, D = xs_ref.shape
    xs = xs_ref[...].reshape(nb * cap, D)
    a = _dot(xs, wg_b[...])
    b = _dot(xs, wu_b[...])
    hidden = (a * jax.nn.sigmoid(a) * b).astype(BF16)
    y_ref[...] = _dot(hidden, wd_b[...]).astype(BF16).reshape(nb, cap, D)


def _ffn(xs, wg, wu, wd, layer, nb):
    B, E, cap, D = xs.shape
    F = wg.shape[3]
    blk = pl.BlockSpec((nb, None, cap, D), lambda e, b: (b, e, 0, 0))
    return pl.pallas_call(
        _ffn_kernel,
        grid=(E, B // nb),
        in_specs=[
            blk,
            pl.BlockSpec((None, None, D, F), lambda e, b: (layer, e, 0, 0)),
            pl.BlockSpec((None, None, D, F), lambda e, b: (layer, e, 0, 0)),
            pl.BlockSpec((None, None, F, D), lambda e, b: (layer, e, 0, 0)),
        ],
        out_specs=blk,
        out_shape=jax.ShapeDtypeStruct((B, E, cap, D), BF16),
        scratch_shapes=[pltpu.VMEM((D, F), BF16), pltpu.VMEM((D, F), BF16), pltpu.VMEM((F, D), BF16)],
        compiler_params=_cparams("arbitrary", "arbitrary"),
        name="ffn",
    )(xs, wg, wu, wd)


def _scatter_rows(list_ref, y_ref, rank_ref, aff_ref, x1_ref, out_ref):
    tb = rank_ref.shape[-1]
    step = pl.program_id(1)

    def chunk(k, c, first):
        entries = _chunk_tiles(list_ref, k, c)
        weights = _onehot(entries, rank_ref, step * list_ref.shape[0] + k, aff_ref)
        ys = jnp.concatenate([y_ref[e_t, _tile_rows(g_t), :] for e_t, g_t, _ in entries], axis=0)
        add = lax.dot_general(weights, ys, (((0,), (0,)), ((), ())), preferred_element_type=F32)
        rows = slice(k * tb, (k + 1) * tb)
        out_ref[rows, :] = add + (x1_ref[rows, :] if first else out_ref[rows, :])

    _for_each_chunk(list_ref, chunk)


def _scatter_specs(y, rank4, lists, nblocks):
    B, E, cap, D = y.shape
    _, _, nblk, tb = rank4.shape
    steps = nblk // nblocks
    rows = pl.BlockSpec((nblocks * tb, D), lambda b, j: (b * steps + j, 0))
    per_seq = pl.BlockSpec((None, E, nblk, tb), lambda b, j: (b, 0, 0, 0))
    in_specs = [_list_spec(lists, nblocks), pl.BlockSpec((None, E, cap, D), lambda b, j: (b, 0, 0, 0)),
                per_seq, per_seq, rows]
    return (B, steps), in_specs, rows


def _scatter_norm_kernel(list_ref, y_ref, rank_ref, aff_ref, x1_ref, gout_ref, out_ref):
    _scatter_rows(list_ref, y_ref, rank_ref, aff_ref, x1_ref, out_ref)
    out_ref[...] = _rms(out_ref[...]) * gout_ref[...]


def _scatter_norm(y, rank4, aff4, x1, lists, gout):
    grid, in_specs, rows = _scatter_specs(y, rank4, lists, BLOCKS_PER_STEP)
    return pl.pallas_call(
        _scatter_norm_kernel,
        grid=grid,
        in_specs=in_specs + [pl.BlockSpec(gout.shape, lambda b, j: (0, 0))],
        out_specs=rows,
        out_shape=jax.ShapeDtypeStruct(x1.shape, F32),
        compiler_params=_cparams("arbitrary", "arbitrary"),
        name="scatter_norm",
    )(lists, y, rank4, aff4, x1, gout)


def _scatter_project_kernel(list_ref, y_ref, rank_ref, aff_ref, x1_ref, *refs):
    n_params = len(refs) - 7
    x_ref = refs[n_params]
    _scatter_rows(list_ref, y_ref, rank_ref, aff_ref, x1_ref, x_ref)
    _project_rows(x_ref[...], *refs[:n_params], *refs[n_params + 1:])


def _scatter_project(y, rank4, aff4, x1, lists, params, S):
    T, D = x1.shape
    nblocks = BLOCKS_PER_STEP // 2
    grid, in_specs, rows = _scatter_specs(y, rank4, lists, nblocks)
    tm = nblocks * rank4.shape[-1]
    steps = S // tm
    p_in, p_out, p_shape = _project_specs(
        T, D, tm, lambda b, j: (b * steps + j, 0), lambda b, j: (j, 0), lambda b, j: (0, 0))
    return pl.pallas_call(
        _scatter_project_kernel,
        grid=grid,
        in_specs=in_specs + p_in,
        out_specs=[rows] + p_out,
        out_shape=[jax.ShapeDtypeStruct(x1.shape, F32)] + p_shape,
        compiler_params=_cparams("arbitrary", "arbitrary"),
        name="scatter_project",
    )(lists, y, rank4, aff4, x1, *params)


def _rope_tables(S, g, scale):
    rows = S // GRID_W
    row = jnp.broadcast_to(jnp.arange(rows, dtype=F32)[:, None], (rows, GRID_W)).reshape(-1)
    col = jnp.broadcast_to(jnp.arange(GRID_W, dtype=F32)[None, :], (rows, GRID_W)).reshape(-1)
    part = HEAD_DIM // 2
    freqs = ROPE_THETA ** (-jnp.arange(0, part, 2, dtype=F32) / part)
    ang = jnp.stack([row[:, None] * freqs, col[:, None] * freqs], axis=1)
    cos, sin = jnp.cos(ang), jnp.sin(ang)
    c64 = jnp.concatenate([cos[:, 0], cos[:, 0], cos[:, 1], cos[:, 1]], axis=-1)
    s64 = jnp.concatenate([-sin[:, 0], sin[:, 0], -sin[:, 1], sin[:, 1]], axis=-1)
    g_swapped = g.reshape(2, 2, HEAD_DIM // 4)[:, ::-1, :].reshape(HEAD_DIM)
    gc = jnp.tile(c64 * g[None, :], (1, 2)) * scale
    gs = jnp.tile(s64 * g_swapped[None, :], (1, 2)) * scale
    return gc, gs


def _block_diag(blocks):
    n = blocks.shape[0]
    m = blocks.shape[1]
    eye = jnp.eye(n, dtype=blocks.dtype)
    return (eye[:, None, :, None] * blocks[:, :, None, :]).reshape(n * m, n * blocks.shape[2])


def kernel(x, g_mix, w_in, g_q, g_k, g_sgu, w_spatial, b_spatial, w_pool, pool_scale, w_attn_o,
           w_sgu_o, w_pool_o, w_out, g_ffn, w_router, w_gate_e, w_up_e, w_down_e, g_final):
    B, S, D = x.shape
    depth = w_in.shape[0]
    cap = CAPACITY_FACTOR * S // N_EXPERTS
    x2 = x.reshape(B * S, D)

    bd = _block_diag(jnp.ones((2, HEAD_DIM, HEAD_DIM), F32)).astype(BF16)
    k_i = lax.broadcasted_iota(jnp.int32, (LANES, LANES), 0)
    l_i = lax.broadcasted_iota(jnp.int32, (LANES, LANES), 1)
    tri = jnp.concatenate([(k_i < l_i).astype(BF16), jnp.ones((LANES, LANES), BF16)], axis=1)

    routed = None
    for l in range(depth):
        gcq, gsq = _rope_tables(S, g_q[l], HEAD_DIM ** -0.5 * math.log2(math.e))
        gck, gsk = _rope_tables(S, g_k[l], 1.0)
        w_gl = w_in[l, :, A_W:].astype(BF16)
        gmix = g_mix[l][None, :]
        proj = (gmix, w_in[l, :, :A_W].astype(BF16), gcq, gsq, gck, gsk, bd, g_sgu[l][None, :])
        if routed is None:
            q, k, v, u, vn, p = _inproj(x2, proj, S, INPROJ_ROWS)
        else:
            x2, q, k, v, u, vn, p = _scatter_project(*routed, proj, S)
        attn = _attention(q, k, v, B, S, ATTN_QUERY_SUBS * QUERY_SUB)

        wsp = w_spatial[l].transpose(1, 0, 2).reshape(SGU_CHUNK, SGU_GROUPS * SGU_CHUNK).astype(BF16)
        bsp = jnp.repeat(b_spatial[l].T, SGU_HEAD, axis=1)
        wpool = _block_diag(w_pool[l]).astype(BF16)
        wr = jnp.pad(w_router[l], ((0, 0), (0, LANES - N_EXPERTS))).astype(BF16)
        x1, h2, lg = _merge(
            x2, gmix, w_gl, attn, u, vn, p, wsp, bsp, wpool, pool_scale[l][None, :],
            w_attn_o[l].astype(BF16), w_sgu_o[l].astype(BF16), w_pool_o[l].astype(BF16),
            w_out[l].astype(BF16), g_ffn[l][None, :], wr, S, MERGE_ROWS)

        rank, aff, lists = _route(lg.reshape(B, S, LANES), tri, cap, math.gcd(B, SEQS_PER_STEP))
        per_block = (B, N_EXPERTS, S // TOKEN_BLOCK, TOKEN_BLOCK)
        rank4, aff4 = rank.reshape(per_block), aff.reshape(per_block)
        xs = _gather(h2.reshape(B, S, D), rank4, lists, cap)
        y = _ffn(xs, w_gate_e, w_up_e, w_down_e, l, math.gcd(B, SEQS_PER_STEP))
        routed = (y, rank4, aff4, x1, lists)

    return _scatter_norm(*routed, g_final[None, :]).reshape(B, S, D)
```

```python
import functools
import math

import jax
import jax.numpy as jnp
from jax import lax
from jax.experimental import pallas as pl
from jax.experimental.pallas import tpu as pltpu

F32 = jnp.float32
BF16 = jnp.bfloat16

GRID_W = 64
N_HEADS = 8
N_KV_HEADS = 2
HEAD_DIM = 64
ROPE_THETA = 10000.0
SGU_GROUPS = 4
SGU_HEAD = 64
SGU_W = SGU_GROUPS * SGU_HEAD
SGU_CHUNK = 128
POOL_WINDOWS = (2, 4, 8, 16)
POOL_HEAD = 64
POOL_W = len(POOL_WINDOWS) * POOL_HEAD
POOL_HALO = 8
KEY_CHUNK = 512
QUERY_SUB = 512
TOKEN_BLOCK = 256
BLOCKS_PER_STEP = 4
INPROJ_ROWS = 1024
MERGE_ROWS = 512
ATTN_QUERY_SUBS = 2
SEQS_PER_STEP = 4
TILE_SHIFT = 4
SLOT_TILE = 1 << TILE_SHIFT
CHUNK_SHIFT = 6
CHUNK_TILES = 1 << CHUNK_SHIFT
NO_SLOT = -1024
Q_W = N_HEADS * HEAD_DIM
KV_W = N_KV_HEADS * HEAD_DIM
N_BRANCHES = 3
N_EXPERTS = 16
CAPACITY_FACTOR = 2
EPS = 1e-6

LANES = 128
F32_SUBLANES = 8
A_W = Q_W + 2 * KV_W + 2 * SGU_W + POOL_W
VMEM_LIMIT = 56 * 1024 * 1024


def _cparams(*sem):
    return pltpu.CompilerParams(dimension_semantics=sem, vmem_limit_bytes=VMEM_LIMIT)


def _dot(a, b):
    return jnp.dot(a, b, preferred_element_type=F32)


def _dot_nt(a, b):
    return lax.dot_general(a, b, (((1,), (1,)), ((), ())), preferred_element_type=F32)


def _rms(x):
    return x * lax.rsqrt(jnp.mean(x * x, axis=-1, keepdims=True) + EPS)


def _head_norm_rope(xq, gc, gs, bd):
    sq = xq * xq
    hi = sq.astype(BF16)
    lo = (sq - hi.astype(F32)).astype(BF16)
    ss = _dot(hi, bd) + _dot(lo, bd)
    r = lax.rsqrt(ss * (1.0 / HEAD_DIM) + EPS)
    lane = lax.broadcasted_iota(jnp.int32, xq.shape, 1)
    first_half = (lane % 32) < 16
    partner = jnp.where(first_half, pltpu.roll(xq, LANES - 16, 1), pltpu.roll(xq, 16, 1))
    return r * (xq * gc + partner * gs)


def _project_rows(x, gmix_ref, w_ref, gcq_ref, gsq_ref, gck_ref, gsk_ref, bd_ref, gsgu_ref,
                  q_ref, k_ref, v_ref, u_ref, vn_ref, p_ref):
    h = (_rms(x) * gmix_ref[...]).astype(BF16)
    acc = _dot(h, w_ref[...])
    bd = bd_ref[...]
    for c in range(Q_W // LANES):
        sl = slice(c * LANES, (c + 1) * LANES)
        q_ref[:, sl] = _head_norm_rope(acc[:, sl], gcq_ref[...], gsq_ref[...], bd).astype(BF16)
    o = Q_W
    k_ref[...] = _head_norm_rope(acc[:, o:o + KV_W], gck_ref[...], gsk_ref[...], bd).astype(BF16)
    o += KV_W
    v_ref[...] = acc[:, o:o + KV_W].astype(BF16)
    o += KV_W
    u_ref[...] = acc[:, o:o + SGU_W]
    o += SGU_W
    vn_ref[...] = (_rms(acc[:, o:o + SGU_W]) * gsgu_ref[...]).astype(BF16)
    o += SGU_W
    p_ref[...] = acc[:, o:o + POOL_W]


def _inproj_kernel(x_ref, *refs):
    _project_rows(x_ref[...], *refs)


def _project_specs(T, D, tm, row, pos, const):
    in_specs = [
        pl.BlockSpec((1, D), const),
        pl.BlockSpec((D, A_W), const),
        pl.BlockSpec((tm, LANES), pos),
        pl.BlockSpec((tm, LANES), pos),
        pl.BlockSpec((tm, LANES), pos),
        pl.BlockSpec((tm, LANES), pos),
        pl.BlockSpec((LANES, LANES), const),
        pl.BlockSpec((1, SGU_W), const),
    ]
    widths = ((Q_W, BF16), (KV_W, BF16), (KV_W, BF16), (SGU_W, F32), (SGU_W, BF16), (POOL_W, F32))
    out_specs = [pl.BlockSpec((tm, w), row) for w, _ in widths]
    out_shape = [jax.ShapeDtypeStruct((T, w), dt) for w, dt in widths]
    return in_specs, out_specs, out_shape


def _inproj(x2, params, S, tm):
    T, D = x2.shape
    nseq = S // tm
    row = lambda i: (i, 0)
    in_specs, out_specs, out_shape = _project_specs(
        T, D, tm, row, lambda i: (i % nseq, 0), lambda i: (0, 0))
    return pl.pallas_call(
        _inproj_kernel,
        grid=(T // tm,),
        in_specs=[pl.BlockSpec((tm, D), row)] + in_specs,
        out_specs=out_specs,
        out_shape=out_shape,
        compiler_params=_cparams("parallel"),
        name="inproj",
    )(x2, *params)


def _attn_kernel(q_ref, k_ref, v_ref, o_ref, kz_ref, vzt_ref, s_ref):
    @pl.when(pl.program_id(1) == 0)
    def _():
        lane = lax.broadcasted_iota(jnp.int32, k_ref.shape, 1)
        low = lane < HEAD_DIM
        k = k_ref[...].astype(F32)
        kr = pltpu.roll(k, HEAD_DIM, 1)
        kz_ref[0] = jnp.where(low, k, 0.0).astype(BF16)
        kz_ref[1] = jnp.where(low, 0.0, kr).astype(BF16)
        kz_ref[2] = jnp.where(low, kr, 0.0).astype(BF16)
        kz_ref[3] = jnp.where(low, 0.0, k).astype(BF16)
        vt = v_ref[...].astype(F32).T
        vtr = pltpu.roll(vt, HEAD_DIM, 0)
        row = lax.broadcasted_iota(jnp.int32, vt.shape, 0)
        top = row < HEAD_DIM
        ones_lo = jnp.where(row == 0, 1.0, 0.0)
        ones_hi = jnp.where(row == HEAD_DIM, 1.0, 0.0)
        vzt_ref[0] = jnp.where(top, vt, ones_hi).astype(BF16)
        vzt_ref[1] = jnp.where(top, ones_lo, vtr).astype(BF16)
        vzt_ref[2] = jnp.where(top, vtr, ones_hi).astype(BF16)
        vzt_ref[3] = jnp.where(top, ones_lo, vt).astype(BF16)

    tq = s_ref.shape[2]
    S = k_ref.shape[0]
    nchunk = S // KEY_CHUNK
    sub = F32_SUBLANES
    slot0 = jnp.minimum(pl.program_id(1), 0)

    def unit_parts(unit):
        qt, head = divmod(unit, N_HEADS)
        pair, parity = divmod(head, 2)
        kvi = 2 * ((2 * pair) // (N_HEADS // N_KV_HEADS)) + parity
        return slice(qt * tq, (qt + 1) * tq), slice(pair * LANES, (pair + 1) * LANES), parity, kvi

    def scores(unit, c, mrun):
        qrows, lanes, _, kvi = unit_parts(unit)
        rows = slice(c * KEY_CHUNK, (c + 1) * KEY_CHUNK)
        s = _dot_nt(kz_ref[kvi, rows, :], q_ref[qrows, lanes])
        s_ref[slot0 + unit % 2, rows, :] = s
        m = jnp.max(s.reshape(KEY_CHUNK // sub, sub, tq), axis=0)
        return m if mrun is None else jnp.maximum(mrun, m)

    def weighted(unit, c, m8, acc):
        kvi = unit_parts(unit)[3]
        rows = slice(c * KEY_CHUNK, (c + 1) * KEY_CHUNK)
        s3 = s_ref[slot0 + unit % 2, rows, :].reshape(KEY_CHUNK // sub, sub, tq)
        e = jnp.exp2(s3 - m8[None]).reshape(KEY_CHUNK, tq).astype(BF16)
        d = _dot(vzt_ref[kvi, :, rows], e)
        return d if acc is None else acc + d

    nunit = (q_ref.shape[0] // tq) * N_HEADS
    top = lax.broadcasted_iota(jnp.int32, (LANES, tq), 0) < HEAD_DIM
    mrun = None
    for c in range(nchunk):
        mrun = scores(0, c, mrun)
    even = None
    for unit in range(nunit):
        m8 = jnp.broadcast_to(jnp.max(mrun, axis=0, keepdims=True), (sub, tq))
        mnext, acc = None, None
        for c in range(nchunk):
            if unit + 1 < nunit:
                mnext = scores(unit + 1, c, mnext)
            acc = weighted(unit, c, m8, acc)
        mrun = mnext
        qrows, lanes, parity, _ = unit_parts(unit)
        if parity == 0:
            even = acc * (1.0 / acc[HEAD_DIM:HEAD_DIM + 1])
        else:
            odd = acc * (1.0 / acc[0:1])
            o_ref[qrows, lanes] = jnp.where(top, even, odd).T.astype(BF16)


def _attention(q, k, v, B, S, tq):
    T = q.shape[0]
    nq = S // tq
    return pl.pallas_call(
        _attn_kernel,
        grid=(B, nq),
        in_specs=[
            pl.BlockSpec((tq, Q_W), lambda b, i: (b * nq + i, 0)),
            pl.BlockSpec((S, KV_W), lambda b, i: (b, 0)),
            pl.BlockSpec((S, KV_W), lambda b, i: (b, 0)),
        ],
        out_specs=pl.BlockSpec((tq, Q_W), lambda b, i: (b * nq + i, 0)),
        out_shape=jax.ShapeDtypeStruct((T, Q_W), BF16),
        scratch_shapes=[pltpu.VMEM((4, S, KV_W), BF16), pltpu.VMEM((4, KV_W, S), BF16),
                        pltpu.VMEM((2, S, QUERY_SUB), F32)],
        compiler_params=_cparams("parallel", "arbitrary"),
        name="attention",
    )(q, k, v)


def _merge_kernel(x_ref, gmix_ref, wgl_ref, attn_ref, u_ref, vn_ref, p_ref, pprev_ref, pnext_ref,
                  wsp_ref, bsp_ref, wpool_ref, pscale_ref, wao_ref, wso_ref, wpo_ref, wout_ref,
                  gffn_ref, wr_ref, x1_ref, h2_ref, lg_ref, *, S):
    ts = x_ref.shape[0]

    vn = vn_ref[...]
    lane_b = lax.broadcasted_iota(jnp.int32, (SGU_CHUNK, SGU_W), 1) // SGU_HEAD
    zs = []
    for c in range(ts // SGU_CHUNK):
        vc = vn[c * SGU_CHUNK:(c + 1) * SGU_CHUNK]
        stacked = jnp.concatenate(
            [jnp.where(lane_b == g, vc, jnp.zeros_like(vc)) for g in range(SGU_GROUPS)], axis=0)
        zs.append(_dot(wsp_ref[...], stacked) + bsp_ref[...])
    sgu = (u_ref[...] * jnp.concatenate(zs, axis=0)).astype(BF16)

    pos0 = (pl.program_id(0) % (S // ts)) * ts
    p = p_ref[...]
    prev = jnp.where(pos0 == 0, 0.0, pprev_ref[...])
    nxt = jnp.where(pos0 + ts == S, 0.0, pnext_ref[...])
    ext = jnp.concatenate([prev, p, nxt], axis=0)
    n = ts + 2 * POOL_HALO
    s2 = ext + pltpu.roll(ext, 1, 0)
    s4 = pltpu.roll(s2, 1, 0) + pltpu.roll(s2, n - 1, 0)
    s8 = pltpu.roll(s4, 2, 0) + pltpu.roll(s4, n - 2, 0)
    s16 = pltpu.roll(s8, 4, 0) + pltpu.roll(s8, n - 4, 0)
    mid = slice(POOL_HALO, POOL_HALO + ts)
    grp = lax.broadcasted_iota(jnp.int32, (ts, POOL_W), 1) // POOL_HEAD
    t = pos0 + lax.broadcasted_iota(jnp.int32, (ts, POOL_W), 0)
    half = jnp.left_shift(1, grp)
    cnt = jnp.minimum(t + half - 1, S - 1) - jnp.maximum(t - half, 0) + 1
    wsum = jnp.where(grp == 0, s2[mid], jnp.where(grp == 1, s4[mid],
                                                  jnp.where(grp == 2, s8[mid], s16[mid])))
    d = (wsum / cnt.astype(F32) - p).astype(BF16)
    pooled = (_dot(d, wpool_ref[...]) * pscale_ref[...]).astype(BF16)

    x = x_ref[...]
    D = x.shape[1]
    h = (_rms(x) * gmix_ref[...]).astype(BF16)
    br_a = _dot(attn_ref[...], wao_ref[...])
    br_b = _dot(sgu, wso_ref[...])
    br_c = _dot(pooled, wpo_ref[...])
    gl = _dot(h, wgl_ref[...])
    gate = lambda j: 0.5 * (1.0 + jnp.tanh(0.5 * gl[:, j * D:(j + 1) * D]))
    merged = (gate(0) * br_a + gate(1) * br_b + gate(2) * br_c).astype(BF16)
    x1 = x + _dot(merged, wout_ref[...])
    x1_ref[...] = x1
    h2 = (_rms(x1) * gffn_ref[...]).astype(BF16)
    h2_ref[...] = h2
    lg_ref[...] = _dot(h2, wr_ref[...])


def _merge(x2, gmix, w_gl, attn, u, vn, p, wsp, bsp, wpool, pscale, wao, wso, wpo, wout, gffn, wr,
           S, ts):
    T, D = x2.shape
    row = lambda i: (i, 0)
    const = lambda i: (0, 0)
    hb = ts // POOL_HALO
    nhb = T // POOL_HALO
    full = lambda a: pl.BlockSpec(a.shape, const)
    return pl.pallas_call(
        functools.partial(_merge_kernel, S=S),
        grid=(T // ts,),
        in_specs=[
            pl.BlockSpec((ts, D), row),
            full(gmix),
            full(w_gl),
            pl.BlockSpec((ts, Q_W), row),
            pl.BlockSpec((ts, SGU_W), row),
            pl.BlockSpec((ts, SGU_W), row),
            pl.BlockSpec((ts, POOL_W), row),
            pl.BlockSpec((POOL_HALO, POOL_W), lambda i: (jnp.maximum(i * hb - 1, 0), 0)),
            pl.BlockSpec((POOL_HALO, POOL_W), lambda i: (jnp.minimum((i + 1) * hb, nhb - 1), 0)),
            full(wsp), full(bsp), full(wpool), full(pscale), full(wao), full(wso), full(wpo),
            full(wout), full(gffn), full(wr),
        ],
        out_specs=[
            pl.BlockSpec((ts, D), row),
            pl.BlockSpec((ts, D), row),
            pl.BlockSpec((ts, LANES), row),
        ],
        out_shape=[
            jax.ShapeDtypeStruct((T, D), F32),
            jax.ShapeDtypeStruct((T, D), BF16),
            jax.ShapeDtypeStruct((T, LANES), F32),
        ],
        compiler_params=_cparams("parallel"),
        name="merge",
    )(x2, gmix, w_gl, attn, u, vn, p, p, p, wsp, bsp, wpool, pscale, wao, wso, wpo, wout, gffn, wr)


def _excl_cumsum(m01, tri_ones):
    off = jnp.zeros((m01.shape[0], LANES), F32)
    pieces = []
    for j in range(m01.shape[1] // LANES):
        res = _dot(m01[:, j * LANES:(j + 1) * LANES].astype(BF16), tri_ones)
        pieces.append(res[:, :LANES] + off)
        off = off + res[:, LANES:]
    return jnp.concatenate(pieces, axis=1)


def _tile_lists(offs, nblk, list_len):
    E = offs.shape[0]
    lane = lax.broadcasted_iota(jnp.int32, offs.shape, 1)
    hi = pltpu.roll(offs, LANES - 1, 1)
    g_lo = jnp.floor(offs * (1.0 / SLOT_TILE))
    g_hi = jnp.floor((hi + (SLOT_TILE - 1)) * (1.0 / SLOT_TILE))
    cnt = jnp.where((hi > offs) & (lane < nblk), g_hi - g_lo, 0.0)
    below = (lax.broadcasted_iota(jnp.int32, (E, E), 1) < lax.broadcasted_iota(jnp.int32, (E, E), 0))
    base = _dot(jnp.where(below, 1.0, 0.0).astype(BF16), cnt.astype(BF16))
    end = base + cnt
    t = lax.broadcasted_iota(jnp.int32, (1, list_len), 1).astype(F32)
    row_e = lax.broadcasted_iota(jnp.int32, (E, list_len), 0).astype(F32)
    blocks = []
    for j in range(nblk):
        col = slice(j, j + 1)
        e_t = jnp.sum(jnp.where(end[:, col] <= t, 1.0, 0.0), axis=0, keepdims=True)
        shift = jnp.sum(jnp.where(row_e == e_t, g_lo[:, col] - base[:, col], 0.0), axis=0, keepdims=True)
        valid = e_t < E
        g_t = jnp.where(valid, t + shift, 0.0)
        total = jnp.broadcast_to(end[E - 1:E, col], (1, list_len))
        rows = [jnp.where(valid, e_t, 0.0), g_t, jnp.where(valid, g_t * SLOT_TILE, float(NO_SLOT)),
                total, jnp.zeros((4, list_len), F32)]
        blocks.append(jnp.concatenate(rows, axis=0).astype(jnp.int32)[None])
    return jnp.concatenate(blocks, axis=0)


def _route_kernel(lg_ref, tri_ref, rank_ref, aff_ref, lists_ref, *, cap):
    nb, S, _ = lg_ref.shape
    E = N_EXPERTS
    rows = []
    for b in range(nb):
        lt = lg_ref[b].T[:E]
        e = jnp.exp(lt - jnp.max(lt, axis=0, keepdims=True))
        rows.append(e / jnp.sum(e, axis=0, keepdims=True))
    aff = jnp.concatenate(rows, axis=0)
    bits = pltpu.bitcast(aff, jnp.int32)

    def refine(i, thr):
        cand = thr | jnp.left_shift(jnp.int32(1), 30 - i)
        cnt = jnp.sum(jnp.where(bits >= cand, 1.0, 0.0), axis=1, keepdims=True)
        return jnp.where(cnt >= cap, cand, thr)

    thr = lax.fori_loop(0, 31, refine, jnp.zeros((nb * E, 1), jnp.int32))
    gt = jnp.where(bits > thr, 1.0, 0.0)
    eq = jnp.where(bits == thr, 1.0, 0.0)
    need = cap - jnp.sum(gt, axis=1, keepdims=True)
    tri = tri_ref[...]
    sel = gt + eq * jnp.where(_excl_cumsum(eq, tri) < need, 1.0, 0.0)
    before = _excl_cumsum(sel, tri)
    rank = jnp.where(sel > 0.5, before, -1.0)
    nblk = S // TOKEN_BLOCK
    lane = lax.broadcasted_iota(jnp.int32, (nb * E, LANES), 1)
    offs = jnp.where(lane == nblk, float(cap), 0.0)
    for j in range(nblk):
        offs = jnp.where(lane == j, before[:, j * TOKEN_BLOCK:j * TOKEN_BLOCK + 1], offs)
    list_len = lists_ref.shape[-1]
    for b in range(nb):
        rank_ref[b] = rank[b * E:(b + 1) * E]
        aff_ref[b] = aff[b * E:(b + 1) * E]
        lists_ref[b] = _tile_lists(offs[b * E:(b + 1) * E], nblk, list_len)


def _list_len(cap):
    worst = N_EXPERTS * (cap // SLOT_TILE + 2)
    return -(-worst // LANES) * LANES


def _route(lg3, tri, cap, nb):
    B, S, _ = lg3.shape
    nblk = S // TOKEN_BLOCK
    blk = lambda b: (b, 0, 0)
    lists_shape = (nblk, 8, _list_len(cap))
    return pl.pallas_call(
        functools.partial(_route_kernel, cap=cap),
        grid=(B // nb,),
        in_specs=[pl.BlockSpec((nb, S, LANES), blk), pl.BlockSpec(tri.shape, lambda b: (0, 0))],
        out_specs=[
            pl.BlockSpec((nb, N_EXPERTS, S), blk),
            pl.BlockSpec((nb, N_EXPERTS, S), blk),
            pl.BlockSpec((nb,) + lists_shape, lambda b: (b, 0, 0, 0)),
        ],
        out_shape=[
            jax.ShapeDtypeStruct((B, N_EXPERTS, S), F32),
            jax.ShapeDtypeStruct((B, N_EXPERTS, S), F32),
            jax.ShapeDtypeStruct((B,) + lists_shape, jnp.int32),
        ],
        compiler_params=_cparams("parallel"),
        name="route",
    )(lg3, tri)


def _chunk_tiles(list_ref, k, c):
    out = []
    for i in range(CHUNK_TILES):
        t = c * CHUNK_TILES + i
        out.append((list_ref[k, 0, t], list_ref[k, 1, t], list_ref[k, 2, t]))
    return out


def _num_chunks(list_ref, k):
    return jnp.right_shift(list_ref[k, 3, 0] + CHUNK_TILES - 1, CHUNK_SHIFT)


def _for_each_chunk(list_ref, body):
    for k in range(list_ref.shape[0]):
        body(k, 0, True)
    for k in range(list_ref.shape[0]):
        def rest(c, carry, k=k):
            body(k, c, False)
            return carry
        lax.fori_loop(1, _num_chunks(list_ref, k), rest, 0)


def _tile_rows(g_t):
    return pl.ds(pl.multiple_of(g_t * SLOT_TILE, SLOT_TILE), SLOT_TILE)


def _onehot(entries, rank_ref, j, weight_ref=None):
    tb = rank_ref.shape[-1]
    sub = lax.broadcasted_iota(jnp.int32, (SLOT_TILE, tb), 0)
    tiles = []
    for e_t, _, slot0 in entries:
        hit = rank_ref[e_t, pl.ds(j, 1), :] == (slot0 + sub).astype(F32)
        val = 1.0 if weight_ref is None else weight_ref[e_t, pl.ds(j, 1), :]
        tiles.append(jnp.where(hit, val, 0.0).astype(BF16))
    return jnp.concatenate(tiles, axis=0)


def _gather_kernel(list_ref, h2_ref, rank_ref, xs_ref):
    tb = rank_ref.shape[-1]
    step = pl.program_id(1)

    @pl.when(step == 0)
    def _():
        xs_ref[...] = jnp.zeros_like(xs_ref)

    def chunk(k, c, first):
        entries = _chunk_tiles(list_ref, k, c)
        onehot = _onehot(entries, rank_ref, step * list_ref.shape[0] + k)
        rows = _dot(onehot, h2_ref[k * tb:(k + 1) * tb, :]).astype(BF16)
        for i, (e_t, g_t, _) in enumerate(entries):
            xs_ref[e_t, _tile_rows(g_t), :] += rows[i * SLOT_TILE:(i + 1) * SLOT_TILE, :]

    _for_each_chunk(list_ref, chunk)


def _list_spec(lists, nblocks):
    return pl.BlockSpec((None, nblocks) + lists.shape[2:], lambda b, j: (b, j, 0, 0),
                        memory_space=pltpu.SMEM)


def _gather(h2, rank4, lists, cap):
    B, S, D = h2.shape
    _, E, nblk, tb = rank4.shape
    rows = BLOCKS_PER_STEP * tb
    return pl.pallas_call(
        _gather_kernel,
        grid=(B, nblk // BLOCKS_PER_STEP),
        in_specs=[
            _list_spec(lists, BLOCKS_PER_STEP),
            pl.BlockSpec((None, rows, D), lambda b, j: (b, j, 0)),
            pl.BlockSpec((None, E, nblk, tb), lambda b, j: (b, 0, 0, 0)),
        ],
        out_specs=pl.BlockSpec((None, E, cap, D), lambda b, j: (b, 0, 0, 0)),
        out_shape=jax.ShapeDtypeStruct((B, E, cap, D), BF16),
        compiler_params=_cparams("arbitrary", "arbitrary"),
        name="gather",
    )(lists, h2, rank4)


def _ffn_kernel(xs_ref, wg_ref, wu_ref, wd_ref, y_ref, wg_b, wu_b, wd_b):
    @pl.when(pl.program_id(1) == 0)
    def _():
        wg_b[...] = wg_ref[...].astype(BF16)
        wu_b[...] = wu_ref[...].astype(BF16)
        wd_b[...] = wd_ref[...].astype(BF16)

    nb, cap, D = xs_ref.shape
    xs = xs_ref[...].reshape(nb * cap, D)
    a = _dot(xs, wg_b[...])
    b = _dot(xs, wu_b[...])
    hidden = (a * jax.nn.sigmoid(a) * b).astype(BF16)
    y_ref[...] = _dot(hidden, wd_b[...]).astype(BF16).reshape(nb, cap, D)


def _ffn(xs, wg, wu, wd, layer, nb):
    B, E, cap, D = xs.shape
    F = wg.shape[3]
    blk = pl.BlockSpec((nb, None, cap, D), lambda e, b: (b, e, 0, 0))
    return pl.pallas_call(
        _ffn_kernel,
        grid=(E, B // nb),
        in_specs=[
            blk,
            pl.BlockSpec((None, None, D, F), lambda e, b: (layer, e, 0, 0)),
            pl.BlockSpec((None, None, D, F), lambda e, b: (layer, e, 0, 0)),
            pl.BlockSpec((None, None, F, D), lambda e, b: (layer, e, 0, 0)),
        ],
        out_specs=blk,
        out_shape=jax.ShapeDtypeStruct((B, E, cap, D), BF16),
        scratch_shapes=[pltpu.VMEM((D, F), BF16), pltpu.VMEM((D, F), BF16), pltpu.VMEM((F, D), BF16)],
        compiler_params=_cparams("arbitrary", "arbitrary"),
        name="ffn",
    )(xs, wg, wu, wd)


def _scatter_rows(list_ref, y_ref, rank_ref, aff_ref, x1_ref, out_ref):
    tb = rank_ref.shape[-1]
    step = pl.program_id(1)

    def chunk(k, c, first):
        entries = _chunk_tiles(list_ref, k, c)
        weights = _onehot(entries, rank_ref, step * list_ref.shape[0] + k, aff_ref)
        ys = jnp.concatenate([y_ref[e_t, _tile_rows(g_t), :] for e_t, g_t, _ in entries], axis=0)
        add = lax.dot_general(weights, ys, (((0,), (0,)), ((), ())), preferred_element_type=F32)
        rows = slice(k * tb, (k + 1) * tb)
        out_ref[rows, :] = add + (x1_ref[rows, :] if first else out_ref[rows, :])

    _for_each_chunk(list_ref, chunk)


def _scatter_specs(y, rank4, lists, nblocks):
    B, E, cap, D = y.shape
    _, _, nblk, tb = rank4.shape
    steps = nblk // nblocks
    rows = pl.BlockSpec((nblocks * tb, D), lambda b, j: (b * steps + j, 0))
    per_seq = pl.BlockSpec((None, E, nblk, tb), lambda b, j: (b, 0, 0, 0))
    in_specs = [_list_spec(lists, nblocks), pl.BlockSpec((None, E, cap, D), lambda b, j: (b, 0, 0, 0)),
                per_seq, per_seq, rows]
    return (B, steps), in_specs, rows


def _scatter_kernel(list_ref, y_ref, rank_ref, aff_ref, x1_ref, gout_ref, out_ref, *, norm_out):
    _scatter_rows(list_ref, y_ref, rank_ref, aff_ref, x1_ref, out_ref)
    if norm_out:
        out_ref[...] = _rms(out_ref[...]) * gout_ref[...]


def _scatter(y, rank4, aff4, x1, lists, gout, norm_out):
    grid, in_specs, rows = _scatter_specs(y, rank4, lists, BLOCKS_PER_STEP)
    return pl.pallas_call(
        functools.partial(_scatter_kernel, norm_out=norm_out),
        grid=grid,
        in_specs=in_specs + [pl.BlockSpec(gout.shape, lambda b, j: (0, 0))],
        out_specs=rows,
        out_shape=jax.ShapeDtypeStruct(x1.shape, F32),
        compiler_params=_cparams("arbitrary", "arbitrary"),
        name="scatter",
    )(lists, y, rank4, aff4, x1, gout)


def _rope_tables(S, g, scale):
    rows = S // GRID_W
    row = jnp.broadcast_to(jnp.arange(rows, dtype=F32)[:, None], (rows, GRID_W)).reshape(-1)
    col = jnp.broadcast_to(jnp.arange(GRID_W, dtype=F32)[None, :], (rows, GRID_W)).reshape(-1)
    part = HEAD_DIM // 2
    freqs = ROPE_THETA ** (-jnp.arange(0, part, 2, dtype=F32) / part)
    ang = jnp.stack([row[:, None] * freqs, col[:, None] * freqs], axis=1)
    cos, sin = jnp.cos(ang), jnp.sin(ang)
    c64 = jnp.concatenate([cos[:, 0], cos[:, 0], cos[:, 1], cos[:, 1]], axis=-1)
    s64 = jnp.concatenate([-sin[:, 0], sin[:, 0], -sin[:, 1], sin[:, 1]], axis=-1)
    g_swapped = g.reshape(2, 2, HEAD_DIM // 4)[:, ::-1, :].reshape(HEAD_DIM)
    gc = jnp.tile(c64 * g[None, :], (1, 2)) * scale
    gs = jnp.tile(s64 * g_swapped[None, :], (1, 2)) * scale
    return gc, gs


def _block_diag(blocks):
    n = blocks.shape[0]
    m = blocks.shape[1]
    eye = jnp.eye(n, dtype=blocks.dtype)
    return (eye[:, None, :, None] * blocks[:, :, None, :]).reshape(n * m, n * blocks.shape[2])


def kernel(x, g_mix, w_in, g_q, g_k, g_sgu, w_spatial, b_spatial, w_pool, pool_scale, w_attn_o,
           w_sgu_o, w_pool_o, w_out, g_ffn, w_router, w_gate_e, w_up_e, w_down_e, g_final):
    B, S, D = x.shape
    depth = w_in.shape[0]
    cap = CAPACITY_FACTOR * S // N_EXPERTS
    x2 = x.reshape(B * S, D)

    bd = _block_diag(jnp.ones((2, HEAD_DIM, HEAD_DIM), F32)).astype(BF16)
    k_i = lax.broadcasted_iota(jnp.int32, (LANES, LANES), 0)
    l_i = lax.broadcasted_iota(jnp.int32, (LANES, LANES), 1)
    tri = jnp.concatenate([(k_i < l_i).astype(BF16), jnp.ones((LANES, LANES), BF16)], axis=1)

    for l in range(depth):
        gcq, gsq = _rope_tables(S, g_q[l], HEAD_DIM ** -0.5 * math.log2(math.e))
        gck, gsk = _rope_tables(S, g_k[l], 1.0)
        w_gl = w_in[l, :, A_W:].astype(BF16)
        gmix = g_mix[l][None, :]
        proj = (gmix, w_in[l, :, :A_W].astype(BF16), gcq, gsq, gck, gsk, bd, g_sgu[l][None, :])
        q, k, v, u, vn, p = _inproj(x2, proj, S, INPROJ_ROWS)
        attn = _attention(q, k, v, B, S, ATTN_QUERY_SUBS * QUERY_SUB)

        wsp = w_spatial[l].transpose(1, 0, 2).reshape(SGU_CHUNK, SGU_GROUPS * SGU_CHUNK).astype(BF16)
        bsp = jnp.repeat(b_spatial[l].T, SGU_HEAD, axis=1)
        wpool = _block_diag(w_pool[l]).astype(BF16)
        wr = jnp.pad(w_router[l], ((0, 0), (0, LANES - N_EXPERTS))).astype(BF16)
        x1, h2, lg = _merge(
            x2, gmix, w_gl, attn, u, vn, p, wsp, bsp, wpool, pool_scale[l][None, :],
            w_attn_o[l].astype(BF16), w_sgu_o[l].astype(BF16), w_pool_o[l].astype(BF16),
            w_out[l].astype(BF16), g_ffn[l][None, :], wr, S, MERGE_ROWS)

        rank, aff, lists = _route(lg.reshape(B, S, LANES), tri, cap, math.gcd(B, SEQS_PER_STEP))
        per_block = (B, N_EXPERTS, S // TOKEN_BLOCK, TOKEN_BLOCK)
        rank4, aff4 = rank.reshape(per_block), aff.reshape(per_block)
        xs = _gather(h2.reshape(B, S, D), rank4, lists, cap)
        y = _ffn(xs, w_gate_e, w_up_e, w_down_e, l, math.gcd(B, SEQS_PER_STEP))
        x2 = _scatter(y, rank4, aff4, x1, lists, g_final[None, :], norm_out=(l == depth - 1))

    return x2.reshape(B, S, D)
```

```python
import functools
import math

import jax
import jax.numpy as jnp
from jax import lax
from jax.experimental import pallas as pl
from jax.experimental.pallas import tpu as pltpu

F32 = jnp.float32
BF16 = jnp.bfloat16

GRID_W = 64
N_HEADS = 8
N_KV_HEADS = 2
HEAD_DIM = 64
ROPE_THETA = 10000.0
SGU_GROUPS = 4
SGU_HEAD = 64
SGU_W = SGU_GROUPS * SGU_HEAD
SGU_CHUNK = 128
POOL_WINDOWS = (2, 4, 8, 16)
POOL_HEAD = 64
POOL_W = len(POOL_WINDOWS) * POOL_HEAD
POOL_HALO = 8
KEY_CHUNK = 512
QUERY_SUB = 512
TOKEN_BLOCK = 256
BLOCKS_PER_STEP = 4
INPROJ_ROWS = 1024
MERGE_ROWS = 512
ATTN_QUERY_SUBS = 2
SEQS_PER_STEP = 4
TILE_SHIFT = 4
SLOT_TILE = 1 << TILE_SHIFT
CHUNK_SHIFT = 6
CHUNK_TILES = 1 << CHUNK_SHIFT
NO_SLOT = -1024
Q_W = N_HEADS * HEAD_DIM
KV_W = N_KV_HEADS * HEAD_DIM
N_BRANCHES = 3
N_EXPERTS = 16
CAPACITY_FACTOR = 2
EPS = 1e-6

LANES = 128
F32_SUBLANES = 8
A_W = Q_W + 2 * KV_W + 2 * SGU_W + POOL_W
VMEM_LIMIT = 56 * 1024 * 1024


def _cparams(*sem):
    return pltpu.CompilerParams(dimension_semantics=sem, vmem_limit_bytes=VMEM_LIMIT)


def _dot(a, b):
    return jnp.dot(a, b, preferred_element_type=F32)


def _dot_nt(a, b):
    return lax.dot_general(a, b, (((1,), (1,)), ((), ())), preferred_element_type=F32)


def _rms(x):
    return x * lax.rsqrt(jnp.mean(x * x, axis=-1, keepdims=True) + EPS)


def _head_norm_rope(xq, gc, gs, bd):
    sq = xq * xq
    hi = sq.astype(BF16)
    lo = (sq - hi.astype(F32)).astype(BF16)
    ss = _dot(hi, bd) + _dot(lo, bd)
    r = lax.rsqrt(ss * (1.0 / HEAD_DIM) + EPS)
    lane = lax.broadcasted_iota(jnp.int32, xq.shape, 1)
    first_half = (lane % 32) < 16
    partner = jnp.where(first_half, pltpu.roll(xq, LANES - 16, 1), pltpu.roll(xq, 16, 1))
    return r * (xq * gc + partner * gs)


def _project_rows(x, gmix_ref, w_ref, gcq_ref, gsq_ref, gck_ref, gsk_ref, bd_ref, gsgu_ref,
                  q_ref, k_ref, v_ref, u_ref, vn_ref, p_ref):
    h = (_rms(x) * gmix_ref[...]).astype(BF16)
    acc = _dot(h, w_ref[...])
    bd = bd_ref[...]
    for c in range(Q_W // LANES):
        sl = slice(c * LANES, (c + 1) * LANES)
        q_ref[:, sl] = _head_norm_rope(acc[:, sl], gcq_ref[...], gsq_ref[...], bd).astype(BF16)
    o = Q_W
    k_ref[...] = _head_norm_rope(acc[:, o:o + KV_W], gck_ref[...], gsk_ref[...], bd).astype(BF16)
    o += KV_W
    v_ref[...] = acc[:, o:o + KV_W].astype(BF16)
    o += KV_W
    u_ref[...] = acc[:, o:o + SGU_W]
    o += SGU_W
    vn_ref[...] = (_rms(acc[:, o:o + SGU_W]) * gsgu_ref[...]).astype(BF16)
    o += SGU_W
    p_ref[...] = acc[:, o:o + POOL_W]


def _stream(refs, n_terms):
    x = refs[0][...]
    for r in refs[1:n_terms]:
        x = x + r[...]
    return x


def _inproj_kernel(*refs, n_terms):
    _project_rows(_stream(refs, n_terms), *refs[n_terms:])


def _project_specs(T, D, tm, row, pos, const):
    in_specs = [
        pl.BlockSpec((1, D), const),
        pl.BlockSpec((D, A_W), const),
        pl.BlockSpec((tm, LANES), pos),
        pl.BlockSpec((tm, LANES), pos),
        pl.BlockSpec((tm, LANES), pos),
        pl.BlockSpec((tm, LANES), pos),
        pl.BlockSpec((LANES, LANES), const),
        pl.BlockSpec((1, SGU_W), const),
    ]
    widths = ((Q_W, BF16), (KV_W, BF16), (KV_W, BF16), (SGU_W, F32), (SGU_W, BF16), (POOL_W, F32))
    out_specs = [pl.BlockSpec((tm, w), row) for w, _ in widths]
    out_shape = [jax.ShapeDtypeStruct((T, w), dt) for w, dt in widths]
    return in_specs, out_specs, out_shape


def _inproj(stream, params, S, tm):
    T, D = stream[0].shape
    nseq = S // tm
    row = lambda i: (i, 0)
    in_specs, out_specs, out_shape = _project_specs(
        T, D, tm, row, lambda i: (i % nseq, 0), lambda i: (0, 0))
    return pl.pallas_call(
        functools.partial(_inproj_kernel, n_terms=len(stream)),
        grid=(T // tm,),
        in_specs=[pl.BlockSpec((tm, D), row)] * len(stream) + in_specs,
        out_specs=out_specs,
        out_shape=out_shape,
        compiler_params=_cparams("parallel"),
        name="inproj",
    )(*stream, *params)


def _attn_kernel(q_ref, k_ref, v_ref, o_ref, kz_ref, vzt_ref, s_ref, qt_ref):
    @pl.when(pl.program_id(1) == 0)
    def _():
        lane = lax.broadcasted_iota(jnp.int32, k_ref.shape, 1)
        low = lane < HEAD_DIM
        k = k_ref[...].astype(F32)
        kr = pltpu.roll(k, HEAD_DIM, 1)
        kz_ref[0] = jnp.where(low, k, 0.0).astype(BF16)
        kz_ref[1] = jnp.where(low, 0.0, kr).astype(BF16)
        kz_ref[2] = jnp.where(low, kr, 0.0).astype(BF16)
        kz_ref[3] = jnp.where(low, 0.0, k).astype(BF16)
        vt = v_ref[...].astype(F32).T
        vtr = pltpu.roll(vt, HEAD_DIM, 0)
        row = lax.broadcasted_iota(jnp.int32, vt.shape, 0)
        top = row < HEAD_DIM
        ones_lo = jnp.where(row == 0, 1.0, 0.0)
        ones_hi = jnp.where(row == HEAD_DIM, 1.0, 0.0)
        vzt_ref[0] = jnp.where(top, vt, ones_hi).astype(BF16)
        vzt_ref[1] = jnp.where(top, ones_lo, vtr).astype(BF16)
        vzt_ref[2] = jnp.where(top, vtr, ones_hi).astype(BF16)
        vzt_ref[3] = jnp.where(top, ones_lo, vt).astype(BF16)

    tq = s_ref.shape[2]
    S = k_ref.shape[0]
    nchunk = S // KEY_CHUNK
    sub = F32_SUBLANES
    slot0 = jnp.minimum(pl.program_id(1), 0)

    def unit_parts(unit):
        qt, head = divmod(unit, N_HEADS)
        pair, parity = divmod(head, 2)
        kvi = 2 * ((2 * pair) // (N_HEADS // N_KV_HEADS)) + parity
        return slice(qt * tq, (qt + 1) * tq), slice(pair * LANES, (pair + 1) * LANES), parity, kvi

    npair = N_HEADS // 2
    for qt in range(q_ref.shape[0] // tq):
        for pair in range(npair):
            q_pair = q_ref[qt * tq:(qt + 1) * tq, pair * LANES:(pair + 1) * LANES]
            qt_ref[qt * npair + pair] = q_pair.astype(F32).T.astype(BF16)

    def scores(unit, c, mrun):
        kvi = unit_parts(unit)[3]
        rows = slice(c * KEY_CHUNK, (c + 1) * KEY_CHUNK)
        s = _dot(kz_ref[kvi, rows, :], qt_ref[unit // 2])
        s_ref[slot0 + unit % 2, rows, :] = s
        m = jnp.max(s.reshape(KEY_CHUNK // sub, sub, tq), axis=0)
        return m if mrun is None else jnp.maximum(mrun, m)

    def weighted(unit, c, m8, acc):
        kvi = unit_parts(unit)[3]
        rows = slice(c * KEY_CHUNK, (c + 1) * KEY_CHUNK)
        s3 = s_ref[slot0 + unit % 2, rows, :].reshape(KEY_CHUNK // sub, sub, tq)
        e = jnp.exp2(s3 - m8[None]).reshape(KEY_CHUNK, tq).astype(BF16)
        d = _dot(vzt_ref[kvi, :, rows], e)
        return d if acc is None else acc + d

    nunit = (q_ref.shape[0] // tq) * N_HEADS
    top = lax.broadcasted_iota(jnp.int32, (LANES, tq), 0) < HEAD_DIM
    mrun = None
    for c in range(nchunk):
        mrun = scores(0, c, mrun)
    even = None
    for unit in range(nunit):
        m8 = jnp.broadcast_to(jnp.max(mrun, axis=0, keepdims=True), (sub, tq))
        mnext, acc = None, None
        for c in range(nchunk):
            if unit + 1 < nunit:
                mnext = scores(unit + 1, c, mnext)
            acc = weighted(unit, c, m8, acc)
        mrun = mnext
        qrows, lanes, parity, _ = unit_parts(unit)
        if parity == 0:
            even = acc * (1.0 / acc[HEAD_DIM:HEAD_DIM + 1])
        else:
            odd = acc * (1.0 / acc[0:1])
            o_ref[qrows, lanes] = jnp.where(top, even, odd).T.astype(BF16)


def _attention(q, k, v, B, S, tq):
    T = q.shape[0]
    nq = S // tq
    return pl.pallas_call(
        _attn_kernel,
        grid=(B, nq),
        in_specs=[
            pl.BlockSpec((tq, Q_W), lambda b, i: (b * nq + i, 0)),
            pl.BlockSpec((S, KV_W), lambda b, i: (b, 0)),
            pl.BlockSpec((S, KV_W), lambda b, i: (b, 0)),
        ],
        out_specs=pl.BlockSpec((tq, Q_W), lambda b, i: (b * nq + i, 0)),
        out_shape=jax.ShapeDtypeStruct((T, Q_W), BF16),
        scratch_shapes=[pltpu.VMEM((4, S, KV_W), BF16), pltpu.VMEM((4, KV_W, S), BF16),
                        pltpu.VMEM((2, S, QUERY_SUB), F32),
                        pltpu.VMEM((tq // QUERY_SUB * (N_HEADS // 2), LANES, QUERY_SUB), BF16)],
        compiler_params=_cparams("parallel", "arbitrary"),
        name="attention",
    )(q, k, v)


def _merge_kernel(*refs, S, n_terms):
    (gmix_ref, wgl_ref, attn_ref, u_ref, vn_ref, p_ref, pprev_ref, pnext_ref, wsp_ref, bsp_ref,
     wpool_ref, pscale_ref, wao_ref, wso_ref, wpo_ref, wout_ref, gffn_ref, wr_ref,
     x1_ref, h2_ref, lg_ref) = refs[n_terms:]
    ts = x1_ref.shape[0]

    vn = vn_ref[...]
    lane_b = lax.broadcasted_iota(jnp.int32, (SGU_CHUNK, SGU_W), 1) // SGU_HEAD
    zs = []
    for c in range(ts // SGU_CHUNK):
        vc = vn[c * SGU_CHUNK:(c + 1) * SGU_CHUNK]
        stacked = jnp.concatenate(
            [jnp.where(lane_b == g, vc, jnp.zeros_like(vc)) for g in range(SGU_GROUPS)], axis=0)
        zs.append(_dot(wsp_ref[...], stacked) + bsp_ref[...])
    sgu = (u_ref[...] * jnp.concatenate(zs, axis=0)).astype(BF16)

    pos0 = (pl.program_id(0) % (S // ts)) * ts
    p = p_ref[...]
    prev = jnp.where(pos0 == 0, 0.0, pprev_ref[...])
    nxt = jnp.where(pos0 + ts == S, 0.0, pnext_ref[...])
    ext = jnp.concatenate([prev, p, nxt], axis=0)
    n = ts + 2 * POOL_HALO
    s2 = ext + pltpu.roll(ext, 1, 0)
    s4 = pltpu.roll(s2, 1, 0) + pltpu.roll(s2, n - 1, 0)
    s8 = pltpu.roll(s4, 2, 0) + pltpu.roll(s4, n - 2, 0)
    s16 = pltpu.roll(s8, 4, 0) + pltpu.roll(s8, n - 4, 0)
    mid = slice(POOL_HALO, POOL_HALO + ts)
    grp = lax.broadcasted_iota(jnp.int32, (ts, POOL_W), 1) // POOL_HEAD
    t = pos0 + lax.broadcasted_iota(jnp.int32, (ts, POOL_W), 0)
    half = jnp.left_shift(1, grp)
    cnt = jnp.minimum(t + half - 1, S - 1) - jnp.maximum(t - half, 0) + 1
    wsum = jnp.where(grp == 0, s2[mid], jnp.where(grp == 1, s4[mid],
                                                  jnp.where(grp == 2, s8[mid], s16[mid])))
    d = (wsum / cnt.astype(F32) - p).astype(BF16)
    pooled = (_dot(d, wpool_ref[...]) * pscale_ref[...]).astype(BF16)

    x = _stream(refs, n_terms)
    D = x.shape[1]
    h = (_rms(x) * gmix_ref[...]).astype(BF16)
    br_a = _dot(attn_ref[...], wao_ref[...])
    br_b = _dot(sgu, wso_ref[...])
    br_c = _dot(pooled, wpo_ref[...])
    gl = _dot(h, wgl_ref[...])
    gate = lambda j: 0.5 * (1.0 + jnp.tanh(0.5 * gl[:, j * D:(j + 1) * D]))
    merged = (gate(0) * br_a + gate(1) * br_b + gate(2) * br_c).astype(BF16)
    x1 = x + _dot(merged, wout_ref[...])
    x1_ref[...] = x1
    h2 = (_rms(x1) * gffn_ref[...]).astype(BF16)
    h2_ref[...] = h2
    lg_ref[...] = _dot(h2, wr_ref[...])


def _merge(stream, gmix, w_gl, attn, u, vn, p, wsp, bsp, wpool, pscale, wao, wso, wpo, wout, gffn, wr,
           S, ts):
    T, D = stream[0].shape
    row = lambda i: (i, 0)
    const = lambda i: (0, 0)
    hb = ts // POOL_HALO
    nhb = T // POOL_HALO
    full = lambda a: pl.BlockSpec(a.shape, const)
    return pl.pallas_call(
        functools.partial(_merge_kernel, S=S, n_terms=len(stream)),
        grid=(T // ts,),
        in_specs=[pl.BlockSpec((ts, D), row)] * len(stream) + [
            full(gmix),
            full(w_gl),
            pl.BlockSpec((ts, Q_W), row),
            pl.BlockSpec((ts, SGU_W), row),
            pl.BlockSpec((ts, SGU_W), row),
            pl.BlockSpec((ts, POOL_W), row),
            pl.BlockSpec((POOL_HALO, POOL_W), lambda i: (jnp.maximum(i * hb - 1, 0), 0)),
            pl.BlockSpec((POOL_HALO, POOL_W), lambda i: (jnp.minimum((i + 1) * hb, nhb - 1), 0)),
            full(wsp), full(bsp), full(wpool), full(pscale), full(wao), full(wso), full(wpo),
            full(wout), full(gffn), full(wr),
        ],
        out_specs=[
            pl.BlockSpec((ts, D), row),
            pl.BlockSpec((ts, D), row),
            pl.BlockSpec((ts, LANES), row),
        ],
        out_shape=[
            jax.ShapeDtypeStruct((T, D), F32),
            jax.ShapeDtypeStruct((T, D), BF16),
            jax.ShapeDtypeStruct((T, LANES), F32),
        ],
        compiler_params=_cparams("parallel"),
        name="merge",
    )(*stream, gmix, w_gl, attn, u, vn, p, p, p, wsp, bsp, wpool, pscale, wao, wso, wpo, wout, gffn,
      wr)


def _excl_cumsum(m01, tri_ones):
    off = jnp.zeros((m01.shape[0], LANES), F32)
    pieces = []
    for j in range(m01.shape[1] // LANES):
        res = _dot(m01[:, j * LANES:(j + 1) * LANES].astype(BF16), tri_ones)
        pieces.append(res[:, :LANES] + off)
        off = off + res[:, LANES:]
    return jnp.concatenate(pieces, axis=1)


def _tile_lists(offs, nblk, list_len):
    E = offs.shape[0]
    lane = lax.broadcasted_iota(jnp.int32, offs.shape, 1)
    hi = pltpu.roll(offs, LANES - 1, 1)
    g_lo = jnp.floor(offs * (1.0 / SLOT_TILE))
    g_hi = jnp.floor((hi + (SLOT_TILE - 1)) * (1.0 / SLOT_TILE))
    cnt = jnp.where((hi > offs) & (lane < nblk), g_hi - g_lo, 0.0)
    below = (lax.broadcasted_iota(jnp.int32, (E, E), 1) < lax.broadcasted_iota(jnp.int32, (E, E), 0))
    base = _dot(jnp.where(below, 1.0, 0.0).astype(BF16), cnt.astype(BF16))
    end = base + cnt
    t = lax.broadcasted_iota(jnp.int32, (1, list_len), 1).astype(F32)
    row_e = lax.broadcasted_iota(jnp.int32, (E, list_len), 0).astype(F32)
    blocks = []
    for j in range(nblk):
        col = slice(j, j + 1)
        e_t = jnp.sum(jnp.where(end[:, col] <= t, 1.0, 0.0), axis=0, keepdims=True)
        shift = jnp.sum(jnp.where(row_e == e_t, g_lo[:, col] - base[:, col], 0.0), axis=0, keepdims=True)
        valid = e_t < E
        g_t = jnp.where(valid, t + shift, 0.0)
        total = jnp.broadcast_to(end[E - 1:E, col], (1, list_len))
        rows = [jnp.where(valid, e_t, 0.0), g_t, jnp.where(valid, g_t * SLOT_TILE, float(NO_SLOT)),
                total, jnp.zeros((4, list_len), F32)]
        blocks.append(jnp.concatenate(rows, axis=0).astype(jnp.int32)[None])
    return jnp.concatenate(blocks, axis=0)


def _route_kernel(lg_ref, tri_ref, rank_ref, aff_ref, lists_ref, *, cap):
    nb, S, _ = lg_ref.shape
    E = N_EXPERTS
    rows = []
    for b in range(nb):
        lt = lg_ref[b].T[:E]
        e = jnp.exp(lt - jnp.max(lt, axis=0, keepdims=True))
        rows.append(e / jnp.sum(e, axis=0, keepdims=True))
    aff = jnp.concatenate(rows, axis=0)
    bits = pltpu.bitcast(aff, jnp.int32)

    def refine(i, thr):
        cand = thr | jnp.left_shift(jnp.int32(1), 30 - i)
        cnt = jnp.sum(jnp.where(bits >= cand, 1.0, 0.0), axis=1, keepdims=True)
        return jnp.where(cnt >= cap, cand, thr)

    thr = lax.fori_loop(0, 31, refine, jnp.zeros((nb * E, 1), jnp.int32))
    gt = jnp.where(bits > thr, 1.0, 0.0)
    eq = jnp.where(bits == thr, 1.0, 0.0)
    need = cap - jnp.sum(gt, axis=1, keepdims=True)
    tri = tri_ref[...]
    sel = gt + eq * jnp.where(_excl_cumsum(eq, tri) < need, 1.0, 0.0)
    before = _excl_cumsum(sel, tri)
    rank = jnp.where(sel > 0.5, before, -1.0)
    nblk = S // TOKEN_BLOCK
    lane = lax.broadcasted_iota(jnp.int32, (nb * E, LANES), 1)
    offs = jnp.where(lane == nblk, float(cap), 0.0)
    for j in range(nblk):
        offs = jnp.where(lane == j, before[:, j * TOKEN_BLOCK:j * TOKEN_BLOCK + 1], offs)
    list_len = lists_ref.shape[-1]
    for b in range(nb):
        rank_ref[b] = rank[b * E:(b + 1) * E]
        aff_ref[b] = aff[b * E:(b + 1) * E]
        lists_ref[b] = _tile_lists(offs[b * E:(b + 1) * E], nblk, list_len)


def _list_len(cap):
    worst = N_EXPERTS * (cap // SLOT_TILE + 2)
    return -(-worst // LANES) * LANES


def _route(lg3, tri, cap, nb):
    B, S, _ = lg3.shape
    nblk = S // TOKEN_BLOCK
    blk = lambda b: (b, 0, 0)
    lists_shape = (nblk, 8, _list_len(cap))
    return pl.pallas_call(
        functools.partial(_route_kernel, cap=cap),
        grid=(B // nb,),
        in_specs=[pl.BlockSpec((nb, S, LANES), blk), pl.BlockSpec(tri.shape, lambda b: (0, 0))],
        out_specs=[
            pl.BlockSpec((nb, N_EXPERTS, S), blk),
            pl.BlockSpec((nb, N_EXPERTS, S), blk),
            pl.BlockSpec((nb,) + lists_shape, lambda b: (b, 0, 0, 0)),
        ],
        out_shape=[
            jax.ShapeDtypeStruct((B, N_EXPERTS, S), F32),
            jax.ShapeDtypeStruct((B, N_EXPERTS, S), F32),
            jax.ShapeDtypeStruct((B,) + lists_shape, jnp.int32),
        ],
        compiler_params=_cparams("parallel"),
        name="route",
    )(lg3, tri)


def _chunk_tiles(list_ref, k, c):
    out = []
    for i in range(CHUNK_TILES):
        t = c * CHUNK_TILES + i
        out.append((list_ref[k, 0, t], list_ref[k, 1, t], list_ref[k, 2, t]))
    return out


def _num_chunks(list_ref, k):
    return jnp.right_shift(list_ref[k, 3, 0] + CHUNK_TILES - 1, CHUNK_SHIFT)


def _for_each_chunk(list_ref, body):
    for k in range(list_ref.shape[0]):
        body(k, 0, True)
    for k in range(list_ref.shape[0]):
        def rest(c, carry, k=k):
            body(k, c, False)
            return carry
        lax.fori_loop(1, _num_chunks(list_ref, k), rest, 0)


def _tile_rows(g_t):
    return pl.ds(pl.multiple_of(g_t * SLOT_TILE, SLOT_TILE), SLOT_TILE)


def _onehot(entries, rank_ref, j, weight_ref=None):
    tb = rank_ref.shape[-1]
    sub = lax.broadcasted_iota(jnp.int32, (SLOT_TILE, tb), 0)
    tiles = []
    for e_t, _, slot0 in entries:
        hit = rank_ref[e_t, pl.ds(j, 1), :] == (slot0 + sub).astype(F32)
        val = 1.0 if weight_ref is None else weight_ref[e_t, pl.ds(j, 1), :]
        tiles.append(jnp.where(hit, val, 0.0).astype(BF16))
    return jnp.concatenate(tiles, axis=0)


def _gather_kernel(list_ref, h2_ref, rank_ref, xs_ref):
    tb = rank_ref.shape[-1]
    step = pl.program_id(1)

    @pl.when(step == 0)
    def _():
        xs_ref[...] = jnp.zeros_like(xs_ref)

    def chunk(k, c, first):
        entries = _chunk_tiles(list_ref, k, c)
        onehot = _onehot(entries, rank_ref, step * list_ref.shape[0] + k)
        rows = _dot(onehot, h2_ref[k * tb:(k + 1) * tb, :]).astype(BF16)
        for i, (e_t, g_t, _) in enumerate(entries):
            xs_ref[e_t, _tile_rows(g_t), :] += rows[i * SLOT_TILE:(i + 1) * SLOT_TILE, :]

    _for_each_chunk(list_ref, chunk)


def _list_spec(lists, nblocks):
    return pl.BlockSpec((None, nblocks) + lists.shape[2:], lambda b, j: (b, j, 0, 0),
                        memory_space=pltpu.SMEM)


def _gather(h2, rank4, lists, cap):
    B, S, D = h2.shape
    _, E, nblk, tb = rank4.shape
    rows = BLOCKS_PER_STEP * tb
    return pl.pallas_call(
        _gather_kernel,
        grid=(B, nblk // BLOCKS_PER_STEP),
        in_specs=[
            _list_spec(lists, BLOCKS_PER_STEP),
            pl.BlockSpec((None, rows, D), lambda b, j: (b, j, 0)),
            pl.BlockSpec((None, E, nblk, tb), lambda b, j: (b, 0, 0, 0)),
        ],
        out_specs=pl.BlockSpec((None, E, cap, D), lambda b, j: (b, 0, 0, 0)),
        out_shape=jax.ShapeDtypeStruct((B, E, cap, D), BF16),
        compiler_params=_cparams("arbitrary", "arbitrary"),
        name="gather",
    )(lists, h2, rank4)


def _ffn_kernel(xs_ref, wg_ref, wu_ref, wd_ref, y_ref, wg_b, wu_b, wd_b):
    @pl.when(pl.program_id(1) == 0)
    def _():
        wg_b[...] = wg_ref[...].astype(BF16)
        wu_b[...] = wu_ref[...].astype(BF16)
        wd_b[...] = wd_ref[...].astype(BF16)

    nb, cap, D = xs_ref.shape
    xs = xs_ref[...].reshape(nb * cap, D)
    a = _dot(xs, wg_b[...])
    b = _dot(xs, wu_b[...])
    hidden = (a * jax.nn.sigmoid(a) * b).astype(BF16)
    y_ref[...] = _dot(hidden, wd_b[...]).astype(BF16).reshape(nb, cap, D)


def _ffn(xs, wg, wu, wd, layer, nb):
    B, E, cap, D = xs.shape
    F = wg.shape[3]
    blk = pl.BlockSpec((nb, None, cap, D), lambda e, b: (b, e, 0, 0))
    return pl.pallas_call(
        _ffn_kernel,
        grid=(E, B // nb),
        in_specs=[
            blk,
            pl.BlockSpec((None, None, D, F), lambda e, b: (layer, e, 0, 0)),
            pl.BlockSpec((None, None, D, F), lambda e, b: (layer, e, 0, 0)),
            pl.BlockSpec((None, None, F, D), lambda e, b: (layer, e, 0, 0)),
        ],
        out_specs=blk,
        out_shape=jax.ShapeDtypeStruct((B, E, cap, D), BF16),
        scratch_shapes=[pltpu.VMEM((D, F), BF16), pltpu.VMEM((D, F), BF16), pltpu.VMEM((F, D), BF16)],
        compiler_params=_cparams("arbitrary", "arbitrary"),
        name="ffn",
    )(xs, wg, wu, wd)


def _scatter_kernel(list_ref, y_ref, rank_ref, aff_ref, *refs, closing):
    out_ref = refs[-1]
    tb = rank_ref.shape[-1]
    step = pl.program_id(1)

    def chunk(k, c, first):
        entries = _chunk_tiles(list_ref, k, c)
        weights = _onehot(entries, rank_ref, step * list_ref.shape[0] + k, aff_ref)
        ys = jnp.concatenate([y_ref[e_t, _tile_rows(g_t), :] for e_t, g_t, _ in entries], axis=0)
        add = lax.dot_general(weights, ys, (((0,), (0,)), ((), ())), preferred_element_type=F32)
        rows = slice(k * tb, (k + 1) * tb)
        if not first:
            add = add + out_ref[rows, :]
        elif closing:
            add = add + refs[0][rows, :]
        out_ref[rows, :] = add

    _for_each_chunk(list_ref, chunk)
    if closing:
        out_ref[...] = _rms(out_ref[...]) * refs[1][...]


def _scatter(y, rank4, aff4, lists, closing_inputs=None):
    B, E, cap, D = y.shape
    _, _, nblk, tb = rank4.shape
    steps = nblk // BLOCKS_PER_STEP
    rows = pl.BlockSpec((BLOCKS_PER_STEP * tb, D), lambda b, j: (b * steps + j, 0))
    per_seq = pl.BlockSpec((None, E, nblk, tb), lambda b, j: (b, 0, 0, 0))
    in_specs = [_list_spec(lists, BLOCKS_PER_STEP),
                pl.BlockSpec((None, E, cap, D), lambda b, j: (b, 0, 0, 0)), per_seq, per_seq]
    extra = ()
    if closing_inputs is not None:
        extra = closing_inputs
        in_specs += [rows, pl.BlockSpec(closing_inputs[1].shape, lambda b, j: (0, 0))]
    return pl.pallas_call(
        functools.partial(_scatter_kernel, closing=closing_inputs is not None),
        grid=(B, steps),
        in_specs=in_specs,
        out_specs=rows,
        out_shape=jax.ShapeDtypeStruct((B * nblk * tb, D), F32),
        compiler_params=_cparams("arbitrary", "arbitrary"),
        name="scatter",
    )(lists, y, rank4, aff4, *extra)


def _rope_tables(S, g, scale):
    rows = S // GRID_W
    row = jnp.broadcast_to(jnp.arange(rows, dtype=F32)[:, None], (rows, GRID_W)).reshape(-1)
    col = jnp.broadcast_to(jnp.arange(GRID_W, dtype=F32)[None, :], (rows, GRID_W)).reshape(-1)
    part = HEAD_DIM // 2
    freqs = ROPE_THETA ** (-jnp.arange(0, part, 2, dtype=F32) / part)
    ang = jnp.stack([row[:, None] * freqs, col[:, None] * freqs], axis=1)
    cos, sin = jnp.cos(ang), jnp.sin(ang)
    c64 = jnp.concatenate([cos[:, 0], cos[:, 0], cos[:, 1], cos[:, 1]], axis=-1)
    s64 = jnp.concatenate([-sin[:, 0], sin[:, 0], -sin[:, 1], sin[:, 1]], axis=-1)
    g_swapped = g.reshape(2, 2, HEAD_DIM // 4)[:, ::-1, :].reshape(HEAD_DIM)
    gc = jnp.tile(c64 * g[None, :], (1, 2)) * scale
    gs = jnp.tile(s64 * g_swapped[None, :], (1, 2)) * scale
    return gc, gs


def _block_diag(blocks):
    n = blocks.shape[0]
    m = blocks.shape[1]
    eye = jnp.eye(n, dtype=blocks.dtype)
    return (eye[:, None, :, None] * blocks[:, :, None, :]).reshape(n * m, n * blocks.shape[2])


def kernel(x, g_mix, w_in, g_q, g_k, g_sgu, w_spatial, b_spatial, w_pool, pool_scale, w_attn_o,
           w_sgu_o, w_pool_o, w_out, g_ffn, w_router, w_gate_e, w_up_e, w_down_e, g_final):
    B, S, D = x.shape
    depth = w_in.shape[0]
    cap = CAPACITY_FACTOR * S // N_EXPERTS
    stream = (x.reshape(B * S, D),)

    bd = _block_diag(jnp.ones((2, HEAD_DIM, HEAD_DIM), F32)).astype(BF16)
    k_i = lax.broadcasted_iota(jnp.int32, (LANES, LANES), 0)
    l_i = lax.broadcasted_iota(jnp.int32, (LANES, LANES), 1)
    tri = jnp.concatenate([(k_i < l_i).astype(BF16), jnp.ones((LANES, LANES), BF16)], axis=1)

    for l in range(depth):
        gcq, gsq = _rope_tables(S, g_q[l], HEAD_DIM ** -0.5 * math.log2(math.e))
        gck, gsk = _rope_tables(S, g_k[l], 1.0)
        w_gl = w_in[l, :, A_W:].astype(BF16)
        gmix = g_mix[l][None, :]
        proj = (gmix, w_in[l, :, :A_W].astype(BF16), gcq, gsq, gck, gsk, bd, g_sgu[l][None, :])
        q, k, v, u, vn, p = _inproj(stream, proj, S, INPROJ_ROWS)
        attn = _attention(q, k, v, B, S, ATTN_QUERY_SUBS * QUERY_SUB)

        wsp = w_spatial[l].transpose(1, 0, 2).reshape(SGU_CHUNK, SGU_GROUPS * SGU_CHUNK).astype(BF16)
        bsp = jnp.repeat(b_spatial[l].T, SGU_HEAD, axis=1)
        wpool = _block_diag(w_pool[l]).astype(BF16)
        wr = jnp.pad(w_router[l], ((0, 0), (0, LANES - N_EXPERTS))).astype(BF16)
        x1, h2, lg = _merge(
            stream, gmix, w_gl, attn, u, vn, p, wsp, bsp, wpool, pool_scale[l][None, :],
            w_attn_o[l].astype(BF16), w_sgu_o[l].astype(BF16), w_pool_o[l].astype(BF16),
            w_out[l].astype(BF16), g_ffn[l][None, :], wr, S, MERGE_ROWS)

        rank, aff, lists = _route(lg.reshape(B, S, LANES), tri, cap, math.gcd(B, SEQS_PER_STEP))
        per_block = (B, N_EXPERTS, S // TOKEN_BLOCK, TOKEN_BLOCK)
        rank4, aff4 = rank.reshape(per_block), aff.reshape(per_block)
        xs = _gather(h2.reshape(B, S, D), rank4, lists, cap)
        y = _ffn(xs, w_gate_e, w_up_e, w_down_e, l, math.gcd(B, SEQS_PER_STEP))
        if l < depth - 1:
            stream = (x1, _scatter(y, rank4, aff4, lists))

    return _scatter(y, rank4, aff4, lists, (x1, g_final[None, :])).reshape(B, S, D)
```

```python
import functools
import math

import jax
import jax.numpy as jnp
from jax import lax
from jax.experimental import pallas as pl
from jax.experimental.pallas import tpu as pltpu

F32 = jnp.float32
BF16 = jnp.bfloat16

GRID_W = 64
N_HEADS = 8
N_KV_HEADS = 2
HEAD_DIM = 64
ROPE_THETA = 10000.0
SGU_GROUPS = 4
SGU_HEAD = 64
SGU_W = SGU_GROUPS * SGU_HEAD
SGU_CHUNK = 128
POOL_WINDOWS = (2, 4, 8, 16)
POOL_HEAD = 64
POOL_W = len(POOL_WINDOWS) * POOL_HEAD
POOL_HALO = 8
KEY_CHUNK = 512
QUERY_SUB = 512
TOKEN_BLOCK = 256
BLOCKS_PER_STEP = 4
INPROJ_ROWS = 1024
MERGE_ROWS = 512
ATTN_QUERY_SUBS = 2
SEQS_PER_STEP = 4
TILE_SHIFT = 4
SLOT_TILE = 1 << TILE_SHIFT
CHUNK_SHIFT = 6
CHUNK_TILES = 1 << CHUNK_SHIFT
NO_SLOT = -1024
Q_W = N_HEADS * HEAD_DIM
KV_W = N_KV_HEADS * HEAD_DIM
N_BRANCHES = 3
N_EXPERTS = 16
CAPACITY_FACTOR = 2
EPS = 1e-6

LANES = 128
F32_SUBLANES = 8
A_W = Q_W + 2 * KV_W + 2 * SGU_W + POOL_W
VMEM_LIMIT = 56 * 1024 * 1024


def _cparams(*sem):
    return pltpu.CompilerParams(dimension_semantics=sem, vmem_limit_bytes=VMEM_LIMIT)


def _dot(a, b):
    return jnp.dot(a, b, preferred_element_type=F32)


def _dot_nt(a, b):
    return lax.dot_general(a, b, (((1,), (1,)), ((), ())), preferred_element_type=F32)


def _rms(x):
    return x * lax.rsqrt(jnp.mean(x * x, axis=-1, keepdims=True) + EPS)


def _head_norm_rope(xq, gc, gs, bd):
    sq = xq * xq
    hi = sq.astype(BF16)
    lo = (sq - hi.astype(F32)).astype(BF16)
    ss = _dot(hi, bd) + _dot(lo, bd)
    r = lax.rsqrt(ss * (1.0 / HEAD_DIM) + EPS)
    lane = lax.broadcasted_iota(jnp.int32, xq.shape, 1)
    first_half = (lane % 32) < 16
    partner = jnp.where(first_half, pltpu.roll(xq, LANES - 16, 1), pltpu.roll(xq, 16, 1))
    return r * (xq * gc + partner * gs)


def _project_rows(x, gmix_ref, w_ref, gcq_ref, gsq_ref, gck_ref, gsk_ref, bd_ref, gsgu_ref,
                  q_ref, k_ref, v_ref, u_ref, vn_ref, p_ref):
    h = (_rms(x) * gmix_ref[...]).astype(BF16)
    acc = _dot(h, w_ref[...])
    bd = bd_ref[...]
    for c in range(Q_W // LANES):
        sl = slice(c * LANES, (c + 1) * LANES)
        q_ref[:, sl] = _head_norm_rope(acc[:, sl], gcq_ref[...], gsq_ref[...], bd).astype(BF16)
    o = Q_W
    k_ref[...] = _head_norm_rope(acc[:, o:o + KV_W], gck_ref[...], gsk_ref[...], bd).astype(BF16)
    o += KV_W
    v_ref[...] = acc[:, o:o + KV_W].astype(BF16)
    o += KV_W
    u_ref[...] = acc[:, o:o + SGU_W]
    o += SGU_W
    vn_ref[...] = (_rms(acc[:, o:o + SGU_W]) * gsgu_ref[...]).astype(BF16)
    o += SGU_W
    p_ref[...] = acc[:, o:o + POOL_W]


def _inproj_kernel(x_ref, *refs):
    _project_rows(x_ref[...], *refs)


def _project_specs(T, D, tm, row, pos, const):
    in_specs = [
        pl.BlockSpec((1, D), const),
        pl.BlockSpec((D, A_W), const),
        pl.BlockSpec((tm, LANES), pos),
        pl.BlockSpec((tm, LANES), pos),
        pl.BlockSpec((tm, LANES), pos),
        pl.BlockSpec((tm, LANES), pos),
        pl.BlockSpec((LANES, LANES), const),
        pl.BlockSpec((1, SGU_W), const),
    ]
    widths = ((Q_W, BF16), (KV_W, BF16), (KV_W, BF16), (SGU_W, F32), (SGU_W, BF16), (POOL_W, F32))
    out_specs = [pl.BlockSpec((tm, w), row) for w, _ in widths]
    out_shape = [jax.ShapeDtypeStruct((T, w), dt) for w, dt in widths]
    return in_specs, out_specs, out_shape


def _inproj(x2, params, S, tm):
    T, D = x2.shape
    nseq = S // tm
    row = lambda i: (i, 0)
    in_specs, out_specs, out_shape = _project_specs(
        T, D, tm, row, lambda i: (i % nseq, 0), lambda i: (0, 0))
    return pl.pallas_call(
        _inproj_kernel,
        grid=(T // tm,),
        in_specs=[pl.BlockSpec((tm, D), row)] + in_specs,
        out_specs=out_specs,
        out_shape=out_shape,
        compiler_params=_cparams("parallel"),
        name="inproj",
    )(x2, *params)


def _attn_kernel(q_ref, k_ref, v_ref, o_ref, kz_ref, vzt_ref, s_ref, qt_ref):
    @pl.when(pl.program_id(1) == 0)
    def _():
        lane = lax.broadcasted_iota(jnp.int32, k_ref.shape, 1)
        low = lane < HEAD_DIM
        k = k_ref[...].astype(F32)
        kr = pltpu.roll(k, HEAD_DIM, 1)
        kz_ref[0] = jnp.where(low, k, 0.0).astype(BF16)
        kz_ref[1] = jnp.where(low, 0.0, kr).astype(BF16)
        kz_ref[2] = jnp.where(low, kr, 0.0).astype(BF16)
        kz_ref[3] = jnp.where(low, 0.0, k).astype(BF16)
        vt = v_ref[...].astype(F32).T
        vtr = pltpu.roll(vt, HEAD_DIM, 0)
        row = lax.broadcasted_iota(jnp.int32, vt.shape, 0)
        top = row < HEAD_DIM
        ones_lo = jnp.where(row == 0, 1.0, 0.0)
        ones_hi = jnp.where(row == HEAD_DIM, 1.0, 0.0)
        vzt_ref[0] = jnp.where(top, vt, ones_hi).astype(BF16)
        vzt_ref[1] = jnp.where(top, ones_lo, vtr).astype(BF16)
        vzt_ref[2] = jnp.where(top, vtr, ones_hi).astype(BF16)
        vzt_ref[3] = jnp.where(top, ones_lo, vt).astype(BF16)

    tq = s_ref.shape[2]
    S = k_ref.shape[0]
    nchunk = S // KEY_CHUNK
    sub = F32_SUBLANES
    slot0 = jnp.minimum(pl.program_id(1), 0)

    def unit_parts(unit):
        qt, head = divmod(unit, N_HEADS)
        pair, parity = divmod(head, 2)
        kvi = 2 * ((2 * pair) // (N_HEADS // N_KV_HEADS)) + parity
        return slice(qt * tq, (qt + 1) * tq), slice(pair * LANES, (pair + 1) * LANES), parity, kvi

    npair = N_HEADS // 2
    for qt in range(q_ref.shape[0] // tq):
        for pair in range(npair):
            q_pair = q_ref[qt * tq:(qt + 1) * tq, pair * LANES:(pair + 1) * LANES]
            qt_ref[qt * npair + pair] = q_pair.astype(F32).T.astype(BF16)

    def scores(unit, c, mrun):
        kvi = unit_parts(unit)[3]
        rows = slice(c * KEY_CHUNK, (c + 1) * KEY_CHUNK)
        s = _dot(kz_ref[kvi, rows, :], qt_ref[unit // 2])
        s_ref[slot0 + unit % 2, rows, :] = s
        m = jnp.max(s.reshape(KEY_CHUNK // sub, sub, tq), axis=0)
        return m if mrun is None else jnp.maximum(mrun, m)

    def weighted(unit, c, m8, acc):
        kvi = unit_parts(unit)[3]
        rows = slice(c * KEY_CHUNK, (c + 1) * KEY_CHUNK)
        s3 = s_ref[slot0 + unit % 2, rows, :].reshape(KEY_CHUNK // sub, sub, tq)
        e = jnp.exp2(s3 - m8[None]).reshape(KEY_CHUNK, tq).astype(BF16)
        d = _dot(vzt_ref[kvi, :, rows], e)
        return d if acc is None else acc + d

    nunit = (q_ref.shape[0] // tq) * N_HEADS
    top = lax.broadcasted_iota(jnp.int32, (LANES, tq), 0) < HEAD_DIM
    mrun = None
    for c in range(nchunk):
        mrun = scores(0, c, mrun)
    even = None
    for unit in range(nunit):
        m8 = jnp.broadcast_to(jnp.max(mrun, axis=0, keepdims=True), (sub, tq))
        mnext, acc = None, None
        for c in range(nchunk):
            if unit + 1 < nunit:
                mnext = scores(unit + 1, c, mnext)
            acc = weighted(unit, c, m8, acc)
        mrun = mnext
        qrows, lanes, parity, _ = unit_parts(unit)
        if parity == 0:
            even = acc * (1.0 / acc[HEAD_DIM:HEAD_DIM + 1])
        else:
            odd = acc * (1.0 / acc[0:1])
            o_ref[qrows, lanes] = jnp.where(top, even, odd).T.astype(BF16)


def _attention(q, k, v, B, S, tq):
    T = q.shape[0]
    nq = S // tq
    return pl.pallas_call(
        _attn_kernel,
        grid=(B, nq),
        in_specs=[
            pl.BlockSpec((tq, Q_W), lambda b, i: (b * nq + i, 0)),
            pl.BlockSpec((S, KV_W), lambda b, i: (b, 0)),
            pl.BlockSpec((S, KV_W), lambda b, i: (b, 0)),
        ],
        out_specs=pl.BlockSpec((tq, Q_W), lambda b, i: (b * nq + i, 0)),
        out_shape=jax.ShapeDtypeStruct((T, Q_W), BF16),
        scratch_shapes=[pltpu.VMEM((4, S, KV_W), BF16), pltpu.VMEM((4, KV_W, S), BF16),
                        pltpu.VMEM((2, S, QUERY_SUB), F32),
                        pltpu.VMEM((tq // QUERY_SUB * (N_HEADS // 2), LANES, QUERY_SUB), BF16)],
        compiler_params=_cparams("parallel", "arbitrary"),
        name="attention",
    )(q, k, v)


def _merge_kernel(x_ref, gmix_ref, wgl_ref, attn_ref, u_ref, vn_ref, p_ref, pprev_ref, pnext_ref,
                  wsp_ref, bsp_ref, wpool_ref, pscale_ref, wao_ref, wso_ref, wpo_ref, wout_ref,
                  gffn_ref, wr_ref, x1_ref, h2_ref, lg_ref, *, S):
    ts = x_ref.shape[0]

    vn = vn_ref[...]
    lane_b = lax.broadcasted_iota(jnp.int32, (SGU_CHUNK, SGU_W), 1) // SGU_HEAD
    zs = []
    for c in range(ts // SGU_CHUNK):
        vc = vn[c * SGU_CHUNK:(c + 1) * SGU_CHUNK]
        stacked = jnp.concatenate(
            [jnp.where(lane_b == g, vc, jnp.zeros_like(vc)) for g in range(SGU_GROUPS)], axis=0)
        zs.append(_dot(wsp_ref[...], stacked) + bsp_ref[...])
    sgu = (u_ref[...] * jnp.concatenate(zs, axis=0)).astype(BF16)

    pos0 = (pl.program_id(0) % (S // ts)) * ts
    p = p_ref[...]
    prev = jnp.where(pos0 == 0, 0.0, pprev_ref[...])
    nxt = jnp.where(pos0 + ts == S, 0.0, pnext_ref[...])
    ext = jnp.concatenate([prev, p, nxt], axis=0)
    n = ts + 2 * POOL_HALO
    s2 = ext + pltpu.roll(ext, 1, 0)
    s4 = pltpu.roll(s2, 1, 0) + pltpu.roll(s2, n - 1, 0)
    s8 = pltpu.roll(s4, 2, 0) + pltpu.roll(s4, n - 2, 0)
    s16 = pltpu.roll(s8, 4, 0) + pltpu.roll(s8, n - 4, 0)
    mid = slice(POOL_HALO, POOL_HALO + ts)
    grp = lax.broadcasted_iota(jnp.int32, (ts, POOL_W), 1) // POOL_HEAD
    t = pos0 + lax.broadcasted_iota(jnp.int32, (ts, POOL_W), 0)
    half = jnp.left_shift(1, grp)
    cnt = jnp.minimum(t + half - 1, S - 1) - jnp.maximum(t - half, 0) + 1
    wsum = jnp.where(grp == 0, s2[mid], jnp.where(grp == 1, s4[mid],
                                                  jnp.where(grp == 2, s8[mid], s16[mid])))
    d = (wsum / cnt.astype(F32) - p).astype(BF16)
    pooled = (_dot(d, wpool_ref[...]) * pscale_ref[...]).astype(BF16)

    x = x_ref[...]
    D = x.shape[1]
    h = (_rms(x) * gmix_ref[...]).astype(BF16)
    br_a = _dot(attn_ref[...], wao_ref[...])
    br_b = _dot(sgu, wso_ref[...])
    br_c = _dot(pooled, wpo_ref[...])
    gl = _dot(h, wgl_ref[...])
    gate = lambda j: 0.5 * (1.0 + jnp.tanh(0.5 * gl[:, j * D:(j + 1) * D]))
    merged = (gate(0) * br_a + gate(1) * br_b + gate(2) * br_c).astype(BF16)
    x1 = x + _dot(merged, wout_ref[...])
    x1_ref[...] = x1
    h2 = (_rms(x1) * gffn_ref[...]).astype(BF16)
    h2_ref[...] = h2
    lg_ref[...] = _dot(h2, wr_ref[...])


def _merge(x2, gmix, w_gl, attn, u, vn, p, wsp, bsp, wpool, pscale, wao, wso, wpo, wout, gffn, wr,
           S, ts):
    T, D = x2.shape
    row = lambda i: (i, 0)
    const = lambda i: (0, 0)
    hb = ts // POOL_HALO
    nhb = T // POOL_HALO
    full = lambda a: pl.BlockSpec(a.shape, const)
    return pl.pallas_call(
        functools.partial(_merge_kernel, S=S),
        grid=(T // ts,),
        in_specs=[
            pl.BlockSpec((ts, D), row),
            full(gmix),
            full(w_gl),
            pl.BlockSpec((ts, Q_W), row),
            pl.BlockSpec((ts, SGU_W), row),
            pl.BlockSpec((ts, SGU_W), row),
            pl.BlockSpec((ts, POOL_W), row),
            pl.BlockSpec((POOL_HALO, POOL_W), lambda i: (jnp.maximum(i * hb - 1, 0), 0)),
            pl.BlockSpec((POOL_HALO, POOL_W), lambda i: (jnp.minimum((i + 1) * hb, nhb - 1), 0)),
            full(wsp), full(bsp), full(wpool), full(pscale), full(wao), full(wso), full(wpo),
            full(wout), full(gffn), full(wr),
        ],
        out_specs=[
            pl.BlockSpec((ts, D), row),
            pl.BlockSpec((ts, D), row),
            pl.BlockSpec((ts, LANES), row),
        ],
        out_shape=[
            jax.ShapeDtypeStruct((T, D), F32),
            jax.ShapeDtypeStruct((T, D), BF16),
            jax.ShapeDtypeStruct((T, LANES), F32),
        ],
        compiler_params=_cparams("parallel"),
        name="merge",
    )(x2, gmix, w_gl, attn, u, vn, p, p, p, wsp, bsp, wpool, pscale, wao, wso, wpo, wout, gffn, wr)


def _excl_cumsum(m01, tri_ones):
    off = jnp.zeros((m01.shape[0], LANES), F32)
    pieces = []
    for j in range(m01.shape[1] // LANES):
        res = _dot(m01[:, j * LANES:(j + 1) * LANES].astype(BF16), tri_ones)
        pieces.append(res[:, :LANES] + off)
        off = off + res[:, LANES:]
    return jnp.concatenate(pieces, axis=1)


def _tile_lists(offs, nblk, list_len):
    E = offs.shape[0]
    lane = lax.broadcasted_iota(jnp.int32, offs.shape, 1)
    hi = pltpu.roll(offs, LANES - 1, 1)
    g_lo = jnp.floor(offs * (1.0 / SLOT_TILE))
    g_hi = jnp.floor((hi + (SLOT_TILE - 1)) * (1.0 / SLOT_TILE))
    cnt = jnp.where((hi > offs) & (lane < nblk), g_hi - g_lo, 0.0)
    below = (lax.broadcasted_iota(jnp.int32, (E, E), 1) < lax.broadcasted_iota(jnp.int32, (E, E), 0))
    base = _dot(jnp.where(below, 1.0, 0.0).astype(BF16), cnt.astype(BF16))
    end = base + cnt
    t = lax.broadcasted_iota(jnp.int32, (1, list_len), 1).astype(F32)
    row_e = lax.broadcasted_iota(jnp.int32, (E, list_len), 0).astype(F32)
    blocks = []
    for j in range(nblk):
        col = slice(j, j + 1)
        e_t = jnp.sum(jnp.where(end[:, col] <= t, 1.0, 0.0), axis=0, keepdims=True)
        shift = jnp.sum(jnp.where(row_e == e_t, g_lo[:, col] - base[:, col], 0.0), axis=0, keepdims=True)
        valid = e_t < E
        g_t = jnp.where(valid, t + shift, 0.0)
        total = jnp.broadcast_to(end[E - 1:E, col], (1, list_len))
        rows = [jnp.where(valid, e_t, 0.0), g_t, jnp.where(valid, g_t * SLOT_TILE, float(NO_SLOT)),
                total, jnp.zeros((4, list_len), F32)]
        blocks.append(jnp.concatenate(rows, axis=0).astype(jnp.int32)[None])
    return jnp.concatenate(blocks, axis=0)


def _route_kernel(lg_ref, tri_ref, rank_ref, aff_ref, lists_ref, *, cap):
    nb, S, _ = lg_ref.shape
    E = N_EXPERTS
    rows = []
    for b in range(nb):
        lt = lg_ref[b].T[:E]
        e = jnp.exp(lt - jnp.max(lt, axis=0, keepdims=True))
        rows.append(e / jnp.sum(e, axis=0, keepdims=True))
    aff = jnp.concatenate(rows, axis=0)

    def refine(i, thr):
        cand = thr | jnp.left_shift(jnp.int32(1), 30 - i)
        cnt = jnp.sum(jnp.where(aff >= pltpu.bitcast(cand, F32), 1.0, 0.0), axis=1, keepdims=True)
        return jnp.where(cnt >= cap, cand, thr)

    thr = pltpu.bitcast(lax.fori_loop(0, 31, refine, jnp.zeros((nb * E, 1), jnp.int32)), F32)
    gt = jnp.where(aff > thr, 1.0, 0.0)
    eq = jnp.where(aff == thr, 1.0, 0.0)
    need = cap - jnp.sum(gt, axis=1, keepdims=True)
    tri = tri_ref[...]
    sel = gt + eq * jnp.where(_excl_cumsum(eq, tri) < need, 1.0, 0.0)
    before = _excl_cumsum(sel, tri)
    rank = jnp.where(sel > 0.5, before, -1.0)
    nblk = S // TOKEN_BLOCK
    lane = lax.broadcasted_iota(jnp.int32, (nb * E, LANES), 1)
    offs = jnp.where(lane == nblk, float(cap), 0.0)
    for j in range(nblk):
        offs = jnp.where(lane == j, before[:, j * TOKEN_BLOCK:j * TOKEN_BLOCK + 1], offs)
    list_len = lists_ref.shape[-1]
    for b in range(nb):
        rank_ref[b] = rank[b * E:(b + 1) * E]
        aff_ref[b] = aff[b * E:(b + 1) * E]
        lists_ref[b] = _tile_lists(offs[b * E:(b + 1) * E], nblk, list_len)


def _list_len(cap):
    worst = N_EXPERTS * (cap // SLOT_TILE + 2)
    return -(-worst // LANES) * LANES


def _route(lg3, tri, cap, nb):
    B, S, _ = lg3.shape
    nblk = S // TOKEN_BLOCK
    blk = lambda b: (b, 0, 0)
    lists_shape = (nblk, 8, _list_len(cap))
    return pl.pallas_call(
        functools.partial(_route_kernel, cap=cap),
        grid=(B // nb,),
        in_specs=[pl.BlockSpec((nb, S, LANES), blk), pl.BlockSpec(tri.shape, lambda b: (0, 0))],
        out_specs=[
            pl.BlockSpec((nb, N_EXPERTS, S), blk),
            pl.BlockSpec((nb, N_EXPERTS, S), blk),
            pl.BlockSpec((nb,) + lists_shape, lambda b: (b, 0, 0, 0)),
        ],
        out_shape=[
            jax.ShapeDtypeStruct((B, N_EXPERTS, S), F32),
            jax.ShapeDtypeStruct((B, N_EXPERTS, S), F32),
            jax.ShapeDtypeStruct((B,) + lists_shape, jnp.int32),
        ],
        compiler_params=_cparams("parallel"),
        name="route",
    )(lg3, tri)


def _chunk_tiles(list_ref, k, c):
    out = []
    for i in range(CHUNK_TILES):
        t = c * CHUNK_TILES + i
        out.append((list_ref[k, 0, t], list_ref[k, 1, t], list_ref[k, 2, t]))
    return out


def _num_chunks(list_ref, k):
    return jnp.right_shift(list_ref[k, 3, 0] + CHUNK_TILES - 1, CHUNK_SHIFT)


def _for_each_chunk(list_ref, body):
    for k in range(list_ref.shape[0]):
        body(k, 0, True)
    for k in range(list_ref.shape[0]):
        def rest(c, carry, k=k):
            body(k, c, False)
            return carry
        lax.fori_loop(1, _num_chunks(list_ref, k), rest, 0)


def _tile_rows(g_t):
    return pl.ds(pl.multiple_of(g_t * SLOT_TILE, SLOT_TILE), SLOT_TILE)


def _onehot(entries, rank_ref, j, weight_ref=None):
    tb = rank_ref.shape[-1]
    sub = lax.broadcasted_iota(jnp.int32, (SLOT_TILE, tb), 0)
    tiles = []
    for e_t, _, slot0 in entries:
        hit = rank_ref[e_t, pl.ds(j, 1), :] == (slot0 + sub).astype(F32)
        val = 1.0 if weight_ref is None else weight_ref[e_t, pl.ds(j, 1), :]
        tiles.append(jnp.where(hit, val, 0.0).astype(BF16))
    return jnp.concatenate(tiles, axis=0)


def _gather_kernel(list_ref, h2_ref, rank_ref, xs_ref):
    tb = rank_ref.shape[-1]
    step = pl.program_id(1)

    @pl.when(step == 0)
    def _():
        xs_ref[...] = jnp.zeros_like(xs_ref)

    def chunk(k, c, first):
        entries = _chunk_tiles(list_ref, k, c)
        onehot = _onehot(entries, rank_ref, step * list_ref.shape[0] + k)
        rows = _dot(onehot, h2_ref[k * tb:(k + 1) * tb, :]).astype(BF16)
        for i, (e_t, g_t, _) in enumerate(entries):
            xs_ref[e_t, _tile_rows(g_t), :] += rows[i * SLOT_TILE:(i + 1) * SLOT_TILE, :]

    _for_each_chunk(list_ref, chunk)


def _list_spec(lists, nblocks):
    return pl.BlockSpec((None, nblocks) + lists.shape[2:], lambda b, j: (b, j, 0, 0),
                        memory_space=pltpu.SMEM)


def _gather(h2, rank4, lists, cap):
    B, S, D = h2.shape
    _, E, nblk, tb = rank4.shape
    rows = BLOCKS_PER_STEP * tb
    return pl.pallas_call(
        _gather_kernel,
        grid=(B, nblk // BLOCKS_PER_STEP),
        in_specs=[
            _list_spec(lists, BLOCKS_PER_STEP),
            pl.BlockSpec((None, rows, D), lambda b, j: (b, j, 0)),
            pl.BlockSpec((None, E, nblk, tb), lambda b, j: (b, 0, 0, 0)),
        ],
        out_specs=pl.BlockSpec((None, E, cap, D), lambda b, j: (b, 0, 0, 0)),
        out_shape=jax.ShapeDtypeStruct((B, E, cap, D), BF16),
        compiler_params=_cparams("arbitrary", "arbitrary"),
        name="gather",
    )(lists, h2, rank4)


def _ffn_kernel(xs_ref, wg_ref, wu_ref, wd_ref, y_ref, wg_b, wu_b, wd_b):
    @pl.when(pl.program_id(1) == 0)
    def _():
        wg_b[...] = wg_ref[...].astype(BF16)
        wu_b[...] = wu_ref[...].astype(BF16)
        wd_b[...] = wd_ref[...].astype(BF16)

    nb, cap, D = xs_ref.shape
    xs = xs_ref[...].reshape(nb * cap, D)
    a = _dot(xs, wg_b[...])
    b = _dot(xs, wu_b[...])
    hidden = (a * jax.nn.sigmoid(a) * b).astype(BF16)
    y_ref[...] = _dot(hidden, wd_b[...]).astype(BF16).reshape(nb, cap, D)


def _ffn(xs, wg, wu, wd, layer, nb):
    B, E, cap, D = xs.shape
    F = wg.shape[3]
    blk = pl.BlockSpec((nb, None, cap, D), lambda e, b: (b, e, 0, 0))
    return pl.pallas_call(
        _ffn_kernel,
        grid=(E, B // nb),
        in_specs=[
            blk,
            pl.BlockSpec((None, None, D, F), lambda e, b: (layer, e, 0, 0)),
            pl.BlockSpec((None, None, D, F), lambda e, b: (layer, e, 0, 0)),
            pl.BlockSpec((None, None, F, D), lambda e, b: (layer, e, 0, 0)),
        ],
        out_specs=blk,
        out_shape=jax.ShapeDtypeStruct((B, E, cap, D), BF16),
        scratch_shapes=[pltpu.VMEM((D, F), BF16), pltpu.VMEM((D, F), BF16), pltpu.VMEM((F, D), BF16)],
        compiler_params=_cparams("arbitrary", "arbitrary"),
        name="ffn",
    )(xs, wg, wu, wd)


def _scatter_rows(list_ref, y_ref, rank_ref, aff_ref, x1_ref, out_ref):
    tb = rank_ref.shape[-1]
    step = pl.program_id(1)

    def chunk(k, c, first):
        entries = _chunk_tiles(list_ref, k, c)
        weights = _onehot(entries, rank_ref, step * list_ref.shape[0] + k, aff_ref)
        ys = jnp.concatenate([y_ref[e_t, _tile_rows(g_t), :] for e_t, g_t, _ in entries], axis=0)
        add = lax.dot_general(weights, ys, (((0,), (0,)), ((), ())), preferred_element_type=F32)
        rows = slice(k * tb, (k + 1) * tb)
        out_ref[rows, :] = add + (x1_ref[rows, :] if first else out_ref[rows, :])

    _for_each_chunk(list_ref, chunk)


def _scatter_specs(y, rank4, lists, nblocks):
    B, E, cap, D = y.shape
    _, _, nblk, tb = rank4.shape
    steps = nblk // nblocks
    rows = pl.BlockSpec((nblocks * tb, D), lambda b, j: (b * steps + j, 0))
    per_seq = pl.BlockSpec((None, E, nblk, tb), lambda b, j: (b, 0, 0, 0))
    in_specs = [_list_spec(lists, nblocks), pl.BlockSpec((None, E, cap, D), lambda b, j: (b, 0, 0, 0)),
                per_seq, per_seq, rows]
    return (B, steps), in_specs, rows


def _scatter_kernel(list_ref, y_ref, rank_ref, aff_ref, x1_ref, gout_ref, out_ref, *, norm_out):
    _scatter_rows(list_ref, y_ref, rank_ref, aff_ref, x1_ref, out_ref)
    if norm_out:
        out_ref[...] = _rms(out_ref[...]) * gout_ref[...]


def _scatter(y, rank4, aff4, x1, lists, gout, norm_out):
    grid, in_specs, rows = _scatter_specs(y, rank4, lists, BLOCKS_PER_STEP)
    return pl.pallas_call(
        functools.partial(_scatter_kernel, norm_out=norm_out),
        grid=grid,
        in_specs=in_specs + [pl.BlockSpec(gout.shape, lambda b, j: (0, 0))],
        out_specs=rows,
        out_shape=jax.ShapeDtypeStruct(x1.shape, F32),
        compiler_params=_cparams("arbitrary", "arbitrary"),
        name="scatter",
    )(lists, y, rank4, aff4, x1, gout)


def _rope_tables(S, g, scale):
    rows = S // GRID_W
    row = jnp.broadcast_to(jnp.arange(rows, dtype=F32)[:, None], (rows, GRID_W)).reshape(-1)
    col = jnp.broadcast_to(jnp.arange(GRID_W, dtype=F32)[None, :], (rows, GRID_W)).reshape(-1)
    part = HEAD_DIM // 2
    freqs = ROPE_THETA ** (-jnp.arange(0, part, 2, dtype=F32) / part)
    ang = jnp.stack([row[:, None] * freqs, col[:, None] * freqs], axis=1)
    cos, sin = jnp.cos(ang), jnp.sin(ang)
    c64 = jnp.concatenate([cos[:, 0], cos[:, 0], cos[:, 1], cos[:, 1]], axis=-1)
    s64 = jnp.concatenate([-sin[:, 0], sin[:, 0], -sin[:, 1], sin[:, 1]], axis=-1)
    g_swapped = g.reshape(2, 2, HEAD_DIM // 4)[:, ::-1, :].reshape(HEAD_DIM)
    gc = jnp.tile(c64 * g[None, :], (1, 2)) * scale
    gs = jnp.tile(s64 * g_swapped[None, :], (1, 2)) * scale
    return gc, gs


def _block_diag(blocks):
    n = blocks.shape[0]
    m = blocks.shape[1]
    eye = jnp.eye(n, dtype=blocks.dtype)
    return (eye[:, None, :, None] * blocks[:, :, None, :]).reshape(n * m, n * blocks.shape[2])


def kernel(x, g_mix, w_in, g_q, g_k, g_sgu, w_spatial, b_spatial, w_pool, pool_scale, w_attn_o,
           w_sgu_o, w_pool_o, w_out, g_ffn, w_router, w_gate_e, w_up_e, w_down_e, g_final):
    B, S, D = x.shape
    depth = w_in.shape[0]
    cap = CAPACITY_FACTOR * S // N_EXPERTS
    x2 = x.reshape(B * S, D)

    bd = _block_diag(jnp.ones((2, HEAD_DIM, HEAD_DIM), F32)).astype(BF16)
    k_i = lax.broadcasted_iota(jnp.int32, (LANES, LANES), 0)
    l_i = lax.broadcasted_iota(jnp.int32, (LANES, LANES), 1)
    tri = jnp.concatenate([(k_i < l_i).astype(BF16), jnp.ones((LANES, LANES), BF16)], axis=1)

    for l in range(depth):
        gcq, gsq = _rope_tables(S, g_q[l], HEAD_DIM ** -0.5 * math.log2(math.e))
        gck, gsk = _rope_tables(S, g_k[l], 1.0)
        w_gl = w_in[l, :, A_W:].astype(BF16)
        gmix = g_mix[l][None, :]
        proj = (gmix, w_in[l, :, :A_W].astype(BF16), gcq, gsq, gck, gsk, bd, g_sgu[l][None, :])
        q, k, v, u, vn, p = _inproj(x2, proj, S, INPROJ_ROWS)
        attn = _attention(q, k, v, B, S, ATTN_QUERY_SUBS * QUERY_SUB)

        wsp = w_spatial[l].transpose(1, 0, 2).reshape(SGU_CHUNK, SGU_GROUPS * SGU_CHUNK).astype(BF16)
        bsp = jnp.repeat(b_spatial[l].T, SGU_HEAD, axis=1)
        wpool = _block_diag(w_pool[l]).astype(BF16)
        wr = jnp.pad(w_router[l], ((0, 0), (0, LANES - N_EXPERTS))).astype(BF16)
        x1, h2, lg = _merge(
            x2, gmix, w_gl, attn, u, vn, p, wsp, bsp, wpool, pool_scale[l][None, :],
            w_attn_o[l].astype(BF16), w_sgu_o[l].astype(BF16), w_pool_o[l].astype(BF16),
            w_out[l].astype(BF16), g_ffn[l][None, :], wr, S, MERGE_ROWS)

        rank, aff, lists = _route(lg.reshape(B, S, LANES), tri, cap, math.gcd(B, SEQS_PER_STEP))
        per_block = (B, N_EXPERTS, S // TOKEN_BLOCK, TOKEN_BLOCK)
        rank4, aff4 = rank.reshape(per_block), aff.reshape(per_block)
        xs = _gather(h2.reshape(B, S, D), rank4, lists, cap)
        y = _ffn(xs, w_gate_e, w_up_e, w_down_e, l, math.gcd(B, SEQS_PER_STEP))
        x2 = _scatter(y, rank4, aff4, x1, lists, g_final[None, :], norm_out=(l == depth - 1))

    return x2.reshape(B, S, D)
```

```python
import functools
import math

import jax
import jax.numpy as jnp
from jax import lax
from jax.experimental import pallas as pl
from jax.experimental.pallas import tpu as pltpu

F32 = jnp.float32
BF16 = jnp.bfloat16

GRID_W = 64
N_HEADS = 8
N_KV_HEADS = 2
HEAD_DIM = 64
ROPE_THETA = 10000.0
SGU_GROUPS = 4
SGU_HEAD = 64
SGU_W = SGU_GROUPS * SGU_HEAD
SGU_CHUNK = 128
POOL_WINDOWS = (2, 4, 8, 16)
POOL_HEAD = 64
POOL_W = len(POOL_WINDOWS) * POOL_HEAD
POOL_HALO = 8
KEY_CHUNK = 512
QUERY_SUB = 512
TOKEN_BLOCK = 256
BLOCKS_PER_STEP = 4
INPROJ_ROWS = 1024
MERGE_ROWS = 1024
ATTN_QUERY_SUBS = 2
SEQS_PER_STEP = 4
TILE_SHIFT = 4
SLOT_TILE = 1 << TILE_SHIFT
CHUNK_SHIFT = 6
CHUNK_TILES = 1 << CHUNK_SHIFT
NO_SLOT = -1024
Q_W = N_HEADS * HEAD_DIM
KV_W = N_KV_HEADS * HEAD_DIM
N_BRANCHES = 3
N_EXPERTS = 16
CAPACITY_FACTOR = 2
EPS = 1e-6

LANES = 128
F32_SUBLANES = 8
A_W = Q_W + 2 * KV_W + 2 * SGU_W + POOL_W
VMEM_LIMIT = 56 * 1024 * 1024


def _cparams(*sem):
    return pltpu.CompilerParams(dimension_semantics=sem, vmem_limit_bytes=VMEM_LIMIT)


def _dot(a, b):
    return jnp.dot(a, b, preferred_element_type=F32)


def _dot_nt(a, b):
    return lax.dot_general(a, b, (((1,), (1,)), ((), ())), preferred_element_type=F32)


def _rms(x):
    return x * lax.rsqrt(jnp.mean(x * x, axis=-1, keepdims=True) + EPS)


def _head_norm_rope(xq, gc, gs, bd):
    sq = xq * xq
    hi = sq.astype(BF16)
    lo = (sq - hi.astype(F32)).astype(BF16)
    ss = _dot(hi, bd) + _dot(lo, bd)
    r = lax.rsqrt(ss * (1.0 / HEAD_DIM) + EPS)
    lane = lax.broadcasted_iota(jnp.int32, xq.shape, 1)
    first_half = (lane % 32) < 16
    partner = jnp.where(first_half, pltpu.roll(xq, LANES - 16, 1), pltpu.roll(xq, 16, 1))
    return r * (xq * gc + partner * gs)


def _project_rows(x, gmix_ref, w_ref, gcq_ref, gsq_ref, gck_ref, gsk_ref, bd_ref, gsgu_ref,
                  q_ref, k_ref, v_ref, u_ref, vn_ref, p_ref):
    h = (_rms(x) * gmix_ref[...]).astype(BF16)
    acc = _dot(h, w_ref[...])
    bd = bd_ref[...]
    for c in range(Q_W // LANES):
        sl = slice(c * LANES, (c + 1) * LANES)
        q_ref[:, sl] = _head_norm_rope(acc[:, sl], gcq_ref[...], gsq_ref[...], bd).astype(BF16)
    o = Q_W
    k_ref[...] = _head_norm_rope(acc[:, o:o + KV_W], gck_ref[...], gsk_ref[...], bd).astype(BF16)
    o += KV_W
    v_ref[...] = acc[:, o:o + KV_W].astype(BF16)
    o += KV_W
    u_ref[...] = acc[:, o:o + SGU_W]
    o += SGU_W
    vn_ref[...] = (_rms(acc[:, o:o + SGU_W]) * gsgu_ref[...]).astype(BF16)
    o += SGU_W
    p_ref[...] = acc[:, o:o + POOL_W]


def _inproj_kernel(x_ref, *refs):
    _project_rows(x_ref[...], *refs)


def _project_specs(T, D, tm, row, pos, const):
    in_specs = [
        pl.BlockSpec((1, D), const),
        pl.BlockSpec((D, A_W), const),
        pl.BlockSpec((tm, LANES), pos),
        pl.BlockSpec((tm, LANES), pos),
        pl.BlockSpec((tm, LANES), pos),
        pl.BlockSpec((tm, LANES), pos),
        pl.BlockSpec((LANES, LANES), const),
        pl.BlockSpec((1, SGU_W), const),
    ]
    widths = ((Q_W, BF16), (KV_W, BF16), (KV_W, BF16), (SGU_W, F32), (SGU_W, BF16), (POOL_W, F32))
    out_specs = [pl.BlockSpec((tm, w), row) for w, _ in widths]
    out_shape = [jax.ShapeDtypeStruct((T, w), dt) for w, dt in widths]
    return in_specs, out_specs, out_shape


def _inproj(x2, params, S, tm):
    T, D = x2.shape
    nseq = S // tm
    row = lambda i: (i, 0)
    in_specs, out_specs, out_shape = _project_specs(
        T, D, tm, row, lambda i: (i % nseq, 0), lambda i: (0, 0))
    return pl.pallas_call(
        _inproj_kernel,
        grid=(T // tm,),
        in_specs=[pl.BlockSpec((tm, D), row)] + in_specs,
        out_specs=out_specs,
        out_shape=out_shape,
        compiler_params=_cparams("parallel"),
        name="inproj",
    )(x2, *params)


def _attn_kernel(q_ref, k_ref, v_ref, o_ref, kz_ref, vzt_ref, s_ref, qt_ref):
    @pl.when(pl.program_id(1) == 0)
    def _():
        lane = lax.broadcasted_iota(jnp.int32, k_ref.shape, 1)
        low = lane < HEAD_DIM
        k = k_ref[...].astype(F32)
        kr = pltpu.roll(k, HEAD_DIM, 1)
        kz_ref[0] = jnp.where(low, k, 0.0).astype(BF16)
        kz_ref[1] = jnp.where(low, 0.0, kr).astype(BF16)
        kz_ref[2] = jnp.where(low, kr, 0.0).astype(BF16)
        kz_ref[3] = jnp.where(low, 0.0, k).astype(BF16)
        vt = v_ref[...].astype(F32).T
        vtr = pltpu.roll(vt, HEAD_DIM, 0)
        row = lax.broadcasted_iota(jnp.int32, vt.shape, 0)
        top = row < HEAD_DIM
        ones_lo = jnp.where(row == 0, 1.0, 0.0)
        ones_hi = jnp.where(row == HEAD_DIM, 1.0, 0.0)
        vzt_ref[0] = jnp.where(top, vt, ones_hi).astype(BF16)
        vzt_ref[1] = jnp.where(top, ones_lo, vtr).astype(BF16)
        vzt_ref[2] = jnp.where(top, vtr, ones_hi).astype(BF16)
        vzt_ref[3] = jnp.where(top, ones_lo, vt).astype(BF16)

    tq = s_ref.shape[2]
    S = k_ref.shape[0]
    nchunk = S // KEY_CHUNK
    sub = F32_SUBLANES
    slot0 = jnp.minimum(pl.program_id(1), 0)

    def unit_parts(unit):
        qt, head = divmod(unit, N_HEADS)
        pair, parity = divmod(head, 2)
        kvi = 2 * ((2 * pair) // (N_HEADS // N_KV_HEADS)) + parity
        return slice(qt * tq, (qt + 1) * tq), slice(pair * LANES, (pair + 1) * LANES), parity, kvi

    npair = N_HEADS // 2
    for qt in range(q_ref.shape[0] // tq):
        for pair in range(npair):
            q_pair = q_ref[qt * tq:(qt + 1) * tq, pair * LANES:(pair + 1) * LANES]
            qt_ref[qt * npair + pair] = q_pair.astype(F32).T.astype(BF16)

    def scores(unit, c, mrun):
        kvi = unit_parts(unit)[3]
        rows = slice(c * KEY_CHUNK, (c + 1) * KEY_CHUNK)
        s = _dot(kz_ref[kvi, rows, :], qt_ref[unit // 2])
        s_ref[slot0 + unit % 2, rows, :] = s
        m = jnp.max(s.reshape(KEY_CHUNK // sub, sub, tq), axis=0)
        return m if mrun is None else jnp.maximum(mrun, m)

    def weighted(unit, c, m8, acc):
        kvi = unit_parts(unit)[3]
        rows = slice(c * KEY_CHUNK, (c + 1) * KEY_CHUNK)
        s3 = s_ref[slot0 + unit % 2, rows, :].reshape(KEY_CHUNK // sub, sub, tq)
        e = jnp.exp2(s3 - m8[None]).reshape(KEY_CHUNK, tq).astype(BF16)
        d = _dot(vzt_ref[kvi, :, rows], e)
        return d if acc is None else acc + d

    nunit = (q_ref.shape[0] // tq) * N_HEADS
    top = lax.broadcasted_iota(jnp.int32, (LANES, tq), 0) < HEAD_DIM
    mrun = None
    for c in range(nchunk):
        mrun = scores(0, c, mrun)
    even = None
    for unit in range(nunit):
        m8 = jnp.broadcast_to(jnp.max(mrun, axis=0, keepdims=True), (sub, tq))
        mnext, acc = None, None
        for c in range(nchunk):
            if unit + 1 < nunit:
                mnext = scores(unit + 1, c, mnext)
            acc = weighted(unit, c, m8, acc)
        mrun = mnext
        qrows, lanes, parity, _ = unit_parts(unit)
        if parity == 0:
            even = acc * (1.0 / acc[HEAD_DIM:HEAD_DIM + 1])
        else:
            odd = acc * (1.0 / acc[0:1])
            o_ref[qrows, lanes] = jnp.where(top, even, odd).T.astype(BF16)


def _attention(q, k, v, B, S, tq):
    T = q.shape[0]
    nq = S // tq
    return pl.pallas_call(
        _attn_kernel,
        grid=(B, nq),
        in_specs=[
            pl.BlockSpec((tq, Q_W), lambda b, i: (b * nq + i, 0)),
            pl.BlockSpec((S, KV_W), lambda b, i: (b, 0)),
            pl.BlockSpec((S, KV_W), lambda b, i: (b, 0)),
        ],
        out_specs=pl.BlockSpec((tq, Q_W), lambda b, i: (b * nq + i, 0)),
        out_shape=jax.ShapeDtypeStruct((T, Q_W), BF16),
        scratch_shapes=[pltpu.VMEM((4, S, KV_W), BF16), pltpu.VMEM((4, KV_W, S), BF16),
                        pltpu.VMEM((2, S, QUERY_SUB), F32),
                        pltpu.VMEM((tq // QUERY_SUB * (N_HEADS // 2), LANES, QUERY_SUB), BF16)],
        compiler_params=_cparams("parallel", "arbitrary"),
        name="attention",
    )(q, k, v)


def _merge_kernel(x_ref, gmix_ref, wgl_ref, attn_ref, u_ref, vn_ref, p_ref, pprev_ref, pnext_ref,
                  wsp_ref, bsp_ref, wpool_ref, pscale_ref, wao_ref, wso_ref, wpo_ref, wout_ref,
                  gffn_ref, wr_ref, x1_ref, h2_ref, lg_ref, *, S):
    ts = x_ref.shape[0]

    vn = vn_ref[...]
    lane_b = lax.broadcasted_iota(jnp.int32, (SGU_CHUNK, SGU_W), 1) // SGU_HEAD
    zs = []
    for c in range(ts // SGU_CHUNK):
        vc = vn[c * SGU_CHUNK:(c + 1) * SGU_CHUNK]
        stacked = jnp.concatenate(
            [jnp.where(lane_b == g, vc, jnp.zeros_like(vc)) for g in range(SGU_GROUPS)], axis=0)
        zs.append(_dot(wsp_ref[...], stacked) + bsp_ref[...])
    sgu = (u_ref[...] * jnp.concatenate(zs, axis=0)).astype(BF16)

    pos0 = (pl.program_id(0) % (S // ts)) * ts
    p = p_ref[...]
    prev = jnp.where(pos0 == 0, 0.0, pprev_ref[...])
    nxt = jnp.where(pos0 + ts == S, 0.0, pnext_ref[...])
    ext = jnp.concatenate([prev, p, nxt], axis=0)
    n = ts + 2 * POOL_HALO
    s2 = ext + pltpu.roll(ext, 1, 0)
    s4 = pltpu.roll(s2, 1, 0) + pltpu.roll(s2, n - 1, 0)
    s8 = pltpu.roll(s4, 2, 0) + pltpu.roll(s4, n - 2, 0)
    s16 = pltpu.roll(s8, 4, 0) + pltpu.roll(s8, n - 4, 0)
    mid = slice(POOL_HALO, POOL_HALO + ts)
    grp = lax.broadcasted_iota(jnp.int32, (ts, POOL_W), 1) // POOL_HEAD
    t = pos0 + lax.broadcasted_iota(jnp.int32, (ts, POOL_W), 0)
    half = jnp.left_shift(1, grp)
    cnt = jnp.minimum(t + half - 1, S - 1) - jnp.maximum(t - half, 0) + 1
    wsum = jnp.where(grp == 0, s2[mid], jnp.where(grp == 1, s4[mid],
                                                  jnp.where(grp == 2, s8[mid], s16[mid])))
    d = (wsum / cnt.astype(F32) - p).astype(BF16)
    pooled = (_dot(d, wpool_ref[...]) * pscale_ref[...]).astype(BF16)

    x = x_ref[...]
    D = x.shape[1]
    h = (_rms(x) * gmix_ref[...]).astype(BF16)
    br_a = _dot(attn_ref[...], wao_ref[...])
    br_b = _dot(sgu, wso_ref[...])
    br_c = _dot(pooled, wpo_ref[...])
    gl = _dot(h, wgl_ref[...])
    gate = lambda j: 0.5 * (1.0 + jnp.tanh(0.5 * gl[:, j * D:(j + 1) * D]))
    merged = (gate(0) * br_a + gate(1) * br_b + gate(2) * br_c).astype(BF16)
    x1 = x + _dot(merged, wout_ref[...])
    x1_ref[...] = x1
    h2 = (_rms(x1) * gffn_ref[...]).astype(BF16)
    h2_ref[...] = h2
    lg_ref[...] = _dot(h2, wr_ref[...])


def _merge(x2, gmix, w_gl, attn, u, vn, p, wsp, bsp, wpool, pscale, wao, wso, wpo, wout, gffn, wr,
           S, ts):
    T, D = x2.shape
    row = lambda i: (i, 0)
    const = lambda i: (0, 0)
    hb = ts // POOL_HALO
    nhb = T // POOL_HALO
    full = lambda a: pl.BlockSpec(a.shape, const)
    return pl.pallas_call(
        functools.partial(_merge_kernel, S=S),
        grid=(T // ts,),
        in_specs=[
            pl.BlockSpec((ts, D), row),
            full(gmix),
            full(w_gl),
            pl.BlockSpec((ts, Q_W), row),
            pl.BlockSpec((ts, SGU_W), row),
            pl.BlockSpec((ts, SGU_W), row),
            pl.BlockSpec((ts, POOL_W), row),
            pl.BlockSpec((POOL_HALO, POOL_W), lambda i: (jnp.maximum(i * hb - 1, 0), 0)),
            pl.BlockSpec((POOL_HALO, POOL_W), lambda i: (jnp.minimum((i + 1) * hb, nhb - 1), 0)),
            full(wsp), full(bsp), full(wpool), full(pscale), full(wao), full(wso), full(wpo),
            full(wout), full(gffn), full(wr),
        ],
        out_specs=[
            pl.BlockSpec((ts, D), row),
            pl.BlockSpec((ts, D), row),
            pl.BlockSpec((ts, LANES), row),
        ],
        out_shape=[
            jax.ShapeDtypeStruct((T, D), F32),
            jax.ShapeDtypeStruct((T, D), BF16),
            jax.ShapeDtypeStruct((T, LANES), F32),
        ],
        compiler_params=_cparams("parallel"),
        name="merge",
    )(x2, gmix, w_gl, attn, u, vn, p, p, p, wsp, bsp, wpool, pscale, wao, wso, wpo, wout, gffn, wr)


def _excl_cumsum(m01, tri_ones):
    off = jnp.zeros((m01.shape[0], LANES), F32)
    pieces = []
    for j in range(m01.shape[1] // LANES):
        res = _dot(m01[:, j * LANES:(j + 1) * LANES].astype(BF16), tri_ones)
        pieces.append(res[:, :LANES] + off)
        off = off + res[:, LANES:]
    return jnp.concatenate(pieces, axis=1)


def _tile_lists(offs, nblk, list_len):
    E = offs.shape[0]
    lane = lax.broadcasted_iota(jnp.int32, offs.shape, 1)
    hi = pltpu.roll(offs, LANES - 1, 1)
    g_lo = jnp.floor(offs * (1.0 / SLOT_TILE))
    g_hi = jnp.floor((hi + (SLOT_TILE - 1)) * (1.0 / SLOT_TILE))
    cnt = jnp.where((hi > offs) & (lane < nblk), g_hi - g_lo, 0.0)
    below = (lax.broadcasted_iota(jnp.int32, (E, E), 1) < lax.broadcasted_iota(jnp.int32, (E, E), 0))
    base = _dot(jnp.where(below, 1.0, 0.0).astype(BF16), cnt.astype(BF16))
    end = base + cnt
    t = lax.broadcasted_iota(jnp.int32, (1, list_len), 1).astype(F32)
    row_e = lax.broadcasted_iota(jnp.int32, (E, list_len), 0).astype(F32)
    blocks = []
    for j in range(nblk):
        col = slice(j, j + 1)
        e_t = jnp.sum(jnp.where(end[:, col] <= t, 1.0, 0.0), axis=0, keepdims=True)
        shift = jnp.sum(jnp.where(row_e == e_t, g_lo[:, col] - base[:, col], 0.0), axis=0, keepdims=True)
        valid = e_t < E
        g_t = jnp.where(valid, t + shift, 0.0)
        total = jnp.broadcast_to(end[E - 1:E, col], (1, list_len))
        rows = [jnp.where(valid, e_t, 0.0), g_t, jnp.where(valid, g_t * SLOT_TILE, float(NO_SLOT)),
                total, jnp.zeros((4, list_len), F32)]
        blocks.append(jnp.concatenate(rows, axis=0).astype(jnp.int32)[None])
    return jnp.concatenate(blocks, axis=0)


def _route_kernel(lg_ref, tri_ref, rank_ref, aff_ref, lists_ref, *, cap):
    nb, S, _ = lg_ref.shape
    E = N_EXPERTS
    rows = []
    for b in range(nb):
        lt = lg_ref[b].T[:E]
        e = jnp.exp(lt - jnp.max(lt, axis=0, keepdims=True))
        rows.append(e / jnp.sum(e, axis=0, keepdims=True))
    aff = jnp.concatenate(rows, axis=0)

    def refine(i, thr):
        cand = thr | jnp.left_shift(jnp.int32(1), 30 - i)
        cnt = jnp.sum(jnp.where(aff >= pltpu.bitcast(cand, F32), 1.0, 0.0), axis=1, keepdims=True)
        return jnp.where(cnt >= cap, cand, thr)

    thr = pltpu.bitcast(lax.fori_loop(0, 31, refine, jnp.zeros((nb * E, 1), jnp.int32)), F32)
    gt = jnp.where(aff > thr, 1.0, 0.0)
    eq = jnp.where(aff == thr, 1.0, 0.0)
    need = cap - jnp.sum(gt, axis=1, keepdims=True)
    tri = tri_ref[...]
    sel = gt + eq * jnp.where(_excl_cumsum(eq, tri) < need, 1.0, 0.0)
    before = _excl_cumsum(sel, tri)
    rank = jnp.where(sel > 0.5, before, -1.0)
    nblk = S // TOKEN_BLOCK
    lane = lax.broadcasted_iota(jnp.int32, (nb * E, LANES), 1)
    offs = jnp.where(lane == nblk, float(cap), 0.0)
    for j in range(nblk):
        offs = jnp.where(lane == j, before[:, j * TOKEN_BLOCK:j * TOKEN_BLOCK + 1], offs)
    list_len = lists_ref.shape[-1]
    for b in range(nb):
        rank_ref[b] = rank[b * E:(b + 1) * E]
        aff_ref[b] = aff[b * E:(b + 1) * E]
        lists_ref[b] = _tile_lists(offs[b * E:(b + 1) * E], nblk, list_len)


def _list_len(cap):
    worst = N_EXPERTS * (cap // SLOT_TILE + 2)
    return -(-worst // LANES) * LANES


def _route(lg3, tri, cap, nb):
    B, S, _ = lg3.shape
    nblk = S // TOKEN_BLOCK
    blk = lambda b: (b, 0, 0)
    lists_shape = (nblk, 8, _list_len(cap))
    return pl.pallas_call(
        functools.partial(_route_kernel, cap=cap),
        grid=(B // nb,),
        in_specs=[pl.BlockSpec((nb, S, LANES), blk), pl.BlockSpec(tri.shape, lambda b: (0, 0))],
        out_specs=[
            pl.BlockSpec((nb, N_EXPERTS, S), blk),
            pl.BlockSpec((nb, N_EXPERTS, S), blk),
            pl.BlockSpec((nb,) + lists_shape, lambda b: (b, 0, 0, 0)),
        ],
        out_shape=[
            jax.ShapeDtypeStruct((B, N_EXPERTS, S), F32),
            jax.ShapeDtypeStruct((B, N_EXPERTS, S), F32),
            jax.ShapeDtypeStruct((B,) + lists_shape, jnp.int32),
        ],
        compiler_params=_cparams("parallel"),
        name="route",
    )(lg3, tri)


def _chunk_tiles(list_ref, k, c):
    out = []
    for i in range(CHUNK_TILES):
        t = c * CHUNK_TILES + i
        out.append((list_ref[k, 0, t], list_ref[k, 1, t], list_ref[k, 2, t]))
    return out


def _num_chunks(list_ref, k):
    return jnp.right_shift(list_ref[k, 3, 0] + CHUNK_TILES - 1, CHUNK_SHIFT)


def _for_each_chunk(list_ref, body):
    for k in range(list_ref.shape[0]):
        body(k, 0, True)
    for k in range(list_ref.shape[0]):
        def rest(c, carry, k=k):
            body(k, c, False)
            return carry
        lax.fori_loop(1, _num_chunks(list_ref, k), rest, 0)


def _tile_rows(g_t):
    return pl.ds(pl.multiple_of(g_t * SLOT_TILE, SLOT_TILE), SLOT_TILE)


def _onehot(entries, rank_ref, j, weight_ref=None):
    tb = rank_ref.shape[-1]
    sub = lax.broadcasted_iota(jnp.int32, (SLOT_TILE, tb), 0)
    tiles = []
    for e_t, _, slot0 in entries:
        hit = rank_ref[e_t, pl.ds(j, 1), :] == (slot0 + sub).astype(F32)
        val = 1.0 if weight_ref is None else weight_ref[e_t, pl.ds(j, 1), :]
        tiles.append(jnp.where(hit, val, 0.0).astype(BF16))
    return jnp.concatenate(tiles, axis=0)


def _gather_kernel(list_ref, h2_ref, rank_ref, xs_ref):
    tb = rank_ref.shape[-1]
    step = pl.program_id(1)

    @pl.when(step == 0)
    def _():
        xs_ref[...] = jnp.zeros_like(xs_ref)

    def chunk(k, c, first):
        entries = _chunk_tiles(list_ref, k, c)
        onehot = _onehot(entries, rank_ref, step * list_ref.shape[0] + k)
        rows = _dot(onehot, h2_ref[k * tb:(k + 1) * tb, :]).astype(BF16)
        for i, (e_t, g_t, _) in enumerate(entries):
            xs_ref[e_t, _tile_rows(g_t), :] += rows[i * SLOT_TILE:(i + 1) * SLOT_TILE, :]

    _for_each_chunk(list_ref, chunk)


def _list_spec(lists, nblocks):
    return pl.BlockSpec((None, nblocks) + lists.shape[2:], lambda b, j: (b, j, 0, 0),
                        memory_space=pltpu.SMEM)


def _gather(h2, rank4, lists, cap):
    B, S, D = h2.shape
    _, E, nblk, tb = rank4.shape
    rows = BLOCKS_PER_STEP * tb
    return pl.pallas_call(
        _gather_kernel,
        grid=(B, nblk // BLOCKS_PER_STEP),
        in_specs=[
            _list_spec(lists, BLOCKS_PER_STEP),
            pl.BlockSpec((None, rows, D), lambda b, j: (b, j, 0)),
            pl.BlockSpec((None, E, nblk, tb), lambda b, j: (b, 0, 0, 0)),
        ],
        out_specs=pl.BlockSpec((None, E, cap, D), lambda b, j: (b, 0, 0, 0)),
        out_shape=jax.ShapeDtypeStruct((B, E, cap, D), BF16),
        compiler_params=_cparams("arbitrary", "arbitrary"),
        name="gather",
    )(lists, h2, rank4)


def _ffn_kernel(xs_ref, wg_ref, wu_ref, wd_ref, y_ref, wg_b, wu_b, wd_b):
    @pl.when(pl.program_id(1) == 0)
    def _():
        wg_b[...] = wg_ref[...].astype(BF16)
        wu_b[...] = wu_ref[...].astype(BF16)
        wd_b[...] = wd_ref[...].astype(BF16)

    nb, cap, D = xs_ref.shape
    xs = xs_ref[...].reshape(nb * cap, D)
    a = _dot(xs, wg_b[...])
    b = _dot(xs, wu_b[...])
    hidden = (a * jax.nn.sigmoid(a) * b).astype(BF16)
    y_ref[...] = _dot(hidden, wd_b[...]).astype(BF16).reshape(nb, cap, D)


def _ffn(xs, wg, wu, wd, layer, nb):
    B, E, cap, D = xs.shape
    F = wg.shape[3]
    blk = pl.BlockSpec((nb, None, cap, D), lambda e, b: (b, e, 0, 0))
    return pl.pallas_call(
        _ffn_kernel,
        grid=(E, B // nb),
        in_specs=[
            blk,
            pl.BlockSpec((None, None, D, F), lambda e, b: (layer, e, 0, 0)),
            pl.BlockSpec((None, None, D, F), lambda e, b: (layer, e, 0, 0)),
            pl.BlockSpec((None, None, F, D), lambda e, b: (layer, e, 0, 0)),
        ],
        out_specs=blk,
        out_shape=jax.ShapeDtypeStruct((B, E, cap, D), BF16),
        scratch_shapes=[pltpu.VMEM((D, F), BF16), pltpu.VMEM((D, F), BF16), pltpu.VMEM((F, D), BF16)],
        compiler_params=_cparams("arbitrary", "arbitrary"),
        name="ffn",
    )(xs, wg, wu, wd)


def _scatter_rows(list_ref, y_ref, rank_ref, aff_ref, x1_ref, out_ref):
    tb = rank_ref.shape[-1]
    step = pl.program_id(1)

    def chunk(k, c, first):
        entries = _chunk_tiles(list_ref, k, c)
        weights = _onehot(entries, rank_ref, step * list_ref.shape[0] + k, aff_ref)
        ys = jnp.concatenate([y_ref[e_t, _tile_rows(g_t), :] for e_t, g_t, _ in entries], axis=0)
        add = lax.dot_general(weights, ys, (((0,), (0,)), ((), ())), preferred_element_type=F32)
        rows = slice(k * tb, (k + 1) * tb)
        out_ref[rows, :] = add + (x1_ref[rows, :] if first else out_ref[rows, :])

    _for_each_chunk(list_ref, chunk)


def _scatter_specs(y, rank4, lists, nblocks):
    B, E, cap, D = y.shape
    _, _, nblk, tb = rank4.shape
    steps = nblk // nblocks
    rows = pl.BlockSpec((nblocks * tb, D), lambda b, j: (b * steps + j, 0))
    per_seq = pl.BlockSpec((None, E, nblk, tb), lambda b, j: (b, 0, 0, 0))
    in_specs = [_list_spec(lists, nblocks), pl.BlockSpec((None, E, cap, D), lambda b, j: (b, 0, 0, 0)),
                per_seq, per_seq, rows]
    return (B, steps), in_specs, rows


def _scatter_kernel(list_ref, y_ref, rank_ref, aff_ref, x1_ref, gout_ref, out_ref, *, norm_out):
    _scatter_rows(list_ref, y_ref, rank_ref, aff_ref, x1_ref, out_ref)
    if norm_out:
        out_ref[...] = _rms(out_ref[...]) * gout_ref[...]


def _scatter(y, rank4, aff4, x1, lists, gout, norm_out):
    grid, in_specs, rows = _scatter_specs(y, rank4, lists, BLOCKS_PER_STEP)
    return pl.pallas_call(
        functools.partial(_scatter_kernel, norm_out=norm_out),
        grid=grid,
        in_specs=in_specs + [pl.BlockSpec(gout.shape, lambda b, j: (0, 0))],
        out_specs=rows,
        out_shape=jax.ShapeDtypeStruct(x1.shape, F32),
        compiler_params=_cparams("arbitrary", "arbitrary"),
        name="scatter",
    )(lists, y, rank4, aff4, x1, gout)


def _rope_tables(S, g, scale):
    rows = S // GRID_W
    row = jnp.broadcast_to(jnp.arange(rows, dtype=F32)[:, None], (rows, GRID_W)).reshape(-1)
    col = jnp.broadcast_to(jnp.arange(GRID_W, dtype=F32)[None, :], (rows, GRID_W)).reshape(-1)
    part = HEAD_DIM // 2
    freqs = ROPE_THETA ** (-jnp.arange(0, part, 2, dtype=F32) / part)
    ang = jnp.stack([row[:, None] * freqs, col[:, None] * freqs], axis=1)
    cos, sin = jnp.cos(ang), jnp.sin(ang)
    c64 = jnp.concatenate([cos[:, 0], cos[:, 0], cos[:, 1], cos[:, 1]], axis=-1)
    s64 = jnp.concatenate([-sin[:, 0], sin[:, 0], -sin[:, 1], sin[:, 1]], axis=-1)
    g_swapped = g.reshape(2, 2, HEAD_DIM // 4)[:, ::-1, :].reshape(HEAD_DIM)
    gc = jnp.tile(c64 * g[None, :], (1, 2)) * scale
    gs = jnp.tile(s64 * g_swapped[None, :], (1, 2)) * scale
    return gc, gs


def _block_diag(blocks):
    n = blocks.shape[0]
    m = blocks.shape[1]
    eye = jnp.eye(n, dtype=blocks.dtype)
    return (eye[:, None, :, None] * blocks[:, :, None, :]).reshape(n * m, n * blocks.shape[2])


def kernel(x, g_mix, w_in, g_q, g_k, g_sgu, w_spatial, b_spatial, w_pool, pool_scale, w_attn_o,
           w_sgu_o, w_pool_o, w_out, g_ffn, w_router, w_gate_e, w_up_e, w_down_e, g_final):
    B, S, D = x.shape
    depth = w_in.shape[0]
    cap = CAPACITY_FACTOR * S // N_EXPERTS
    x2 = x.reshape(B * S, D)

    bd = _block_diag(jnp.ones((2, HEAD_DIM, HEAD_DIM), F32)).astype(BF16)
    k_i = lax.broadcasted_iota(jnp.int32, (LANES, LANES), 0)
    l_i = lax.broadcasted_iota(jnp.int32, (LANES, LANES), 1)
    tri = jnp.concatenate([(k_i < l_i).astype(BF16), jnp.ones((LANES, LANES), BF16)], axis=1)

    for l in range(depth):
        gcq, gsq = _rope_tables(S, g_q[l], HEAD_DIM ** -0.5 * math.log2(math.e))
        gck, gsk = _rope_tables(S, g_k[l], 1.0)
        w_gl = w_in[l, :, A_W:].astype(BF16)
        gmix = g_mix[l][None, :]
        proj = (gmix, w_in[l, :, :A_W].astype(BF16), gcq, gsq, gck, gsk, bd, g_sgu[l][None, :])
        q, k, v, u, vn, p = _inproj(x2, proj, S, INPROJ_ROWS)
        attn = _attention(q, k, v, B, S, ATTN_QUERY_SUBS * QUERY_SUB)

        wsp = w_spatial[l].transpose(1, 0, 2).reshape(SGU_CHUNK, SGU_GROUPS * SGU_CHUNK).astype(BF16)
        bsp = jnp.repeat(b_spatial[l].T, SGU_HEAD, axis=1)
        wpool = _block_diag(w_pool[l]).astype(BF16)
        wr = jnp.pad(w_router[l], ((0, 0), (0, LANES - N_EXPERTS))).astype(BF16)
        x1, h2, lg = _merge(
            x2, gmix, w_gl, attn, u, vn, p, wsp, bsp, wpool, pool_scale[l][None, :],
            w_attn_o[l].astype(BF16), w_sgu_o[l].astype(BF16), w_pool_o[l].astype(BF16),
            w_out[l].astype(BF16), g_ffn[l][None, :], wr, S, MERGE_ROWS)

        rank, aff, lists = _route(lg.reshape(B, S, LANES), tri, cap, math.gcd(B, SEQS_PER_STEP))
        per_block = (B, N_EXPERTS, S // TOKEN_BLOCK, TOKEN_BLOCK)
        rank4, aff4 = rank.reshape(per_block), aff.reshape(per_block)
        xs = _gather(h2.reshape(B, S, D), rank4, lists, cap)
        y = _ffn(xs, w_gate_e, w_up_e, w_down_e, l, math.gcd(B, SEQS_PER_STEP))
        x2 = _scatter(y, rank4, aff4, x1, lists, g_final[None, :], norm_out=(l == depth - 1))

    return x2.reshape(B, S, D)
```

```python
import functools
import math

import jax
import jax.numpy as jnp
from jax import lax
from jax.experimental import pallas as pl
from jax.experimental.pallas import tpu as pltpu

F32 = jnp.float32
BF16 = jnp.bfloat16

GRID_W = 64
N_HEADS = 8
N_KV_HEADS = 2
HEAD_DIM = 64
ROPE_THETA = 10000.0
SGU_GROUPS = 4
SGU_HEAD = 64
SGU_W = SGU_GROUPS * SGU_HEAD
SGU_CHUNK = 128
POOL_WINDOWS = (2, 4, 8, 16)
POOL_HEAD = 64
POOL_W = len(POOL_WINDOWS) * POOL_HEAD
POOL_HALO = 8
KEY_CHUNK = 512
QUERY_SUB = 512
TOKEN_BLOCK = 256
BLOCKS_PER_STEP = 4
GATHER_BLOCKS = 8
INPROJ_ROWS = 1024
MERGE_ROWS = 1024
ATTN_QUERY_SUBS = 2
SEQS_PER_STEP = 4
TILE_SHIFT = 4
SLOT_TILE = 1 << TILE_SHIFT
CHUNK_SHIFT = 6
CHUNK_TILES = 1 << CHUNK_SHIFT
NO_SLOT = -1024
Q_W = N_HEADS * HEAD_DIM
KV_W = N_KV_HEADS * HEAD_DIM
N_BRANCHES = 3
N_EXPERTS = 16
CAPACITY_FACTOR = 2
EPS = 1e-6

LANES = 128
F32_SUBLANES = 8
A_W = Q_W + 2 * KV_W + 2 * SGU_W + POOL_W
VMEM_LIMIT = 56 * 1024 * 1024


def _cparams(*sem):
    return pltpu.CompilerParams(dimension_semantics=sem, vmem_limit_bytes=VMEM_LIMIT)


def _dot(a, b):
    return jnp.dot(a, b, preferred_element_type=F32)


def _dot_nt(a, b):
    return lax.dot_general(a, b, (((1,), (1,)), ((), ())), preferred_element_type=F32)


def _rms(x):
    return x * lax.rsqrt(jnp.mean(x * x, axis=-1, keepdims=True) + EPS)


def _head_norm_rope(xq, gc, gs, bd):
    sq = xq * xq
    hi = sq.astype(BF16)
    lo = (sq - hi.astype(F32)).astype(BF16)
    ss = _dot(hi, bd) + _dot(lo, bd)
    r = lax.rsqrt(ss * (1.0 / HEAD_DIM) + EPS)
    lane = lax.broadcasted_iota(jnp.int32, xq.shape, 1)
    first_half = (lane % 32) < 16
    partner = jnp.where(first_half, pltpu.roll(xq, LANES - 16, 1), pltpu.roll(xq, 16, 1))
    return r * (xq * gc + partner * gs)


def _project_rows(x, gmix_ref, w_ref, gcq_ref, gsq_ref, gck_ref, gsk_ref, bd_ref, gsgu_ref,
                  q_ref, k_ref, v_ref, u_ref, vn_ref, p_ref):
    h = (_rms(x) * gmix_ref[...]).astype(BF16)
    acc = _dot(h, w_ref[...])
    bd = bd_ref[...]
    for c in range(Q_W // LANES):
        sl = slice(c * LANES, (c + 1) * LANES)
        q_ref[:, sl] = _head_norm_rope(acc[:, sl], gcq_ref[...], gsq_ref[...], bd).astype(BF16)
    o = Q_W
    k_ref[...] = _head_norm_rope(acc[:, o:o + KV_W], gck_ref[...], gsk_ref[...], bd).astype(BF16)
    o += KV_W
    v_ref[...] = acc[:, o:o + KV_W].astype(BF16)
    o += KV_W
    u_ref[...] = acc[:, o:o + SGU_W]
    o += SGU_W
    vn_ref[...] = (_rms(acc[:, o:o + SGU_W]) * gsgu_ref[...]).astype(BF16)
    o += SGU_W
    p_ref[...] = acc[:, o:o + POOL_W]


def _inproj_kernel(x_ref, *refs):
    _project_rows(x_ref[...], *refs)


def _project_specs(T, D, tm, row, pos, const):
    in_specs = [
        pl.BlockSpec((1, D), const),
        pl.BlockSpec((D, A_W), const),
        pl.BlockSpec((tm, LANES), pos),
        pl.BlockSpec((tm, LANES), pos),
        pl.BlockSpec((tm, LANES), pos),
        pl.BlockSpec((tm, LANES), pos),
        pl.BlockSpec((LANES, LANES), const),
        pl.BlockSpec((1, SGU_W), const),
    ]
    widths = ((Q_W, BF16), (KV_W, BF16), (KV_W, BF16), (SGU_W, F32), (SGU_W, BF16), (POOL_W, F32))
    out_specs = [pl.BlockSpec((tm, w), row) for w, _ in widths]
    out_shape = [jax.ShapeDtypeStruct((T, w), dt) for w, dt in widths]
    return in_specs, out_specs, out_shape


def _inproj(x2, params, S, tm):
    T, D = x2.shape
    nseq = S // tm
    row = lambda i: (i, 0)
    in_specs, out_specs, out_shape = _project_specs(
        T, D, tm, row, lambda i: (i % nseq, 0), lambda i: (0, 0))
    return pl.pallas_call(
        _inproj_kernel,
        grid=(T // tm,),
        in_specs=[pl.BlockSpec((tm, D), row)] + in_specs,
        out_specs=out_specs,
        out_shape=out_shape,
        compiler_params=_cparams("parallel"),
        name="inproj",
    )(x2, *params)


def _attn_kernel(q_ref, k_ref, v_ref, o_ref, kz_ref, vzt_ref, s_ref, qt_ref):
    @pl.when(pl.program_id(1) == 0)
    def _():
        lane = lax.broadcasted_iota(jnp.int32, k_ref.shape, 1)
        low = lane < HEAD_DIM
        k = k_ref[...].astype(F32)
        kr = pltpu.roll(k, HEAD_DIM, 1)
        kz_ref[0] = jnp.where(low, k, 0.0).astype(BF16)
        kz_ref[1] = jnp.where(low, 0.0, kr).astype(BF16)
        kz_ref[2] = jnp.where(low, kr, 0.0).astype(BF16)
        kz_ref[3] = jnp.where(low, 0.0, k).astype(BF16)
        vt = v_ref[...].astype(F32).T
        vtr = pltpu.roll(vt, HEAD_DIM, 0)
        row = lax.broadcasted_iota(jnp.int32, vt.shape, 0)
        top = row < HEAD_DIM
        ones_lo = jnp.where(row == 0, 1.0, 0.0)
        ones_hi = jnp.where(row == HEAD_DIM, 1.0, 0.0)
        vzt_ref[0] = jnp.where(top, vt, ones_hi).astype(BF16)
        vzt_ref[1] = jnp.where(top, ones_lo, vtr).astype(BF16)
        vzt_ref[2] = jnp.where(top, vtr, ones_hi).astype(BF16)
        vzt_ref[3] = jnp.where(top, ones_lo, vt).astype(BF16)

    tq = s_ref.shape[2]
    S = k_ref.shape[0]
    nchunk = S // KEY_CHUNK
    sub = F32_SUBLANES
    slot0 = jnp.minimum(pl.program_id(1), 0)

    def unit_parts(unit):
        qt, head = divmod(unit, N_HEADS)
        pair, parity = divmod(head, 2)
        kvi = 2 * ((2 * pair) // (N_HEADS // N_KV_HEADS)) + parity
        return slice(qt * tq, (qt + 1) * tq), slice(pair * LANES, (pair + 1) * LANES), parity, kvi

    npair = N_HEADS // 2
    for qt in range(q_ref.shape[0] // tq):
        for pair in range(npair):
            q_pair = q_ref[qt * tq:(qt + 1) * tq, pair * LANES:(pair + 1) * LANES]
            qt_ref[qt * npair + pair] = q_pair.astype(F32).T.astype(BF16)

    def scores(unit, c, mrun):
        kvi = unit_parts(unit)[3]
        rows = slice(c * KEY_CHUNK, (c + 1) * KEY_CHUNK)
        s = _dot(kz_ref[kvi, rows, :], qt_ref[unit // 2])
        s_ref[slot0 + unit % 2, rows, :] = s
        m = jnp.max(s.reshape(KEY_CHUNK // sub, sub, tq), axis=0)
        return m if mrun is None else jnp.maximum(mrun, m)

    def weighted(unit, c, m8, acc):
        kvi = unit_parts(unit)[3]
        rows = slice(c * KEY_CHUNK, (c + 1) * KEY_CHUNK)
        s3 = s_ref[slot0 + unit % 2, rows, :].reshape(KEY_CHUNK // sub, sub, tq)
        e = jnp.exp2(s3 - m8[None]).reshape(KEY_CHUNK, tq).astype(BF16)
        d = _dot(vzt_ref[kvi, :, rows], e)
        return d if acc is None else acc + d

    nunit = (q_ref.shape[0] // tq) * N_HEADS
    top = lax.broadcasted_iota(jnp.int32, (LANES, tq), 0) < HEAD_DIM
    mrun = None
    for c in range(nchunk):
        mrun = scores(0, c, mrun)
    even = None
    for unit in range(nunit):
        m8 = jnp.broadcast_to(jnp.max(mrun, axis=0, keepdims=True), (sub, tq))
        mnext, acc = None, None
        for c in range(nchunk):
            if unit + 1 < nunit:
                mnext = scores(unit + 1, c, mnext)
            acc = weighted(unit, c, m8, acc)
        mrun = mnext
        qrows, lanes, parity, _ = unit_parts(unit)
        if parity == 0:
            even = acc * (1.0 / acc[HEAD_DIM:HEAD_DIM + 1])
        else:
            odd = acc * (1.0 / acc[0:1])
            o_ref[qrows, lanes] = jnp.where(top, even, odd).T.astype(BF16)


def _attention(q, k, v, B, S, tq):
    T = q.shape[0]
    nq = S // tq
    return pl.pallas_call(
        _attn_kernel,
        grid=(B, nq),
        in_specs=[
            pl.BlockSpec((tq, Q_W), lambda b, i: (b * nq + i, 0)),
            pl.BlockSpec((S, KV_W), lambda b, i: (b, 0)),
            pl.BlockSpec((S, KV_W), lambda b, i: (b, 0)),
        ],
        out_specs=pl.BlockSpec((tq, Q_W), lambda b, i: (b * nq + i, 0)),
        out_shape=jax.ShapeDtypeStruct((T, Q_W), BF16),
        scratch_shapes=[pltpu.VMEM((4, S, KV_W), BF16), pltpu.VMEM((4, KV_W, S), BF16),
                        pltpu.VMEM((2, S, QUERY_SUB), F32),
                        pltpu.VMEM((tq // QUERY_SUB * (N_HEADS // 2), LANES, QUERY_SUB), BF16)],
        compiler_params=_cparams("parallel", "arbitrary"),
        name="attention",
    )(q, k, v)


def _merge_kernel(x_ref, gmix_ref, wgl_ref, attn_ref, u_ref, vn_ref, p_ref, pprev_ref, pnext_ref,
                  wsp_ref, bsp_ref, wpool_ref, pscale_ref, wao_ref, wso_ref, wpo_ref, wout_ref,
                  gffn_ref, wr_ref, x1_ref, h2_ref, lg_ref, *, S):
    ts = x_ref.shape[0]

    vn = vn_ref[...]
    lane_b = lax.broadcasted_iota(jnp.int32, (SGU_CHUNK, SGU_W), 1) // SGU_HEAD
    zs = []
    for c in range(ts // SGU_CHUNK):
        vc = vn[c * SGU_CHUNK:(c + 1) * SGU_CHUNK]
        stacked = jnp.concatenate(
            [jnp.where(lane_b == g, vc, jnp.zeros_like(vc)) for g in range(SGU_GROUPS)], axis=0)
        zs.append(_dot(wsp_ref[...], stacked) + bsp_ref[...])
    sgu = (u_ref[...] * jnp.concatenate(zs, axis=0)).astype(BF16)

    pos0 = (pl.program_id(0) % (S // ts)) * ts
    p = p_ref[...]
    prev = jnp.where(pos0 == 0, 0.0, pprev_ref[...])
    nxt = jnp.where(pos0 + ts == S, 0.0, pnext_ref[...])
    ext = jnp.concatenate([prev, p, nxt], axis=0)
    n = ts + 2 * POOL_HALO
    s2 = ext + pltpu.roll(ext, 1, 0)
    s4 = pltpu.roll(s2, 1, 0) + pltpu.roll(s2, n - 1, 0)
    s8 = pltpu.roll(s4, 2, 0) + pltpu.roll(s4, n - 2, 0)
    s16 = pltpu.roll(s8, 4, 0) + pltpu.roll(s8, n - 4, 0)
    mid = slice(POOL_HALO, POOL_HALO + ts)
    grp = lax.broadcasted_iota(jnp.int32, (ts, POOL_W), 1) // POOL_HEAD
    t = pos0 + lax.broadcasted_iota(jnp.int32, (ts, POOL_W), 0)
    half = jnp.left_shift(1, grp)
    cnt = jnp.minimum(t + half - 1, S - 1) - jnp.maximum(t - half, 0) + 1
    wsum = jnp.where(grp == 0, s2[mid], jnp.where(grp == 1, s4[mid],
                                                  jnp.where(grp == 2, s8[mid], s16[mid])))
    d = (wsum / cnt.astype(F32) - p).astype(BF16)
    pooled = (_dot(d, wpool_ref[...]) * pscale_ref[...]).astype(BF16)

    x = x_ref[...]
    D = x.shape[1]
    h = (_rms(x) * gmix_ref[...]).astype(BF16)
    br_a = _dot(attn_ref[...], wao_ref[...])
    br_b = _dot(sgu, wso_ref[...])
    br_c = _dot(pooled, wpo_ref[...])
    gl = _dot(h, wgl_ref[...])
    gate = lambda j: 0.5 * (1.0 + jnp.tanh(0.5 * gl[:, j * D:(j + 1) * D]))
    merged = (gate(0) * br_a + gate(1) * br_b + gate(2) * br_c).astype(BF16)
    x1 = x + _dot(merged, wout_ref[...])
    x1_ref[...] = x1
    h2 = (_rms(x1) * gffn_ref[...]).astype(BF16)
    h2_ref[...] = h2
    lg_ref[...] = _dot(h2, wr_ref[...])


def _merge(x2, gmix, w_gl, attn, u, vn, p, wsp, bsp, wpool, pscale, wao, wso, wpo, wout, gffn, wr,
           S, ts):
    T, D = x2.shape
    row = lambda i: (i, 0)
    const = lambda i: (0, 0)
    hb = ts // POOL_HALO
    nhb = T // POOL_HALO
    full = lambda a: pl.BlockSpec(a.shape, const)
    return pl.pallas_call(
        functools.partial(_merge_kernel, S=S),
        grid=(T // ts,),
        in_specs=[
            pl.BlockSpec((ts, D), row),
            full(gmix),
            full(w_gl),
            pl.BlockSpec((ts, Q_W), row),
            pl.BlockSpec((ts, SGU_W), row),
            pl.BlockSpec((ts, SGU_W), row),
            pl.BlockSpec((ts, POOL_W), row),
            pl.BlockSpec((POOL_HALO, POOL_W), lambda i: (jnp.maximum(i * hb - 1, 0), 0)),
            pl.BlockSpec((POOL_HALO, POOL_W), lambda i: (jnp.minimum((i + 1) * hb, nhb - 1), 0)),
            full(wsp), full(bsp), full(wpool), full(pscale), full(wao), full(wso), full(wpo),
            full(wout), full(gffn), full(wr),
        ],
        out_specs=[
            pl.BlockSpec((ts, D), row),
            pl.BlockSpec((ts, D), row),
            pl.BlockSpec((ts, LANES), row),
        ],
        out_shape=[
            jax.ShapeDtypeStruct((T, D), F32),
            jax.ShapeDtypeStruct((T, D), BF16),
            jax.ShapeDtypeStruct((T, LANES), F32),
        ],
        compiler_params=_cparams("parallel"),
        name="merge",
    )(x2, gmix, w_gl, attn, u, vn, p, p, p, wsp, bsp, wpool, pscale, wao, wso, wpo, wout, gffn, wr)


def _excl_cumsum(m01, tri_ones):
    off = jnp.zeros((m01.shape[0], LANES), F32)
    pieces = []
    for j in range(m01.shape[1] // LANES):
        res = _dot(m01[:, j * LANES:(j + 1) * LANES].astype(BF16), tri_ones)
        pieces.append(res[:, :LANES] + off)
        off = off + res[:, LANES:]
    return jnp.concatenate(pieces, axis=1)


def _tile_lists(offs, nblk, list_len):
    E = offs.shape[0]
    lane = lax.broadcasted_iota(jnp.int32, offs.shape, 1)
    hi = pltpu.roll(offs, LANES - 1, 1)
    g_lo = jnp.floor(offs * (1.0 / SLOT_TILE))
    g_hi = jnp.floor((hi + (SLOT_TILE - 1)) * (1.0 / SLOT_TILE))
    cnt = jnp.where((hi > offs) & (lane < nblk), g_hi - g_lo, 0.0)
    below = (lax.broadcasted_iota(jnp.int32, (E, E), 1) < lax.broadcasted_iota(jnp.int32, (E, E), 0))
    base = _dot(jnp.where(below, 1.0, 0.0).astype(BF16), cnt.astype(BF16))
    end = base + cnt
    t = lax.broadcasted_iota(jnp.int32, (1, list_len), 1).astype(F32)
    row_e = lax.broadcasted_iota(jnp.int32, (E, list_len), 0).astype(F32)
    blocks = []
    for j in range(nblk):
        col = slice(j, j + 1)
        e_t = jnp.sum(jnp.where(end[:, col] <= t, 1.0, 0.0), axis=0, keepdims=True)
        shift = jnp.sum(jnp.where(row_e == e_t, g_lo[:, col] - base[:, col], 0.0), axis=0, keepdims=True)
        valid = e_t < E
        g_t = jnp.where(valid, t + shift, 0.0)
        total = jnp.broadcast_to(end[E - 1:E, col], (1, list_len))
        rows = [jnp.where(valid, e_t, 0.0), g_t, jnp.where(valid, g_t * SLOT_TILE, float(NO_SLOT)),
                total, jnp.zeros((4, list_len), F32)]
        blocks.append(jnp.concatenate(rows, axis=0).astype(jnp.int32)[None])
    return jnp.concatenate(blocks, axis=0)


def _route_kernel(lg_ref, tri_ref, rank_ref, aff_ref, lists_ref, *, cap):
    nb, S, _ = lg_ref.shape
    E = N_EXPERTS
    rows = []
    for b in range(nb):
        lt = lg_ref[b].T[:E]
        e = jnp.exp(lt - jnp.max(lt, axis=0, keepdims=True))
        rows.append(e / jnp.sum(e, axis=0, keepdims=True))
    aff = jnp.concatenate(rows, axis=0)

    def refine(i, thr):
        cand = thr | jnp.left_shift(jnp.int32(1), 30 - i)
        cnt = jnp.sum(jnp.where(aff >= pltpu.bitcast(cand, F32), 1.0, 0.0), axis=1, keepdims=True)
        return jnp.where(cnt >= cap, cand, thr)

    thr = pltpu.bitcast(lax.fori_loop(0, 31, refine, jnp.zeros((nb * E, 1), jnp.int32)), F32)
    gt = jnp.where(aff > thr, 1.0, 0.0)
    eq = jnp.where(aff == thr, 1.0, 0.0)
    need = cap - jnp.sum(gt, axis=1, keepdims=True)
    tri = tri_ref[...]
    sel = gt + eq * jnp.where(_excl_cumsum(eq, tri) < need, 1.0, 0.0)
    before = _excl_cumsum(sel, tri)
    rank = jnp.where(sel > 0.5, before, -1.0)
    nblk = S // TOKEN_BLOCK
    lane = lax.broadcasted_iota(jnp.int32, (nb * E, LANES), 1)
    offs = jnp.where(lane == nblk, float(cap), 0.0)
    for j in range(nblk):
        offs = jnp.where(lane == j, before[:, j * TOKEN_BLOCK:j * TOKEN_BLOCK + 1], offs)
    list_len = lists_ref.shape[-1]
    for b in range(nb):
        rank_ref[b] = rank[b * E:(b + 1) * E]
        aff_ref[b] = aff[b * E:(b + 1) * E]
        lists_ref[b] = _tile_lists(offs[b * E:(b + 1) * E], nblk, list_len)


def _list_len(cap):
    worst = N_EXPERTS * (cap // SLOT_TILE + 2)
    return -(-worst // LANES) * LANES


def _route(lg3, tri, cap, nb):
    B, S, _ = lg3.shape
    nblk = S // TOKEN_BLOCK
    blk = lambda b: (b, 0, 0)
    lists_shape = (nblk, 8, _list_len(cap))
    return pl.pallas_call(
        functools.partial(_route_kernel, cap=cap),
        grid=(B // nb,),
        in_specs=[pl.BlockSpec((nb, S, LANES), blk), pl.BlockSpec(tri.shape, lambda b: (0, 0))],
        out_specs=[
            pl.BlockSpec((nb, N_EXPERTS, S), blk),
            pl.BlockSpec((nb, N_EXPERTS, S), blk),
            pl.BlockSpec((nb,) + lists_shape, lambda b: (b, 0, 0, 0)),
        ],
        out_shape=[
            jax.ShapeDtypeStruct((B, N_EXPERTS, S), F32),
            jax.ShapeDtypeStruct((B, N_EXPERTS, S), F32),
            jax.ShapeDtypeStruct((B,) + lists_shape, jnp.int32),
        ],
        compiler_params=_cparams("parallel"),
        name="route",
    )(lg3, tri)


def _chunk_tiles(list_ref, k, c):
    out = []
    for i in range(CHUNK_TILES):
        t = c * CHUNK_TILES + i
        out.append((list_ref[k, 0, t], list_ref[k, 1, t], list_ref[k, 2, t]))
    return out


def _num_chunks(list_ref, k):
    return jnp.right_shift(list_ref[k, 3, 0] + CHUNK_TILES - 1, CHUNK_SHIFT)


def _for_each_chunk(list_ref, body):
    for k in range(list_ref.shape[0]):
        body(k, 0, True)
    for k in range(list_ref.shape[0]):
        def rest(c, carry, k=k):
            body(k, c, False)
            return carry
        lax.fori_loop(1, _num_chunks(list_ref, k), rest, 0)


def _tile_rows(g_t):
    return pl.ds(pl.multiple_of(g_t * SLOT_TILE, SLOT_TILE), SLOT_TILE)


def _onehot(entries, rank_ref, j, weight_ref=None):
    tb = rank_ref.shape[-1]
    sub = lax.broadcasted_iota(jnp.int32, (SLOT_TILE, tb), 0)
    tiles = []
    for e_t, _, slot0 in entries:
        hit = rank_ref[e_t, pl.ds(j, 1), :] == (slot0 + sub).astype(F32)
        val = 1.0 if weight_ref is None else weight_ref[e_t, pl.ds(j, 1), :]
        tiles.append(jnp.where(hit, val, 0.0).astype(BF16))
    return jnp.concatenate(tiles, axis=0)


def _gather_kernel(list_ref, h2_ref, rank_ref, xs_ref):
    tb = rank_ref.shape[-1]
    step = pl.program_id(1)

    @pl.when(step == 0)
    def _():
        xs_ref[...] = jnp.zeros_like(xs_ref)

    def chunk(k, c, first):
        entries = _chunk_tiles(list_ref, k, c)
        onehot = _onehot(entries, rank_ref, step * list_ref.shape[0] + k)
        rows = _dot(onehot, h2_ref[k * tb:(k + 1) * tb, :]).astype(BF16)
        for i, (e_t, g_t, _) in enumerate(entries):
            xs_ref[e_t, _tile_rows(g_t), :] += rows[i * SLOT_TILE:(i + 1) * SLOT_TILE, :]

    _for_each_chunk(list_ref, chunk)


def _list_spec(lists, nblocks):
    return pl.BlockSpec((None, nblocks) + lists.shape[2:], lambda b, j: (b, j, 0, 0),
                        memory_space=pltpu.SMEM)


def _gather(h2, rank4, lists, cap):
    B, S, D = h2.shape
    _, E, nblk, tb = rank4.shape
    rows = GATHER_BLOCKS * tb
    return pl.pallas_call(
        _gather_kernel,
        grid=(B, nblk // GATHER_BLOCKS),
        in_specs=[
            _list_spec(lists, GATHER_BLOCKS),
            pl.BlockSpec((None, rows, D), lambda b, j: (b, j, 0)),
            pl.BlockSpec((None, E, nblk, tb), lambda b, j: (b, 0, 0, 0)),
        ],
        out_specs=pl.BlockSpec((None, E, cap, D), lambda b, j: (b, 0, 0, 0)),
        out_shape=jax.ShapeDtypeStruct((B, E, cap, D), BF16),
        compiler_params=_cparams("arbitrary", "arbitrary"),
        name="gather",
    )(lists, h2, rank4)


def _ffn_kernel(xs_ref, wg_ref, wu_ref, wd_ref, y_ref, wg_b, wu_b, wd_b):
    @pl.when(pl.program_id(1) == 0)
    def _():
        wg_b[...] = wg_ref[...].astype(BF16)
        wu_b[...] = wu_ref[...].astype(BF16)
        wd_b[...] = wd_ref[...].astype(BF16)

    nb, cap, D = xs_ref.shape
    xs = xs_ref[...].reshape(nb * cap, D)
    a = _dot(xs, wg_b[...])
    b = _dot(xs, wu_b[...])
    hidden = (a * jax.nn.sigmoid(a) * b).astype(BF16)
    y_ref[...] = _dot(hidden, wd_b[...]).astype(BF16).reshape(nb, cap, D)


def _ffn(xs, wg, wu, wd, layer, nb):
    B, E, cap, D = xs.shape
    F = wg.shape[3]
    blk = pl.BlockSpec((nb, None, cap, D), lambda e, b: (b, e, 0, 0))
    return pl.pallas_call(
        _ffn_kernel,
        grid=(E, B // nb),
        in_specs=[
            blk,
            pl.BlockSpec((None, None, D, F), lambda e, b: (layer, e, 0, 0)),
            pl.BlockSpec((None, None, D, F), lambda e, b: (layer, e, 0, 0)),
            pl.BlockSpec((None, None, F, D), lambda e, b: (layer, e, 0, 0)),
        ],
        out_specs=blk,
        out_shape=jax.ShapeDtypeStruct((B, E, cap, D), BF16),
        scratch_shapes=[pltpu.VMEM((D, F), BF16), pltpu.VMEM((D, F), BF16), pltpu.VMEM((F, D), BF16)],
        compiler_params=_cparams("arbitrary", "arbitrary"),
        name="ffn",
    )(xs, wg, wu, wd)


def _scatter_rows(list_ref, y_ref, rank_ref, aff_ref, x1_ref, out_ref):
    tb = rank_ref.shape[-1]
    step = pl.program_id(1)

    def chunk(k, c, first):
        entries = _chunk_tiles(list_ref, k, c)
        weights = _onehot(entries, rank_ref, step * list_ref.shape[0] + k, aff_ref)
        ys = jnp.concatenate([y_ref[e_t, _tile_rows(g_t), :] for e_t, g_t, _ in entries], axis=0)
        add = lax.dot_general(weights, ys, (((0,), (0,)), ((), ())), preferred_element_type=F32)
        rows = slice(k * tb, (k + 1) * tb)
        out_ref[rows, :] = add + (x1_ref[rows, :] if first else out_ref[rows, :])

    _for_each_chunk(list_ref, chunk)


def _scatter_specs(y, rank4, lists, nblocks):
    B, E, cap, D = y.shape
    _, _, nblk, tb = rank4.shape
    steps = nblk // nblocks
    rows = pl.BlockSpec((nblocks * tb, D), lambda b, j: (b * steps + j, 0))
    per_seq = pl.BlockSpec((None, E, nblk, tb), lambda b, j: (b, 0, 0, 0))
    in_specs = [_list_spec(lists, nblocks), pl.BlockSpec((None, E, cap, D), lambda b, j: (b, 0, 0, 0)),
                per_seq, per_seq, rows]
    return (B, steps), in_specs, rows


def _scatter_kernel(list_ref, y_ref, rank_ref, aff_ref, x1_ref, gout_ref, out_ref, *, norm_out):
    _scatter_rows(list_ref, y_ref, rank_ref, aff_ref, x1_ref, out_ref)
    if norm_out:
        out_ref[...] = _rms(out_ref[...]) * gout_ref[...]


def _scatter(y, rank4, aff4, x1, lists, gout, norm_out):
    grid, in_specs, rows = _scatter_specs(y, rank4, lists, BLOCKS_PER_STEP)
    return pl.pallas_call(
        functools.partial(_scatter_kernel, norm_out=norm_out),
        grid=grid,
        in_specs=in_specs + [pl.BlockSpec(gout.shape, lambda b, j: (0, 0))],
        out_specs=rows,
        out_shape=jax.ShapeDtypeStruct(x1.shape, F32),
        compiler_params=_cparams("arbitrary", "arbitrary"),
        name="scatter",
    )(lists, y, rank4, aff4, x1, gout)


def _rope_tables(S, g, scale):
    rows = S // GRID_W
    row = jnp.broadcast_to(jnp.arange(rows, dtype=F32)[:, None], (rows, GRID_W)).reshape(-1)
    col = jnp.broadcast_to(jnp.arange(GRID_W, dtype=F32)[None, :], (rows, GRID_W)).reshape(-1)
    part = HEAD_DIM // 2
    freqs = ROPE_THETA ** (-jnp.arange(0, part, 2, dtype=F32) / part)
    ang = jnp.stack([row[:, None] * freqs, col[:, None] * freqs], axis=1)
    cos, sin = jnp.cos(ang), jnp.sin(ang)
    c64 = jnp.concatenate([cos[:, 0], cos[:, 0], cos[:, 1], cos[:, 1]], axis=-1)
    s64 = jnp.concatenate([-sin[:, 0], sin[:, 0], -sin[:, 1], sin[:, 1]], axis=-1)
    g_swapped = g.reshape(2, 2, HEAD_DIM // 4)[:, ::-1, :].reshape(HEAD_DIM)
    gc = jnp.tile(c64 * g[None, :], (1, 2)) * scale
    gs = jnp.tile(s64 * g_swapped[None, :], (1, 2)) * scale
    return gc, gs


def _block_diag(blocks):
    n = blocks.shape[0]
    m = blocks.shape[1]
    eye = jnp.eye(n, dtype=blocks.dtype)
    return (eye[:, None, :, None] * blocks[:, :, None, :]).reshape(n * m, n * blocks.shape[2])


def kernel(x, g_mix, w_in, g_q, g_k, g_sgu, w_spatial, b_spatial, w_pool, pool_scale, w_attn_o,
           w_sgu_o, w_pool_o, w_out, g_ffn, w_router, w_gate_e, w_up_e, w_down_e, g_final):
    B, S, D = x.shape
    depth = w_in.shape[0]
    cap = CAPACITY_FACTOR * S // N_EXPERTS
    x2 = x.reshape(B * S, D)

    bd = _block_diag(jnp.ones((2, HEAD_DIM, HEAD_DIM), F32)).astype(BF16)
    k_i = lax.broadcasted_iota(jnp.int32, (LANES, LANES), 0)
    l_i = lax.broadcasted_iota(jnp.int32, (LANES, LANES), 1)
    tri = jnp.concatenate([(k_i < l_i).astype(BF16), jnp.ones((LANES, LANES), BF16)], axis=1)

    for l in range(depth):
        gcq, gsq = _rope_tables(S, g_q[l], HEAD_DIM ** -0.5 * math.log2(math.e))
        gck, gsk = _rope_tables(S, g_k[l], 1.0)
        w_gl = w_in[l, :, A_W:].astype(BF16)
        gmix = g_mix[l][None, :]
        proj = (gmix, w_in[l, :, :A_W].astype(BF16), gcq, gsq, gck, gsk, bd, g_sgu[l][None, :])
        q, k, v, u, vn, p = _inproj(x2, proj, S, INPROJ_ROWS)
        attn = _attention(q, k, v, B, S, ATTN_QUERY_SUBS * QUERY_SUB)

        wsp = w_spatial[l].transpose(1, 0, 2).reshape(SGU_CHUNK, SGU_GROUPS * SGU_CHUNK).astype(BF16)
        bsp = jnp.repeat(b_spatial[l].T, SGU_HEAD, axis=1)
        wpool = _block_diag(w_pool[l]).astype(BF16)
        wr = jnp.pad(w_router[l], ((0, 0), (0, LANES - N_EXPERTS))).astype(BF16)
        x1, h2, lg = _merge(
            x2, gmix, w_gl, attn, u, vn, p, wsp, bsp, wpool, pool_scale[l][None, :],
            w_attn_o[l].astype(BF16), w_sgu_o[l].astype(BF16), w_pool_o[l].astype(BF16),
            w_out[l].astype(BF16), g_ffn[l][None, :], wr, S, MERGE_ROWS)

        rank, aff, lists = _route(lg.reshape(B, S, LANES), tri, cap, math.gcd(B, SEQS_PER_STEP))
        per_block = (B, N_EXPERTS, S // TOKEN_BLOCK, TOKEN_BLOCK)
        rank4, aff4 = rank.reshape(per_block), aff.reshape(per_block)
        xs = _gather(h2.reshape(B, S, D), rank4, lists, cap)
        y = _ffn(xs, w_gate_e, w_up_e, w_down_e, l, math.gcd(B, SEQS_PER_STEP))
        x2 = _scatter(y, rank4, aff4, x1, lists, g_final[None, :], norm_out=(l == depth - 1))

    return x2.reshape(B, S, D)
```

```python
import functools
import math

import jax
import jax.numpy as jnp
from jax import lax
from jax.experimental import pallas as pl
from jax.experimental.pallas import tpu as pltpu

F32 = jnp.float32
BF16 = jnp.bfloat16

GRID_W = 64
N_HEADS = 8
N_KV_HEADS = 2
HEAD_DIM = 64
ROPE_THETA = 10000.0
SGU_GROUPS = 4
SGU_HEAD = 64
SGU_W = SGU_GROUPS * SGU_HEAD
SGU_CHUNK = 128
POOL_WINDOWS = (2, 4, 8, 16)
POOL_HEAD = 64
POOL_W = len(POOL_WINDOWS) * POOL_HEAD
POOL_HALO = 8
KEY_CHUNK = 512
QUERY_SUB = 512
TOKEN_BLOCK = 256
BLOCKS_PER_STEP = 4
INPROJ_ROWS = 1024
MERGE_ROWS = 1024
ATTN_QUERY_SUBS = 2
SEQS_PER_STEP = 4
TILE_SHIFT = 4
SLOT_TILE = 1 << TILE_SHIFT
CHUNK_SHIFT = 6
CHUNK_TILES = 1 << CHUNK_SHIFT
NO_SLOT = -1024
Q_W = N_HEADS * HEAD_DIM
KV_W = N_KV_HEADS * HEAD_DIM
N_BRANCHES = 3
N_EXPERTS = 16
CAPACITY_FACTOR = 2
EPS = 1e-6

LANES = 128
F32_SUBLANES = 8
A_W = Q_W + 2 * KV_W + 2 * SGU_W + POOL_W
VMEM_LIMIT = 56 * 1024 * 1024


def _cparams(*sem):
    return pltpu.CompilerParams(dimension_semantics=sem, vmem_limit_bytes=VMEM_LIMIT)


def _dot(a, b):
    return jnp.dot(a, b, preferred_element_type=F32)


def _dot_nt(a, b):
    return lax.dot_general(a, b, (((1,), (1,)), ((), ())), preferred_element_type=F32)


def _rms(x):
    return x * lax.rsqrt(jnp.mean(x * x, axis=-1, keepdims=True) + EPS)


def _head_norm_rope(xq, gc, gs, bd):
    sq = xq * xq
    hi = sq.astype(BF16)
    lo = (sq - hi.astype(F32)).astype(BF16)
    ss = _dot(hi, bd) + _dot(lo, bd)
    r = lax.rsqrt(ss * (1.0 / HEAD_DIM) + EPS)
    lane = lax.broadcasted_iota(jnp.int32, xq.shape, 1)
    first_half = (lane % 32) < 16
    partner = jnp.where(first_half, pltpu.roll(xq, LANES - 16, 1), pltpu.roll(xq, 16, 1))
    return r * (xq * gc + partner * gs)


def _project_rows(x, gmix_ref, w_ref, gcq_ref, gsq_ref, gck_ref, gsk_ref, bd_ref, gsgu_ref,
                  q_ref, k_ref, v_ref, u_ref, vn_ref, p_ref):
    h = (_rms(x) * gmix_ref[...]).astype(BF16)
    acc = _dot(h, w_ref[...])
    bd = bd_ref[...]
    for c in range(Q_W // LANES):
        sl = slice(c * LANES, (c + 1) * LANES)
        q_ref[:, sl] = _head_norm_rope(acc[:, sl], gcq_ref[...], gsq_ref[...], bd).astype(BF16)
    o = Q_W
    k_ref[...] = _head_norm_rope(acc[:, o:o + KV_W], gck_ref[...], gsk_ref[...], bd).astype(BF16)
    o += KV_W
    v_ref[...] = acc[:, o:o + KV_W].astype(BF16)
    o += KV_W
    u_ref[...] = acc[:, o:o + SGU_W]
    o += SGU_W
    vn_ref[...] = (_rms(acc[:, o:o + SGU_W]) * gsgu_ref[...]).astype(BF16)
    o += SGU_W
    p_ref[...] = acc[:, o:o + POOL_W]


def _inproj_kernel(x_ref, *refs):
    _project_rows(x_ref[...], *refs)


def _project_specs(T, D, tm, row, pos, const):
    in_specs = [
        pl.BlockSpec((1, D), const),
        pl.BlockSpec((D, A_W), const),
        pl.BlockSpec((tm, LANES), pos),
        pl.BlockSpec((tm, LANES), pos),
        pl.BlockSpec((tm, LANES), pos),
        pl.BlockSpec((tm, LANES), pos),
        pl.BlockSpec((LANES, LANES), const),
        pl.BlockSpec((1, SGU_W), const),
    ]
    widths = ((Q_W, BF16), (KV_W, BF16), (KV_W, BF16), (SGU_W, F32), (SGU_W, BF16), (POOL_W, F32))
    out_specs = [pl.BlockSpec((tm, w), row) for w, _ in widths]
    out_shape = [jax.ShapeDtypeStruct((T, w), dt) for w, dt in widths]
    return in_specs, out_specs, out_shape


def _inproj(x2, params, S, tm):
    T, D = x2.shape
    nseq = S // tm
    row = lambda i: (i, 0)
    in_specs, out_specs, out_shape = _project_specs(
        T, D, tm, row, lambda i: (i % nseq, 0), lambda i: (0, 0))
    return pl.pallas_call(
        _inproj_kernel,
        grid=(T // tm,),
        in_specs=[pl.BlockSpec((tm, D), row)] + in_specs,
        out_specs=out_specs,
        out_shape=out_shape,
        compiler_params=_cparams("parallel"),
        name="inproj",
    )(x2, *params)


def _attn_kernel(q_ref, k_ref, v_ref, o_ref, kz_ref, vzt_ref, s_ref, qt_ref):
    @pl.when(pl.program_id(1) == 0)
    def _():
        lane = lax.broadcasted_iota(jnp.int32, k_ref.shape, 1)
        low = lane < HEAD_DIM
        k = k_ref[...].astype(F32)
        kr = pltpu.roll(k, HEAD_DIM, 1)
        kz_ref[0] = jnp.where(low, k, 0.0).astype(BF16)
        kz_ref[1] = jnp.where(low, 0.0, kr).astype(BF16)
        kz_ref[2] = jnp.where(low, kr, 0.0).astype(BF16)
        kz_ref[3] = jnp.where(low, 0.0, k).astype(BF16)
        vt = v_ref[...].astype(F32).T
        vtr = pltpu.roll(vt, HEAD_DIM, 0)
        row = lax.broadcasted_iota(jnp.int32, vt.shape, 0)
        top = row < HEAD_DIM
        ones_lo = jnp.where(row == 0, 1.0, 0.0)
        ones_hi = jnp.where(row == HEAD_DIM, 1.0, 0.0)
        vzt_ref[0] = jnp.where(top, vt, ones_hi).astype(BF16)
        vzt_ref[1] = jnp.where(top, ones_lo, vtr).astype(BF16)
        vzt_ref[2] = jnp.where(top, vtr, ones_hi).astype(BF16)
        vzt_ref[3] = jnp.where(top, ones_lo, vt).astype(BF16)

    tq = s_ref.shape[2]
    S = k_ref.shape[0]
    nchunk = S // KEY_CHUNK
    sub = F32_SUBLANES
    slot0 = jnp.minimum(pl.program_id(1), 0)

    def unit_parts(unit):
        qt, head = divmod(unit, N_HEADS)
        pair, parity = divmod(head, 2)
        kvi = 2 * ((2 * pair) // (N_HEADS // N_KV_HEADS)) + parity
        return slice(qt * tq, (qt + 1) * tq), slice(pair * LANES, (pair + 1) * LANES), parity, kvi

    npair = N_HEADS // 2
    for qt in range(q_ref.shape[0] // tq):
        for pair in range(npair):
            q_pair = q_ref[qt * tq:(qt + 1) * tq, pair * LANES:(pair + 1) * LANES]
            qt_ref[qt * npair + pair] = q_pair.astype(F32).T.astype(BF16)

    def scores(unit, c, mrun):
        kvi = unit_parts(unit)[3]
        rows = slice(c * KEY_CHUNK, (c + 1) * KEY_CHUNK)
        s = _dot(kz_ref[kvi, rows, :], qt_ref[unit // 2])
        s_ref[slot0 + unit % 2, rows, :] = s
        m = jnp.max(s.reshape(KEY_CHUNK // sub, sub, tq), axis=0)
        return m if mrun is None else jnp.maximum(mrun, m)

    def weighted(unit, c, m8, acc):
        kvi = unit_parts(unit)[3]
        rows = slice(c * KEY_CHUNK, (c + 1) * KEY_CHUNK)
        s3 = s_ref[slot0 + unit % 2, rows, :].reshape(KEY_CHUNK // sub, sub, tq)
        e = jnp.exp2(s3 - m8[None]).reshape(KEY_CHUNK, tq).astype(BF16)
        d = _dot(vzt_ref[kvi, :, rows], e)
        return d if acc is None else acc + d

    nunit = (q_ref.shape[0] // tq) * N_HEADS
    top = lax.broadcasted_iota(jnp.int32, (LANES, tq), 0) < HEAD_DIM
    mrun = None
    for c in range(nchunk):
        mrun = scores(0, c, mrun)
    even = None
    for unit in range(nunit):
        m8 = jnp.broadcast_to(jnp.max(mrun, axis=0, keepdims=True), (sub, tq))
        mnext, acc = None, None
        for c in range(nchunk):
            if unit + 1 < nunit:
                mnext = scores(unit + 1, c, mnext)
            acc = weighted(unit, c, m8, acc)
        mrun = mnext
        qrows, lanes, parity, _ = unit_parts(unit)
        if parity == 0:
            even = acc * (1.0 / acc[HEAD_DIM:HEAD_DIM + 1])
        else:
            odd = acc * (1.0 / acc[0:1])
            o_ref[qrows, lanes] = jnp.where(top, even, odd).T.astype(BF16)


def _attention(q, k, v, B, S, tq):
    T = q.shape[0]
    nq = S // tq
    return pl.pallas_call(
        _attn_kernel,
        grid=(B, nq),
        in_specs=[
            pl.BlockSpec((tq, Q_W), lambda b, i: (b * nq + i, 0)),
            pl.BlockSpec((S, KV_W), lambda b, i: (b, 0)),
            pl.BlockSpec((S, KV_W), lambda b, i: (b, 0)),
        ],
        out_specs=pl.BlockSpec((tq, Q_W), lambda b, i: (b * nq + i, 0)),
        out_shape=jax.ShapeDtypeStruct((T, Q_W), BF16),
        scratch_shapes=[pltpu.VMEM((4, S, KV_W), BF16), pltpu.VMEM((4, KV_W, S), BF16),
                        pltpu.VMEM((2, S, QUERY_SUB), F32),
                        pltpu.VMEM((tq // QUERY_SUB * (N_HEADS // 2), LANES, QUERY_SUB), BF16)],
        compiler_params=_cparams("parallel", "arbitrary"),
        name="attention",
    )(q, k, v)


def _merge_kernel(x_ref, gmix_ref, wgl_ref, attn_ref, u_ref, vn_ref, p_ref, pprev_ref, pnext_ref,
                  wsp_ref, bsp_ref, wpool_ref, pscale_ref, wao_ref, wso_ref, wpo_ref, wout_ref,
                  gffn_ref, wr_ref, x1_ref, h2_ref, lg_ref, *, S):
    ts = x_ref.shape[0]

    vn = vn_ref[...]
    lane_b = lax.broadcasted_iota(jnp.int32, (SGU_CHUNK, SGU_W), 1) // SGU_HEAD
    zs = []
    for c in range(ts // SGU_CHUNK):
        vc = vn[c * SGU_CHUNK:(c + 1) * SGU_CHUNK]
        stacked = jnp.concatenate(
            [jnp.where(lane_b == g, vc, jnp.zeros_like(vc)) for g in range(SGU_GROUPS)], axis=0)
        zs.append(_dot(wsp_ref[...], stacked) + bsp_ref[...])
    sgu = (u_ref[...] * jnp.concatenate(zs, axis=0)).astype(BF16)

    pos0 = (pl.program_id(0) % (S // ts)) * ts
    p = p_ref[...]
    prev = jnp.where(pos0 == 0, 0.0, pprev_ref[...])
    nxt = jnp.where(pos0 + ts == S, 0.0, pnext_ref[...])
    ext = jnp.concatenate([prev, p, nxt], axis=0)
    n = ts + 2 * POOL_HALO
    s2 = ext + pltpu.roll(ext, 1, 0)
    s4 = pltpu.roll(s2, 1, 0) + pltpu.roll(s2, n - 1, 0)
    s8 = pltpu.roll(s4, 2, 0) + pltpu.roll(s4, n - 2, 0)
    s16 = pltpu.roll(s8, 4, 0) + pltpu.roll(s8, n - 4, 0)
    mid = slice(POOL_HALO, POOL_HALO + ts)
    grp = lax.broadcasted_iota(jnp.int32, (ts, POOL_W), 1) // POOL_HEAD
    t = pos0 + lax.broadcasted_iota(jnp.int32, (ts, POOL_W), 0)
    half = jnp.left_shift(1, grp)
    cnt = jnp.minimum(t + half - 1, S - 1) - jnp.maximum(t - half, 0) + 1
    wsum = jnp.where(grp == 0, s2[mid], jnp.where(grp == 1, s4[mid],
                                                  jnp.where(grp == 2, s8[mid], s16[mid])))
    d = (wsum / cnt.astype(F32) - p).astype(BF16)
    pooled = (_dot(d, wpool_ref[...]) * pscale_ref[...]).astype(BF16)

    x = x_ref[...]
    D = x.shape[1]
    h = (_rms(x) * gmix_ref[...]).astype(BF16)
    br_a = _dot(attn_ref[...], wao_ref[...])
    br_b = _dot(sgu, wso_ref[...])
    br_c = _dot(pooled, wpo_ref[...])
    gl = _dot(h, wgl_ref[...])
    gate = lambda j: 0.5 * (1.0 + jnp.tanh(0.5 * gl[:, j * D:(j + 1) * D]))
    merged = (gate(0) * br_a + gate(1) * br_b + gate(2) * br_c).astype(BF16)
    x1 = x + _dot(merged, wout_ref[...])
    x1_ref[...] = x1
    h2 = (_rms(x1) * gffn_ref[...]).astype(BF16)
    h2_ref[...] = h2
    lg_ref[...] = _dot(h2, wr_ref[...])


def _merge(x2, gmix, w_gl, attn, u, vn, p, wsp, bsp, wpool, pscale, wao, wso, wpo, wout, gffn, wr,
           S, ts):
    T, D = x2.shape
    row = lambda i: (i, 0)
    const = lambda i: (0, 0)
    hb = ts // POOL_HALO
    nhb = T // POOL_HALO
    full = lambda a: pl.BlockSpec(a.shape, const)
    return pl.pallas_call(
        functools.partial(_merge_kernel, S=S),
        grid=(T // ts,),
        in_specs=[
            pl.BlockSpec((ts, D), row),
            full(gmix),
            full(w_gl),
            pl.BlockSpec((ts, Q_W), row),
            pl.BlockSpec((ts, SGU_W), row),
            pl.BlockSpec((ts, SGU_W), row),
            pl.BlockSpec((ts, POOL_W), row),
            pl.BlockSpec((POOL_HALO, POOL_W), lambda i: (jnp.maximum(i * hb - 1, 0), 0)),
            pl.BlockSpec((POOL_HALO, POOL_W), lambda i: (jnp.minimum((i + 1) * hb, nhb - 1), 0)),
            full(wsp), full(bsp), full(wpool), full(pscale), full(wao), full(wso), full(wpo),
            full(wout), full(gffn), full(wr),
        ],
        out_specs=[
            pl.BlockSpec((ts, D), row),
            pl.BlockSpec((ts, D), row),
            pl.BlockSpec((ts, LANES), row),
        ],
        out_shape=[
            jax.ShapeDtypeStruct((T, D), F32),
            jax.ShapeDtypeStruct((T, D), BF16),
            jax.ShapeDtypeStruct((T, LANES), F32),
        ],
        compiler_params=_cparams("parallel"),
        name="merge",
    )(x2, gmix, w_gl, attn, u, vn, p, p, p, wsp, bsp, wpool, pscale, wao, wso, wpo, wout, gffn, wr)


def _excl_cumsum(m01, tri_ones):
    off = jnp.zeros((m01.shape[0], LANES), F32)
    pieces = []
    for j in range(m01.shape[1] // LANES):
        res = _dot(m01[:, j * LANES:(j + 1) * LANES].astype(BF16), tri_ones)
        pieces.append(res[:, :LANES] + off)
        off = off + res[:, LANES:]
    return jnp.concatenate(pieces, axis=1)


def _tile_lists(offs, nblk, list_len):
    E = offs.shape[0]
    lane = lax.broadcasted_iota(jnp.int32, offs.shape, 1)
    hi = pltpu.roll(offs, LANES - 1, 1)
    g_lo = jnp.floor(offs * (1.0 / SLOT_TILE))
    g_hi = jnp.floor((hi + (SLOT_TILE - 1)) * (1.0 / SLOT_TILE))
    cnt = jnp.where((hi > offs) & (lane < nblk), g_hi - g_lo, 0.0)
    below = (lax.broadcasted_iota(jnp.int32, (E, E), 1) < lax.broadcasted_iota(jnp.int32, (E, E), 0))
    base = _dot(jnp.where(below, 1.0, 0.0).astype(BF16), cnt.astype(BF16))
    end = base + cnt
    t = lax.broadcasted_iota(jnp.int32, (1, list_len), 1).astype(F32)
    row_e = lax.broadcasted_iota(jnp.int32, (E, list_len), 0).astype(F32)
    blocks = []
    for j in range(nblk):
        col = slice(j, j + 1)
        e_t = jnp.sum(jnp.where(end[:, col] <= t, 1.0, 0.0), axis=0, keepdims=True)
        shift = jnp.sum(jnp.where(row_e == e_t, g_lo[:, col] - base[:, col], 0.0), axis=0, keepdims=True)
        valid = e_t < E
        g_t = jnp.where(valid, t + shift, 0.0)
        total = jnp.broadcast_to(end[E - 1:E, col], (1, list_len))
        rows = [jnp.where(valid, e_t, 0.0), g_t, jnp.where(valid, g_t * SLOT_TILE, float(NO_SLOT)),
                total, jnp.zeros((4, list_len), F32)]
        blocks.append(jnp.concatenate(rows, axis=0).astype(jnp.int32)[None])
    return jnp.concatenate(blocks, axis=0)


def _route_kernel(lg_ref, tri_ref, rank_ref, aff_ref, lists_ref, *, cap):
    nb, S, _ = lg_ref.shape
    E = N_EXPERTS
    rows = []
    for b in range(nb):
        lt = lg_ref[b].T[:E]
        e = jnp.exp(lt - jnp.max(lt, axis=0, keepdims=True))
        rows.append(e / jnp.sum(e, axis=0, keepdims=True))
    aff = jnp.concatenate(rows, axis=0)

    def refine(i, thr):
        cand = thr | jnp.left_shift(jnp.int32(1), 30 - i)
        cnt = jnp.sum(jnp.where(aff >= pltpu.bitcast(cand, F32), 1.0, 0.0), axis=1, keepdims=True)
        return jnp.where(cnt >= cap, cand, thr)

    thr = pltpu.bitcast(lax.fori_loop(0, 31, refine, jnp.zeros((nb * E, 1), jnp.int32)), F32)
    gt = jnp.where(aff > thr, 1.0, 0.0)
    eq = jnp.where(aff == thr, 1.0, 0.0)
    need = cap - jnp.sum(gt, axis=1, keepdims=True)
    tri = tri_ref[...]
    sel = gt + eq * jnp.where(_excl_cumsum(eq, tri) < need, 1.0, 0.0)
    before = _excl_cumsum(sel, tri)
    rank = jnp.where(sel > 0.5, before, -1.0)
    nblk = S // TOKEN_BLOCK
    lane = lax.broadcasted_iota(jnp.int32, (nb * E, LANES), 1)
    offs = jnp.where(lane == nblk, float(cap), 0.0)
    for j in range(nblk):
        offs = jnp.where(lane == j, before[:, j * TOKEN_BLOCK:j * TOKEN_BLOCK + 1], offs)
    list_len = lists_ref.shape[-1]
    for b in range(nb):
        rank_ref[b] = rank[b * E:(b + 1) * E]
        aff_ref[b] = aff[b * E:(b + 1) * E]
        lists_ref[b] = _tile_lists(offs[b * E:(b + 1) * E], nblk, list_len)


def _list_len(cap):
    worst = N_EXPERTS * (cap // SLOT_TILE + 2)
    return -(-worst // LANES) * LANES


def _route(lg3, tri, cap, nb):
    B, S, _ = lg3.shape
    nblk = S // TOKEN_BLOCK
    blk = lambda b: (b, 0, 0)
    lists_shape = (nblk, 8, _list_len(cap))
    return pl.pallas_call(
        functools.partial(_route_kernel, cap=cap),
        grid=(B // nb,),
        in_specs=[pl.BlockSpec((nb, S, LANES), blk), pl.BlockSpec(tri.shape, lambda b: (0, 0))],
        out_specs=[
            pl.BlockSpec((nb, N_EXPERTS, S), blk),
            pl.BlockSpec((nb, N_EXPERTS, S), blk),
            pl.BlockSpec((nb,) + lists_shape, lambda b: (b, 0, 0, 0)),
        ],
        out_shape=[
            jax.ShapeDtypeStruct((B, N_EXPERTS, S), F32),
            jax.ShapeDtypeStruct((B, N_EXPERTS, S), F32),
            jax.ShapeDtypeStruct((B,) + lists_shape, jnp.int32),
        ],
        compiler_params=_cparams("parallel"),
        name="route",
    )(lg3, tri)


def _chunk_tiles(list_ref, k, c):
    out = []
    for i in range(CHUNK_TILES):
        t = c * CHUNK_TILES + i
        out.append((list_ref[k, 0, t], list_ref[k, 1, t], list_ref[k, 2, t]))
    return out


def _num_chunks(list_ref, k):
    return jnp.right_shift(list_ref[k, 3, 0] + CHUNK_TILES - 1, CHUNK_SHIFT)


def _for_each_chunk(list_ref, body):
    for k in range(list_ref.shape[0]):
        body(k, 0, True)
    for k in range(list_ref.shape[0]):
        def rest(c, carry, k=k):
            body(k, c, False)
            return carry
        lax.fori_loop(1, _num_chunks(list_ref, k), rest, 0)


def _tile_rows(g_t):
    return pl.ds(pl.multiple_of(g_t * SLOT_TILE, SLOT_TILE), SLOT_TILE)


def _onehot(entries, rank_ref, j, weight_ref=None):
    tb = rank_ref.shape[-1]
    sub = lax.broadcasted_iota(jnp.int32, (SLOT_TILE, tb), 0)
    tiles = []
    for e_t, _, slot0 in entries:
        hit = rank_ref[e_t, pl.ds(j, 1), :] == (slot0 + sub).astype(F32)
        val = 1.0 if weight_ref is None else weight_ref[e_t, pl.ds(j, 1), :]
        tiles.append(jnp.where(hit, val, 0.0).astype(BF16))
    return jnp.concatenate(tiles, axis=0)


def _gather_kernel(list_ref, h2_ref, rank_ref, xs_ref):
    tb = rank_ref.shape[-1]
    step = pl.program_id(1)

    @pl.when(step == 0)
    def _():
        xs_ref[...] = jnp.zeros_like(xs_ref)

    def chunk(k, c, first):
        entries = _chunk_tiles(list_ref, k, c)
        onehot = _onehot(entries, rank_ref, step * list_ref.shape[0] + k)
        rows = _dot(onehot, h2_ref[k * tb:(k + 1) * tb, :]).astype(BF16)
        for i, (e_t, g_t, _) in enumerate(entries):
            xs_ref[e_t, _tile_rows(g_t), :] += rows[i * SLOT_TILE:(i + 1) * SLOT_TILE, :]

    _for_each_chunk(list_ref, chunk)


def _list_spec(lists, nblocks):
    return pl.BlockSpec((None, nblocks) + lists.shape[2:], lambda b, j: (b, j, 0, 0),
                        memory_space=pltpu.SMEM)


def _gather(h2, rank4, lists, cap):
    B, S, D = h2.shape
    _, E, nblk, tb = rank4.shape
    rows = BLOCKS_PER_STEP * tb
    return pl.pallas_call(
        _gather_kernel,
        grid=(B, nblk // BLOCKS_PER_STEP),
        in_specs=[
            _list_spec(lists, BLOCKS_PER_STEP),
            pl.BlockSpec((None, rows, D), lambda b, j: (b, j, 0)),
            pl.BlockSpec((None, E, nblk, tb), lambda b, j: (b, 0, 0, 0)),
        ],
        out_specs=pl.BlockSpec((None, E, cap, D), lambda b, j: (b, 0, 0, 0)),
        out_shape=jax.ShapeDtypeStruct((B, E, cap, D), BF16),
        compiler_params=_cparams("arbitrary", "arbitrary"),
        name="gather",
    )(lists, h2, rank4)


def _ffn_kernel(xs_ref, wg_ref, wu_ref, wd_ref, y_ref, wg_b, wu_b, wd_b):
    @pl.when(pl.program_id(1) == 0)
    def _():
        wg_b[...] = wg_ref[...].astype(BF16)
        wu_b[...] = wu_ref[...].astype(BF16)
        wd_b[...] = wd_ref[...].astype(BF16)

    nb, cap, D = xs_ref.shape
    xs = xs_ref[...].reshape(nb * cap, D)
    a = _dot(xs, wg_b[...])
    b = _dot(xs, wu_b[...])
    hidden = (a * jax.nn.sigmoid(a) * b).astype(BF16)
    y_ref[...] = _dot(hidden, wd_b[...]).astype(BF16).reshape(nb, cap, D)


def _ffn(xs, wg, wu, wd, layer, nb):
    B, E, cap, D = xs.shape
    F = wg.shape[3]
    blk = pl.BlockSpec((nb, None, cap, D), lambda e, b: (b, e, 0, 0))
    once = pl.Buffered(1)
    return pl.pallas_call(
        _ffn_kernel,
        grid=(E, B // nb),
        in_specs=[
            blk,
            pl.BlockSpec((None, None, D, F), lambda e, b: (layer, e, 0, 0), pipeline_mode=once),
            pl.BlockSpec((None, None, D, F), lambda e, b: (layer, e, 0, 0), pipeline_mode=once),
            pl.BlockSpec((None, None, F, D), lambda e, b: (layer, e, 0, 0), pipeline_mode=once),
        ],
        out_specs=blk,
        out_shape=jax.ShapeDtypeStruct((B, E, cap, D), BF16),
        scratch_shapes=[pltpu.VMEM((D, F), BF16), pltpu.VMEM((D, F), BF16), pltpu.VMEM((F, D), BF16)],
        compiler_params=_cparams("arbitrary", "arbitrary"),
        name="ffn",
    )(xs, wg, wu, wd)


def _scatter_rows(list_ref, y_ref, rank_ref, aff_ref, x1_ref, out_ref):
    tb = rank_ref.shape[-1]
    step = pl.program_id(1)

    def chunk(k, c, first):
        entries = _chunk_tiles(list_ref, k, c)
        weights = _onehot(entries, rank_ref, step * list_ref.shape[0] + k, aff_ref)
        ys = jnp.concatenate([y_ref[e_t, _tile_rows(g_t), :] for e_t, g_t, _ in entries], axis=0)
        add = lax.dot_general(weights, ys, (((0,), (0,)), ((), ())), preferred_element_type=F32)
        rows = slice(k * tb, (k + 1) * tb)
        out_ref[rows, :] = add + (x1_ref[rows, :] if first else out_ref[rows, :])

    _for_each_chunk(list_ref, chunk)


def _scatter_specs(y, rank4, lists, nblocks):
    B, E, cap, D = y.shape
    _, _, nblk, tb = rank4.shape
    steps = nblk // nblocks
    rows = pl.BlockSpec((nblocks * tb, D), lambda b, j: (b * steps + j, 0))
    per_seq = pl.BlockSpec((None, E, nblk, tb), lambda b, j: (b, 0, 0, 0))
    in_specs = [_list_spec(lists, nblocks), pl.BlockSpec((None, E, cap, D), lambda b, j: (b, 0, 0, 0)),
                per_seq, per_seq, rows]
    return (B, steps), in_specs, rows


def _scatter_kernel(list_ref, y_ref, rank_ref, aff_ref, x1_ref, gout_ref, out_ref, *, norm_out):
    _scatter_rows(list_ref, y_ref, rank_ref, aff_ref, x1_ref, out_ref)
    if norm_out:
        out_ref[...] = _rms(out_ref[...]) * gout_ref[...]


def _scatter(y, rank4, aff4, x1, lists, gout, norm_out):
    grid, in_specs, rows = _scatter_specs(y, rank4, lists, BLOCKS_PER_STEP)
    return pl.pallas_call(
        functools.partial(_scatter_kernel, norm_out=norm_out),
        grid=grid,
        in_specs=in_specs + [pl.BlockSpec(gout.shape, lambda b, j: (0, 0))],
        out_specs=rows,
        out_shape=jax.ShapeDtypeStruct(x1.shape, F32),
        compiler_params=_cparams("arbitrary", "arbitrary"),
        name="scatter",
    )(lists, y, rank4, aff4, x1, gout)


def _rope_tables(S, g, scale):
    rows = S // GRID_W
    row = jnp.broadcast_to(jnp.arange(rows, dtype=F32)[:, None], (rows, GRID_W)).reshape(-1)
    col = jnp.broadcast_to(jnp.arange(GRID_W, dtype=F32)[None, :], (rows, GRID_W)).reshape(-1)
    part = HEAD_DIM // 2
    freqs = ROPE_THETA ** (-jnp.arange(0, part, 2, dtype=F32) / part)
    ang = jnp.stack([row[:, None] * freqs, col[:, None] * freqs], axis=1)
    cos, sin = jnp.cos(ang), jnp.sin(ang)
    c64 = jnp.concatenate([cos[:, 0], cos[:, 0], cos[:, 1], cos[:, 1]], axis=-1)
    s64 = jnp.concatenate([-sin[:, 0], sin[:, 0], -sin[:, 1], sin[:, 1]], axis=-1)
    g_swapped = g.reshape(2, 2, HEAD_DIM // 4)[:, ::-1, :].reshape(HEAD_DIM)
    gc = jnp.tile(c64 * g[None, :], (1, 2)) * scale
    gs = jnp.tile(s64 * g_swapped[None, :], (1, 2)) * scale
    return gc, gs


def _block_diag(blocks):
    n = blocks.shape[0]
    m = blocks.shape[1]
    eye = jnp.eye(n, dtype=blocks.dtype)
    return (eye[:, None, :, None] * blocks[:, :, None, :]).reshape(n * m, n * blocks.shape[2])


def kernel(x, g_mix, w_in, g_q, g_k, g_sgu, w_spatial, b_spatial, w_pool, pool_scale, w_attn_o,
           w_sgu_o, w_pool_o, w_out, g_ffn, w_router, w_gate_e, w_up_e, w_down_e, g_final):
    B, S, D = x.shape
    depth = w_in.shape[0]
    cap = CAPACITY_FACTOR * S // N_EXPERTS
    x2 = x.reshape(B * S, D)

    bd = _block_diag(jnp.ones((2, HEAD_DIM, HEAD_DIM), F32)).astype(BF16)
    k_i = lax.broadcasted_iota(jnp.int32, (LANES, LANES), 0)
    l_i = lax.broadcasted_iota(jnp.int32, (LANES, LANES), 1)
    tri = jnp.concatenate([(k_i < l_i).astype(BF16), jnp.ones((LANES, LANES), BF16)], axis=1)

    for l in range(depth):
        gcq, gsq = _rope_tables(S, g_q[l], HEAD_DIM ** -0.5 * math.log2(math.e))
        gck, gsk = _rope_tables(S, g_k[l], 1.0)
        w_gl = w_in[l, :, A_W:].astype(BF16)
        gmix = g_mix[l][None, :]
        proj = (gmix, w_in[l, :, :A_W].astype(BF16), gcq, gsq, gck, gsk, bd, g_sgu[l][None, :])
        q, k, v, u, vn, p = _inproj(x2, proj, S, INPROJ_ROWS)
        attn = _attention(q, k, v, B, S, ATTN_QUERY_SUBS * QUERY_SUB)

        wsp = w_spatial[l].transpose(1, 0, 2).reshape(SGU_CHUNK, SGU_GROUPS * SGU_CHUNK).astype(BF16)
        bsp = jnp.repeat(b_spatial[l].T, SGU_HEAD, axis=1)
        wpool = _block_diag(w_pool[l]).astype(BF16)
        wr = jnp.pad(w_router[l], ((0, 0), (0, LANES - N_EXPERTS))).astype(BF16)
        x1, h2, lg = _merge(
            x2, gmix, w_gl, attn, u, vn, p, wsp, bsp, wpool, pool_scale[l][None, :],
            w_attn_o[l].astype(BF16), w_sgu_o[l].astype(BF16), w_pool_o[l].astype(BF16),
            w_out[l].astype(BF16), g_ffn[l][None, :], wr, S, MERGE_ROWS)

        rank, aff, lists = _route(lg.reshape(B, S, LANES), tri, cap, math.gcd(B, SEQS_PER_STEP))
        per_block = (B, N_EXPERTS, S // TOKEN_BLOCK, TOKEN_BLOCK)
        rank4, aff4 = rank.reshape(per_block), aff.reshape(per_block)
        xs = _gather(h2.reshape(B, S, D), rank4, lists, cap)
        y = _ffn(xs, w_gate_e, w_up_e, w_down_e, l, math.gcd(B, 2 * SEQS_PER_STEP))
        x2 = _scatter(y, rank4, aff4, x1, lists, g_final[None, :], norm_out=(l == depth - 1))

    return x2.reshape(B, S, D)
```
